```python
import jax, jax.numpy as jnp
from jax import lax
import numpy as np

D_MODEL = 1024
BATCH = 16
SEQ = 2048
DEPTH = 4
DEC_BATCH = 128
DEC_SEQ = 1
PAST_LEN = 8192
PAGE_SIZE = 128

N_EVEN = (DEPTH + 1) // 2
N_ODD = DEPTH // 2
EPS = 1e-6
FFN_HIDDEN = 2048
MLA_HEADS = 8
MLA_Q_RANK = 256
MLA_KV_RANK = 128
MLA_NOPE = 64
MLA_ROPE = 32
MLA_V = 64
ROPE_THETA = 10000.0
ATTN_QBLOCK = 128
HGRN_HEADS = 4
HGRN_DK = 128
HGRN_DV = 128
HGRN_CHUNK = 64
MOBA_HEADS = 8
MOBA_KV_HEADS = 4
MOBA_DH = 64
MOBA_BLOCK = 256
MOBA_TOPK = 3
MOBA_QCHUNK = 32
MLSTM_HEADS = 4
MLSTM_DK = 64
MLSTM_DV = 128
MLSTM_CHUNK = 64
CONV_W = 4
HGRN_FD = HGRN_HEADS * HGRN_DK
HGRN_VD = HGRN_HEADS * HGRN_DV
EVEN_SPLITS = (MLA_Q_RANK, MLA_KV_RANK, MLA_ROPE, HGRN_FD, HGRN_FD, HGRN_VD, HGRN_VD)
EVEN_IN = sum(EVEN_SPLITS)
EVEN_MIX = MLA_HEADS * MLA_V + HGRN_VD
MOBA_QD = MOBA_HEADS * MOBA_DH
MOBA_KD = MOBA_KV_HEADS * MOBA_DH
MLSTM_QKD = MLSTM_HEADS * MLSTM_DK
MLSTM_VD = MLSTM_HEADS * MLSTM_DV
ODD_SPLITS = (MOBA_QD, MOBA_KD, MOBA_KD, MLSTM_QKD, MLSTM_QKD, MLSTM_VD, MLSTM_HEADS, MLSTM_HEADS, MLSTM_VD)
ODD_IN = sum(ODD_SPLITS)
ODD_MIX = MOBA_QD + MLSTM_VD
POOL_NUM = 5
POOL_DEN = 4
STATE_NAMES = ('mla', 'hgrn', 'moba_k', 'moba_v', 'mlstm_c', 'mlstm_n', 'mlstm_m', 'mlstm_conv')

kernel_name = 'hybrid_mla_hgrn2_moba_mlstm_decode_step'


def rms_norm(x, g):
    xf = x.astype(jnp.float32)
    y = xf * lax.rsqrt(jnp.mean(xf * xf, axis=-1, keepdims=True) + EPS)
    return (y * g.astype(jnp.float32)).astype(x.dtype)


def swiglu(x, w_in, w_out):
    a, b = jnp.split(x @ w_in, 2, axis=-1)
    return (jax.nn.silu(a) * b) @ w_out


def split_cols(u, sizes):
    outs, start = [], 0
    for s in sizes:
        outs.append(u[..., start:start + s])
        start += s
    return outs


def rope(x, pos):
    half = x.shape[-1] // 2
    freqs = ROPE_THETA ** (-jnp.arange(half, dtype=jnp.float32) / half)
    ang = pos.astype(jnp.float32)[:, None] * freqs[None, :]
    ang = ang.reshape((1, ang.shape[0]) + (1,) * (x.ndim - 3) + (half,))
    cos, sin = jnp.cos(ang), jnp.sin(ang)
    xf = x.astype(jnp.float32)
    x1, x2 = xf[..., :half], xf[..., half:]
    return jnp.concatenate([x1 * cos - x2 * sin, x2 * cos + x1 * sin], axis=-1).astype(x.dtype)


def sweep_queries(fn, q_arrays, q_pos, block):
    B, Q = q_arrays[0].shape[:2]
    qb = min(block, Q)
    nb = -(-Q // qb)
    pad = nb * qb - Q
    def to_blocks(a):
        a = jnp.pad(a, [(0, 0), (0, pad)] + [(0, 0)] * (a.ndim - 2))
        return jnp.swapaxes(a.reshape((B, nb, qb) + a.shape[2:]), 0, 1)
    qp = jnp.pad(q_pos, (0, pad), mode='edge').reshape(nb, qb)
    xs = [to_blocks(a) for a in q_arrays] + [qp]
    out = lax.map(lambda args: fn(*args), xs)
    out = jnp.swapaxes(out, 0, 1)
    return out.reshape((B, nb * qb) + out.shape[3:])[:, :Q]


def chunk_time(arrays, pad_values, chunk):
    L = arrays[0].shape[1]
    c = min(chunk, L)
    n = -(-L // c)
    p = n * c - L
    out = []
    for a, pv in zip(arrays, pad_values):
        a = jnp.pad(a, [(0, 0), (0, p)] + [(0, 0)] * (a.ndim - 2), constant_values=pv)
        out.append(jnp.swapaxes(a.reshape((a.shape[0], n, c) + a.shape[2:]), 0, 1))
    return out, c


def unchunk(o, L):
    o = jnp.swapaxes(o, 0, 1)
    return o.reshape((o.shape[0], o.shape[1] * o.shape[2]) + o.shape[3:])[:, :L]


def mla_attention(q_nope, q_rope, q_pos, c_kv, k_rope, w_uk, w_uv, kn_gain):
    k_nope = rms_norm(jnp.einsum('btr,rhd->bthd', c_kv, w_uk), kn_gain)
    k_pos = jnp.arange(c_kv.shape[1])
    scale = (MLA_NOPE + MLA_ROPE) ** -0.5
    def block(qn, qr, qp):
        s = jnp.einsum('bqhd,bthd->bhqt', qn, k_nope) + jnp.einsum('bqhd,btd->bhqt', qr, k_rope)
        s = jnp.where(k_pos[None, :] <= qp[:, None], s.astype(jnp.float32) * scale, -jnp.inf)
        p = jax.nn.softmax(s, axis=-1).astype(c_kv.dtype)
        o_lat = jnp.einsum('bhqt,btr->bqhr', p, c_kv)
        return jnp.einsum('bqhr,rhd->bqhd', o_lat, w_uv)
    return sweep_queries(block, [q_nope, q_rope], q_pos, ATTN_QBLOCK)


def hgrn2_scan(q, k, logf, v, S0):
    L = q.shape[1]
    (qc, kc, lc, vc), c = chunk_time([q, k, logf, v], [0.0, 0.0, 0.0, 0.0], HGRN_CHUNK)
    causal = jnp.tril(jnp.ones((c, c), bool))
    def step(S, inp):
        qt, kt, lt, vt = inp
        b = jnp.cumsum(lt, axis=1)
        o_inter = jnp.einsum('bthd,bhde->bthe', qt * jnp.exp(b), S)
        diff = b[:, :, None] - b[:, None, :]
        dec = jnp.exp(jnp.where(causal[None, :, :, None, None], diff, -jnp.inf))
        A = jnp.einsum('bthd,btshd->btsh', qt, dec * kt[:, None])
        o_intra = jnp.einsum('btsh,bshe->bthe', A, vt)
        bL = b[:, -1]
        S = jnp.exp(bL)[..., None] * S + jnp.einsum('bshd,bshe->bhde', kt * jnp.exp(bL[:, None] - b), vt)
        return S, o_inter + o_intra
    S, o = lax.scan(step, S0, (qc, kc, lc, vc))
    return unchunk(o, L), S


def mlstm_scan(q, k, v, logi, logf, C0, n0, m0):
    L = q.shape[1]
    (qc, kc, vc, ic, fc), c = chunk_time([q, k, v, logi, logf], [0.0, 0.0, 0.0, -jnp.inf, 0.0], MLSTM_CHUNK)
    causal = jnp.tril(jnp.ones((c, c), bool))
    def step(carry, inp):
        C, n, m = carry
        qt, kt, vt, it, ft = inp
        F = jnp.cumsum(ft, axis=1)
        D = jnp.where(causal[None, :, :, None], F[:, :, None] - F[:, None] + it[:, None], -jnp.inf)
        inter = F + m[:, None]
        mt = jnp.maximum(jnp.max(D, axis=2), inter)
        Wd = jnp.exp(D - mt[:, :, None]) * jnp.einsum('bthd,bshd->btsh', qt, kt)
        a = jnp.exp(inter - mt)
        num = jnp.einsum('btsh,bshe->bthe', Wd, vt) + a[..., None] * jnp.einsum('bthd,bhde->bthe', qt, C)
        den = jnp.sum(Wd, axis=2) + a * jnp.einsum('bthd,bhd->bth', qt, n)
        h = num / jnp.maximum(jnp.abs(den), jnp.exp(-mt))[..., None]
        m_new = mt[:, -1]
        g = jnp.exp(F[:, -1:] - F + it - m_new[:, None])
        decay = jnp.exp(F[:, -1] + m - m_new)
        C = decay[..., None, None] * C + jnp.einsum('bsh,bshd,bshe->bhde', g, kt, vt)
        n = decay[..., None] * n + jnp.einsum('bsh,bshd->bhd', g, kt)
        return (C, n, m_new), h
    (C, n, m), h = lax.scan(step, (C0, n0, m0), (qc, kc, vc, ic, fc))
    return unchunk(h, L), C, n, m


def moba_attention(q, q_pos, kmeans, gather_k, gather_v):
    B, Q, H, Dh = q.shape
    G = H // MOBA_KV_HEADS
    n_cand = kmeans.shape[1]
    R = min(MOBA_TOPK, n_cand)
    own = q_pos // MOBA_BLOCK
    own_b = jnp.broadcast_to(own[None, :, None, None], (B, Q, H, 1)).astype(jnp.int32)
    if R > 0:
        qg = q.reshape(B, Q, MOBA_KV_HEADS, G, Dh).astype(jnp.float32)
        gate = jnp.einsum('bqkgd,bnkd->bqkgn', qg, kmeans).reshape(B, Q, H, n_cand)
        cand_ok = jnp.arange(n_cand)[None, :] < own[:, None]
        gate = jnp.where(cand_ok[None, :, None, :], gate, -jnp.inf)
        _, sel = lax.top_k(gate, R)
        sel_ok = jnp.arange(R)[None, :] < own[:, None]
        sel = jnp.where(sel_ok[None, :, None, :], sel, 0).astype(jnp.int32)
        blk = jnp.concatenate([sel, own_b], axis=-1)
    else:
        blk = own_b
    n_slot = R + 1
    scale = Dh ** -0.5
    def chunk(qc, bc, pc):
        oc = pc // MOBA_BLOCK
        slot_ok = jnp.concatenate([jnp.arange(R)[None, :] < oc[:, None], jnp.ones((pc.shape[0], 1), bool)], axis=1)
        scores = []
        for r in range(n_slot):
            kr = gather_k(bc[..., r])
            s = jnp.einsum('bqhd,bqhkd->bqhk', qc, kr).astype(jnp.float32) * scale
            kpos = bc[..., r, None] * MOBA_BLOCK + jnp.arange(MOBA_BLOCK)
            ok = (kpos <= pc[None, :, None, None]) & slot_ok[:, r][None, :, None, None]
            scores.append(jnp.where(ok, s, -jnp.inf))
        p = jax.nn.softmax(jnp.concatenate(scores, axis=-1), axis=-1).astype(qc.dtype)
        p = p.reshape(p.shape[:3] + (n_slot, MOBA_BLOCK))
        out = jnp.einsum('bqhk,bqhkd->bqhd', p[..., 0, :], gather_v(bc[..., 0]))
        for r in range(1, n_slot):
            out = out + jnp.einsum('bqhk,bqhkd->bqhd', p[..., r, :], gather_v(bc[..., r]))
        return out
    return sweep_queries(chunk, [q, blk], q_pos, MOBA_QCHUNK)


def paged_block_gather(pool, layer, page_table, tail):
    n_pages = page_table.shape[1]
    n_tail = tail.shape[1]
    ppb = MOBA_BLOCK // PAGE_SIZE
    G = MOBA_HEADS // MOBA_KV_HEADS
    def gather(blk):
        B = blk.shape[0]
        lp = blk[..., None] * ppb + jnp.arange(ppb)
        bi = jnp.arange(B)[:, None, None, None]
        kvi = (jnp.arange(MOBA_HEADS) // G)[None, None, :, None]
        phys = page_table[bi, jnp.minimum(lp, n_pages - 1)]
        rows_pool = pool[layer, phys, :, kvi, :].astype(tail.dtype)
        rows_tail = tail[bi, jnp.clip(lp - n_pages, 0, n_tail - 1), :, kvi, :]
        rows = jnp.where((lp < n_pages)[..., None, None], rows_pool, rows_tail)
        return rows.reshape(blk.shape + (MOBA_BLOCK, tail.shape[-1]))
    return gather


def even_mixer(h, pos, e, W, ctx):
    B, L, _ = h.shape
    u = h @ W['even_w_in'][e]
    ql, cl, krl, hq, hf, hi, hg = split_cols(u, EVEN_SPLITS)
    qh = jnp.einsum('blr,rhd->blhd', rms_norm(ql, W['mla_q_norm'][e]), W['mla_w_uq'][e])
    q_nope = rms_norm(qh[..., :MLA_NOPE], W['mla_qn_norm'][e])
    q_rope = rope(rms_norm(qh[..., MLA_NOPE:], W['mla_qr_norm'][e]), pos)
    rows = jnp.concatenate([rms_norm(cl, W['mla_kv_norm'][e]), rope(rms_norm(krl, W['mla_kr_norm'][e]), pos)], axis=-1)
    if ctx is None:
        kv_rows = rows
        S0 = jnp.zeros((B, HGRN_HEADS, HGRN_DK, HGRN_DV), jnp.float32)
    else:
        past = ctx['cache_mla'][e, ctx['page_table']]
        kv_rows = jnp.concatenate([past.reshape(B, -1, past.shape[-1]).astype(rows.dtype), rows], axis=1)
        S0 = ctx['state_hgrn'][e].astype(jnp.float32)
    a = mla_attention(q_nope, q_rope, pos, kv_rows[..., :MLA_KV_RANK], kv_rows[..., MLA_KV_RANK:],
                      W['mla_w_uk'][e], W['mla_w_uv'][e], W['mla_kn_norm'][e])
    lb = W['hgrn_lb'][e]
    logf = jnp.logaddexp(jnp.log(lb), jnp.log1p(-lb) + jax.nn.log_sigmoid(hf.astype(jnp.float32)))
    kk = -jnp.expm1(logf)
    heads = lambda t, d: t.reshape(B, L, HGRN_HEADS, d).astype(jnp.float32)
    o, S = hgrn2_scan(heads(jax.nn.silu(hq), HGRN_DK), heads(kk, HGRN_DK), heads(logf, HGRN_DK), heads(hi, HGRN_DV), S0)
    o = rms_norm(o.astype(h.dtype), W['hgrn_out_norm'][e]) * jax.nn.silu(hg.reshape(B, L, HGRN_HEADS, HGRN_DV))
    y = jnp.concatenate([a.reshape(B, L, -1), o.reshape(B, L, -1)], axis=-1) @ W['even_w_out'][e]
    return y, rows, S


def odd_mixer(h, pos, j, W, ctx):
    B, L, _ = h.shape
    u = h @ W['odd_w_in'][j]
    mq, mk, mv, xq, xk, xv, ig, fg, og = split_cols(u, ODD_SPLITS)
    q = rms_norm(mq.reshape(B, L, MOBA_HEADS, MOBA_DH), W['moba_q_norm'][j])
    k = rms_norm(mk.reshape(B, L, MOBA_KV_HEADS, MOBA_DH), W['moba_k_norm'][j])
    v = mv.reshape(B, L, MOBA_KV_HEADS, MOBA_DH)
    G = MOBA_HEADS // MOBA_KV_HEADS
    if ctx is None:
        n_cand = (L - 1) // MOBA_BLOCK
        kmeans = k[:, :n_cand * MOBA_BLOCK].astype(jnp.float32).reshape(B, n_cand, MOBA_BLOCK, MOBA_KV_HEADS, MOBA_DH).mean(axis=2)
        nb = -(-L // MOBA_BLOCK)
        def blocked(t):
            t = jnp.pad(t, ((0, 0), (0, nb * MOBA_BLOCK - L), (0, 0), (0, 0)))
            return t.reshape(B, nb, MOBA_BLOCK, MOBA_KV_HEADS, MOBA_DH).transpose(0, 3, 1, 2, 4)
        kb, vb = blocked(k), blocked(v)
        bi = jnp.arange(B)[:, None, None]
        kvi = (jnp.arange(MOBA_HEADS) // G)[None, None, :]
        gather_k = lambda blk: kb[bi, kvi, blk]
        gather_v = lambda blk: vb[bi, kvi, blk]
        conv_prev = jnp.zeros((B, CONV_W - 1, 2 * MLSTM_QKD), h.dtype)
        C0 = jnp.zeros((B, MLSTM_HEADS, MLSTM_DK, MLSTM_DV), jnp.float32)
        n0 = jnp.zeros((B, MLSTM_HEADS, MLSTM_DK), jnp.float32)
        m0 = jnp.zeros((B, MLSTM_HEADS), jnp.float32)
    else:
        pt = ctx['page_table']
        past = pt.shape[1] * PAGE_SIZE
        nt = -(-L // PAGE_SIZE)
        def tailed(t):
            t = jnp.pad(t, ((0, 0), (0, nt * PAGE_SIZE - L), (0, 0), (0, 0)))
            return t.reshape(B, nt, PAGE_SIZE, MOBA_KV_HEADS, MOBA_DH)
        k_tail, v_tail = tailed(k), tailed(v)
        n_cand = (past + L - 1) // MOBA_BLOCK
        k_past = ctx['cache_moba_k'][j, pt].reshape(B, past, MOBA_KV_HEADS, MOBA_DH).astype(k.dtype)
        k_all = jnp.concatenate([k_past, k_tail.reshape(B, nt * PAGE_SIZE, MOBA_KV_HEADS, MOBA_DH)], axis=1)
        kmeans = k_all[:, :n_cand * MOBA_BLOCK].astype(jnp.float32).reshape(B, n_cand, MOBA_BLOCK, MOBA_KV_HEADS, MOBA_DH).mean(axis=2)
        gather_k = paged_block_gather(ctx['cache_moba_k'], j, pt, k_tail)
        gather_v = paged_block_gather(ctx['cache_moba_v'], j, pt, v_tail)
        conv_prev = ctx['state_mlstm_conv'][j]
        C0 = ctx['state_mlstm_c'][j].astype(jnp.float32)
        n0 = ctx['state_mlstm_n'][j].astype(jnp.float32)
        m0 = ctx['state_mlstm_m'][j].astype(jnp.float32)
    a = moba_attention(q, pos, kmeans, gather_k, gather_v)
    qk = jnp.concatenate([xq, xk], axis=-1)
    full = jnp.concatenate([conv_prev.astype(qk.dtype), qk], axis=1)
    w = W['mlstm_conv_w'][j]
    conv = W['mlstm_conv_b'][j] + full[:, 0:L] * w[0]
    for t in range(1, CONV_W):
        conv = conv + full[:, t:t + L] * w[t]
    conv_new = full[:, L:]
    qm, km = jnp.split(jax.nn.silu(conv), 2, axis=-1)
    mh = lambda t, d: t.reshape(B, L, MLSTM_HEADS, d).astype(jnp.float32)
    logi = ig.astype(jnp.float32) + W['mlstm_b_i'][j].astype(jnp.float32)
    logf = jax.nn.log_sigmoid(fg.astype(jnp.float32) + W['mlstm_b_f'][j].astype(jnp.float32))
    hm, C, n, m = mlstm_scan(mh(qm, MLSTM_DK), mh(km, MLSTM_DK) * (MLSTM_DK ** -0.5), mh(xv, MLSTM_DV), logi, logf, C0, n0, m0)
    hm = rms_norm(hm.astype(h.dtype), W['mlstm_out_norm'][j]) * jax.nn.sigmoid(og.reshape(B, L, MLSTM_HEADS, MLSTM_DV))
    y = jnp.concatenate([a.reshape(B, L, -1), hm.reshape(B, L, -1)], axis=-1) @ W['odd_w_out'][j]
    return y, k, v, C, n, m, conv_new


def run_trunk(x, pos, W, ctx):
    new = {name: [] for name in STATE_NAMES}
    for l in range(DEPTH):
        x = x + 0.5 * swiglu(rms_norm(x, W['norm_ffn1'][l]), W['ffn1_w_in'][l], W['ffn1_w_out'][l])
        h = rms_norm(x, W['norm_mix'][l])
        if l % 2 == 0:
            y, rows, S = even_mixer(h, pos, l // 2, W, ctx)
            new['mla'].append(rows)
            new['hgrn'].append(S)
        else:
            y, k, v, C, n, m, cv = odd_mixer(h, pos, l // 2, W, ctx)
            new['moba_k'].append(k)
            new['moba_v'].append(v)
            new['mlstm_c'].append(C)
            new['mlstm_n'].append(n)
            new['mlstm_m'].append(m)
            new['mlstm_conv'].append(cv)
        x = x + y
        x = x + 0.5 * swiglu(rms_norm(x, W['norm_ffn2'][l]), W['ffn2_w_in'][l], W['ffn2_w_out'][l])
    return x, {name: jnp.stack(vals).astype(x.dtype) for name, vals in new.items()}


def setup_inputs(seed: int = 0) -> dict:
    key = jax.random.key(seed)
    ks = iter(jax.random.split(key, 64))
    def nrm(shape, scale):
        return scale * jax.random.normal(next(ks), shape, jnp.float32)
    def gain(shape):
        return 1.0 + 0.02 * jax.random.normal(next(ks), shape, jnp.float32)
    n_pages = PAST_LEN // PAGE_SIZE
    n_pool = (DEC_BATCH * n_pages * POOL_NUM) // POOL_DEN
    page_table = jax.random.permutation(next(ks), n_pool)[:DEC_BATCH * n_pages].reshape(DEC_BATCH, n_pages).astype(jnp.int32)
    d = D_MODEL
    return {
        'x_prompt': nrm((BATCH, SEQ, d), 1.0),
        'x_sample': nrm((DEC_BATCH, DEC_SEQ, d), 1.0),
        'cache_mla': nrm((N_EVEN, n_pool, PAGE_SIZE, MLA_KV_RANK + MLA_ROPE), 1.0),
        'state_hgrn': nrm((N_EVEN, DEC_BATCH, HGRN_HEADS, HGRN_DK, HGRN_DV), 0.5),
        'cache_moba_k': nrm((N_ODD, n_pool, PAGE_SIZE, MOBA_KV_HEADS, MOBA_DH), 1.0),
        'cache_moba_v': nrm((N_ODD, n_pool, PAGE_SIZE, MOBA_KV_HEADS, MOBA_DH), 1.0),
        'state_mlstm_c': nrm((N_ODD, DEC_BATCH, MLSTM_HEADS, MLSTM_DK, MLSTM_DV), 0.5),
        'state_mlstm_n': nrm((N_ODD, DEC_BATCH, MLSTM_HEADS, MLSTM_DK), 0.5),
        'state_mlstm_m': nrm((N_ODD, DEC_BATCH, MLSTM_HEADS), 1.0),
        'state_mlstm_conv': nrm((N_ODD, DEC_BATCH, CONV_W - 1, 2 * MLSTM_QKD), 1.0),
        'page_table': page_table,
        'norm_ffn1': gain((DEPTH, d)),
        'norm_mix': gain((DEPTH, d)),
        'norm_ffn2': gain((DEPTH, d)),
        'ffn1_w_in': nrm((DEPTH, d, 2 * FFN_HIDDEN), d ** -0.5),
        'ffn1_w_out': nrm((DEPTH, FFN_HIDDEN, d), FFN_HIDDEN ** -0.5),
        'ffn2_w_in': nrm((DEPTH, d, 2 * FFN_HIDDEN), d ** -0.5),
        'ffn2_w_out': nrm((DEPTH, FFN_HIDDEN, d), FFN_HIDDEN ** -0.5),
        'even_w_in': nrm((N_EVEN, d, EVEN_IN), d ** -0.5),
        'even_w_out': nrm((N_EVEN, EVEN_MIX, d), EVEN_MIX ** -0.5),
        'mla_q_norm': gain((N_EVEN, MLA_Q_RANK)),
        'mla_w_uq': nrm((N_EVEN, MLA_Q_RANK, MLA_HEADS, MLA_NOPE + MLA_ROPE), MLA_Q_RANK ** -0.5),
        'mla_kv_norm': gain((N_EVEN, MLA_KV_RANK)),
        'mla_w_uk': nrm((N_EVEN, MLA_KV_RANK, MLA_HEADS, MLA_NOPE), MLA_KV_RANK ** -0.5),
        'mla_w_uv': nrm((N_EVEN, MLA_KV_RANK, MLA_HEADS, MLA_V), MLA_KV_RANK ** -0.5),
        'mla_qn_norm': gain((N_EVEN, MLA_NOPE)),
        'mla_qr_norm': gain((N_EVEN, MLA_ROPE)),
        'mla_kn_norm': gain((N_EVEN, MLA_NOPE)),
        'mla_kr_norm': gain((N_EVEN, MLA_ROPE)),
        'hgrn_lb_raw': nrm((N_EVEN, HGRN_FD), 0.5),
        'hgrn_out_norm': gain((N_EVEN, HGRN_DV)),
        'odd_w_in': nrm((N_ODD, d, ODD_IN), d ** -0.5),
        'odd_w_out': nrm((N_ODD, ODD_MIX, d), ODD_MIX ** -0.5),
        'moba_q_norm': gain((N_ODD, MOBA_DH)),
        'moba_k_norm': gain((N_ODD, MOBA_DH)),
        'mlstm_conv_w': nrm((N_ODD, CONV_W, 2 * MLSTM_QKD), CONV_W ** -0.5),
        'mlstm_conv_b': nrm((N_ODD, 2 * MLSTM_QKD), 0.02),
        'mlstm_b_i': nrm((N_ODD, MLSTM_HEADS), 0.1),
        'mlstm_b_f': jnp.linspace(3.0, 6.0, MLSTM_HEADS)[None, :] + nrm((N_ODD, MLSTM_HEADS), 0.1),
        'mlstm_out_norm': gain((N_ODD, MLSTM_DV)),
    }


def reference(x_prompt, x_sample, cache_mla, state_hgrn, cache_moba_k, cache_moba_v, state_mlstm_c,
              state_mlstm_n, state_mlstm_m, state_mlstm_conv, page_table, norm_ffn1, norm_mix, norm_ffn2,
              ffn1_w_in, ffn1_w_out, ffn2_w_in, ffn2_w_out, even_w_in, even_w_out, mla_q_norm, mla_w_uq,
              mla_kv_norm, mla_w_uk, mla_w_uv, mla_qn_norm, mla_qr_norm, mla_kn_norm, mla_kr_norm,
              hgrn_lb_raw, hgrn_out_norm, odd_w_in, odd_w_out, moba_q_norm, moba_k_norm, mlstm_conv_w,
              mlstm_conv_b, mlstm_b_i, mlstm_b_f, mlstm_out_norm):
    p_lb = jax.nn.softmax(hgrn_lb_raw.astype(jnp.float32), axis=0)
    hgrn_lb = jnp.maximum(jnp.cumsum(p_lb, axis=0) - p_lb[0], 0.0)
    W = dict(norm_ffn1=norm_ffn1, norm_mix=norm_mix, norm_ffn2=norm_ffn2, ffn1_w_in=ffn1_w_in,
             ffn1_w_out=ffn1_w_out, ffn2_w_in=ffn2_w_in, ffn2_w_out=ffn2_w_out, even_w_in=even_w_in,
             even_w_out=even_w_out, mla_q_norm=mla_q_norm, mla_w_uq=mla_w_uq, mla_kv_norm=mla_kv_norm,
             mla_w_uk=mla_w_uk, mla_w_uv=mla_w_uv, mla_qn_norm=mla_qn_norm, mla_qr_norm=mla_qr_norm,
             mla_kn_norm=mla_kn_norm, mla_kr_norm=mla_kr_norm, hgrn_lb=hgrn_lb, hgrn_out_norm=hgrn_out_norm,
             odd_w_in=odd_w_in, odd_w_out=odd_w_out, moba_q_norm=moba_q_norm, moba_k_norm=moba_k_norm,
             mlstm_conv_w=mlstm_conv_w, mlstm_conv_b=mlstm_conv_b, mlstm_b_i=mlstm_b_i, mlstm_b_f=mlstm_b_f,
             mlstm_out_norm=mlstm_out_norm)
    ctx = dict(page_table=page_table, cache_mla=cache_mla, state_hgrn=state_hgrn, cache_moba_k=cache_moba_k,
               cache_moba_v=cache_moba_v, state_mlstm_c=state_mlstm_c, state_mlstm_n=state_mlstm_n,
               state_mlstm_m=state_mlstm_m, state_mlstm_conv=state_mlstm_conv)
    y_prompt, sp = run_trunk(x_prompt, jnp.arange(x_prompt.shape[1]), W, None)
    past_len = page_table.shape[1] * PAGE_SIZE
    y_sample, ss = run_trunk(x_sample, past_len + jnp.arange(x_sample.shape[1]), W, ctx)
    return (y_prompt, y_sample, sp['mla'], ss['mla'], sp['hgrn'], ss['hgrn'], sp['moba_k'], ss['moba_k'],
            sp['moba_v'], ss['moba_v'], sp['mlstm_c'], ss['mlstm_c'], sp['mlstm_n'], ss['mlstm_n'],
            sp['mlstm_m'], ss['mlstm_m'], sp['mlstm_conv'], ss['mlstm_conv'])
```

```python
import functools

import jax
import jax.numpy as jnp
from jax import lax
from jax.experimental import pallas as pl
from jax.experimental.pallas import tpu as pltpu

F32, BF16, I32 = jnp.float32, jnp.bfloat16, jnp.int32
HI = lax.Precision.HIGHEST
EPS = 1e-6
NEG = -1e30

D_MODEL = 1024
FFN_HIDDEN = 2048
PAGE = 128
MLA_H, MLA_QR, MLA_R, MLA_DN, MLA_DR, MLA_DV = 8, 256, 128, 64, 32, 64
ROPE_THETA = 10000.0
MLA_SCALE = (MLA_DN + MLA_DR) ** -0.5
HG_H, HG_DK, HG_DV = 4, 128, 128
HG_W = HG_H * HG_DK
MB_H, MB_KVH, MB_DH, MB_BLOCK, MB_TOPK = 8, 4, 64, 256, 3
MB_SCALE = MB_DH ** -0.5
ML_H, ML_DK, ML_DV, CONV_W = 4, 64, 128, 4
ML_QK = ML_H * ML_DK

LANES = 128
VMEM_LIMIT = 56 << 20


def _dot(a, b):
    return jnp.dot(a, b, preferred_element_type=F32)


def _dot_nt(a, b):
    return lax.dot_general(a, b, (((1,), (1,)), ((), ())), preferred_element_type=F32)


def _dot_tn(a, b):
    return lax.dot_general(a, b, (((0,), (0,)), ((), ())), preferred_element_type=F32)


def _rms(x, g):
    return x * lax.rsqrt(jnp.mean(x * x, axis=-1, keepdims=True) + EPS) * g


def _sigmoid(x):
    return 1.0 / (1.0 + jnp.exp(-x))


def _silu(x):
    return x * _sigmoid(x)


def _rows(tm, n):
    return pl.BlockSpec((tm, n), lambda i: (i, 0))


def _layer(shape, l):
    zeros = (0,) * len(shape)
    return pl.BlockSpec((None,) + tuple(shape), lambda *_: (l,) + zeros)


def _const(shape):
    zeros = (0,) * len(shape)
    return pl.BlockSpec(tuple(shape), lambda *_: zeros)


def _params(*sem):
    return pltpu.CompilerParams(dimension_semantics=sem, vmem_limit_bytes=VMEM_LIMIT)


def _row_tile(m):
    return min(256, m)


def _ffn_body(x_ref, g_ref, win_ref, wout_ref, o_ref, *, fc):
    x = x_ref[...]
    hn = _rms(x, g_ref[...]).astype(BF16)
    acc = jnp.zeros_like(x)
    for c in range(FFN_HIDDEN // fc):
        a = _dot(hn, win_ref[:, c * fc:(c + 1) * fc])
        b = _dot(hn, win_ref[:, FFN_HIDDEN + c * fc:FFN_HIDDEN + (c + 1) * fc])
        acc = acc + _dot((_silu(a) * b).astype(BF16), wout_ref[c * fc:(c + 1) * fc, :])
    o_ref[...] = x + 0.5 * acc


def _ffn(x, g, w_in, w_out, l):
    m = x.shape[0]
    tm = _row_tile(m)
    return pl.pallas_call(
        functools.partial(_ffn_body, fc=512),
        grid=(m // tm,),
        in_specs=[_rows(tm, D_MODEL), _layer((1, D_MODEL), l),
                  _layer((D_MODEL, 2 * FFN_HIDDEN), l), _layer((FFN_HIDDEN, D_MODEL), l)],
        out_specs=_rows(tm, D_MODEL),
        out_shape=jax.ShapeDtypeStruct((m, D_MODEL), F32),
        compiler_params=_params("parallel"),
        name="ffn",
    )(x, g, w_in, w_out)


EV_COLS = 256 + 128 + 128 + 4 * HG_W


def _even_in_body(x_ref, g_ref, w_ref, qg_ref, wuq_ref, gq_ref, kvg_ref, wuk_ref, gk_ref, gkr_ref, bseg_ref,
                  cos_ref, sa_ref, sb_ref, llb_ref, l1m_ref, omlb_ref,
                  qcat_ref, kcat_ref, rows_ref, vlat_ref, hq_ref, hk_ref, hlf_ref, hi_ref, hg_ref):
    hn = _rms(x_ref[...], g_ref[...]).astype(BF16)
    cos_t, sin_a, sin_b = cos_ref[...], sa_ref[...], sb_ref[...]
    bseg = bseg_ref[...]

    def rope(t):
        return t * cos_t + pltpu.roll(t, 16, 1) * sin_a + pltpu.roll(t, LANES - 16, 1) * sin_b

    def segnorm(t, gain):
        ms = _dot((t * t).astype(BF16), bseg)
        return t * lax.rsqrt(ms + EPS) * gain

    ql = _dot(hn, w_ref[:, 0:256])
    qh = _dot(_rms(ql, qg_ref[...]).astype(BF16), wuq_ref[...])
    gq = gq_ref[...]
    for h in range(MLA_H):
        t = segnorm(qh[:, h * LANES:(h + 1) * LANES], gq)
        qcat_ref[:, h * LANES:(h + 1) * LANES] = (rope(t) * MLA_SCALE).astype(BF16)

    cn = _rms(_dot(hn, w_ref[:, 256:384]), kvg_ref[...])
    krl = _dot(hn, w_ref[:, 384:512])
    ms = jnp.sum(krl * krl, axis=-1, keepdims=True) * (1.0 / MLA_DR)
    krr = rope(krl * lax.rsqrt(ms + EPS) * gkr_ref[...])
    rows_ref[:, 0:MLA_R] = cn
    rows_ref[:, MLA_R:MLA_R + MLA_DR] = pltpu.roll(krr, 64, 1)[:, 0:MLA_DR]
    cnb = cn.astype(BF16)
    vlat_ref[...] = cnb
    kn = _dot(cnb, wuk_ref[...])
    gk = gk_ref[...]
    for h in range(MLA_H):
        t = segnorm(kn[:, h * LANES:(h + 1) * LANES], gk)
        kcat_ref[:, h * LANES:(h + 1) * LANES] = (t + krr).astype(BF16)

    o = 512
    hq_ref[...] = _silu(_dot(hn, w_ref[:, o:o + HG_W])).astype(hq_ref.dtype)
    hf = _dot(hn, w_ref[:, o + HG_W:o + 2 * HG_W])
    logsig = jnp.minimum(hf, 0.0) - jnp.log(1.0 + jnp.exp(-jnp.abs(hf)))
    aa, bb = llb_ref[...], l1m_ref[...] + logsig
    hlf_ref[...] = jnp.maximum(aa, bb) + jnp.log(1.0 + jnp.exp(-jnp.abs(aa - bb)))
    hk_ref[...] = omlb_ref[...] * _sigmoid(-hf)
    hi_ref[...] = _dot(hn, w_ref[:, o + 2 * HG_W:o + 3 * HG_W]).astype(hi_ref.dtype)
    hg_ref[...] = _silu(_dot(hn, w_ref[:, o + 3 * HG_W:o + 4 * HG_W])).astype(hg_ref.dtype)


def _even_in(x, P, e, l, tabs):
    m = x.shape[0]
    tm = _row_tile(m)
    cos_t, sin_a, sin_b = tabs
    nt = cos_t.shape[0] // tm
    tab = pl.BlockSpec((tm, LANES), lambda i: (i % nt, 0))
    outs = [(8 * LANES, BF16), (8 * LANES, BF16), (MLA_R + MLA_DR, F32), (MLA_R, BF16),
            (HG_W, BF16), (HG_W, F32), (HG_W, F32), (HG_W, BF16), (HG_W, BF16)]
    return pl.pallas_call(
        _even_in_body,
        grid=(m // tm,),
        in_specs=[_rows(tm, D_MODEL), _layer((1, D_MODEL), l), _layer((D_MODEL, EV_COLS), e),
                  _layer((1, MLA_QR), e), _layer((MLA_QR, 8 * LANES), e), _layer((1, LANES), e),
                  _layer((1, MLA_R), e), _layer((MLA_R, 8 * LANES), e), _layer((1, LANES), e),
                  _layer((1, LANES), e), _const((LANES, LANES)), tab, tab, tab,
                  _layer((1, HG_W), e), _layer((1, HG_W), e), _layer((1, HG_W), e)],
        out_specs=[_rows(tm, n) for n, _ in outs],
        out_shape=[jax.ShapeDtypeStruct((m, n), dt) for n, dt in outs],
        compiler_params=_params("parallel"),
        name="even_in",
    )(x, P["norm_mix"], P["even_w"], P["mla_q_norm"], P["mla_wuq"], P["mla_gq"], P["mla_kv_norm"],
      P["mla_wuk"], P["mla_gk"], P["mla_gkr"], P["bseg"], cos_t, sin_a, sin_b,
      P["hg_loglb"], P["hg_log1mlb"], P["hg_1mlb"])


def _mla_p_body(q_ref, k_ref, v_ref, wuv_ref, a_ref, *, tq):
    i = pl.program_id(1)
    row = lax.broadcasted_iota(I32, (tq, tq), 0)
    col = lax.broadcasted_iota(I32, (tq, tq), 1)
    causal = col <= row

    def head(h):
        q = q_ref[:, h * LANES:(h + 1) * LANES]

        def step(j, carry, diag):
            m, l, acc = carry
            start = pl.multiple_of(j * tq, tq)
            k = k_ref[pl.ds(start, tq), h * LANES:(h + 1) * LANES]
            v = v_ref[pl.ds(start, tq), :]
            s = _dot_nt(q, k)
            if diag:
                s = jnp.where(causal, s, NEG)
            m_new = jnp.maximum(m, jnp.max(s, axis=-1, keepdims=True))
            alpha = jnp.exp(m - m_new)
            p = jnp.exp(s - m_new)
            l = alpha * l + jnp.sum(p, axis=-1, keepdims=True)
            acc = alpha * acc + _dot(p.astype(BF16), v)
            return m_new, l, acc

        init = (jnp.full((tq, 1), NEG, F32), jnp.zeros((tq, 1), F32), jnp.zeros((tq, MLA_R), F32))
        carry = lax.fori_loop(0, i, functools.partial(step, diag=False), init)
        _, l, acc = step(i, carry, True)
        return _dot((acc / l).astype(BF16), wuv_ref[h])

    for hp in range(MLA_H // 2):
        a_ref[:, hp * LANES:(hp + 1) * LANES] = (head(2 * hp) + head(2 * hp + 1)).astype(a_ref.dtype)


def _mla_prompt(qcat, kcat, vlat, wuv, e, b, l):
    tq = min(256, l)
    nq = l // tq
    return pl.pallas_call(
        functools.partial(_mla_p_body, tq=tq),
        grid=(b, nq),
        in_specs=[pl.BlockSpec((tq, 8 * LANES), lambda bi, i: (bi * nq + i, 0)),
                  pl.BlockSpec((l, 8 * LANES), lambda bi, i: (bi, 0)),
                  pl.BlockSpec((l, MLA_R), lambda bi, i: (bi, 0)),
                  _layer((MLA_H, MLA_R, LANES), e)],
        out_specs=pl.BlockSpec((tq, MLA_H * MLA_DV), lambda bi, i: (bi * nq + i, 0)),
        out_shape=jax.ShapeDtypeStruct((b * l, MLA_H * MLA_DV), BF16),
        compiler_params=_params("parallel", "arbitrary"),
        name="mla_prompt",
    )(qcat, kcat, vlat, wuv)


MLA_PG = 8


def _mla_s_body(pt_ref, qbd_ref, qr_ref, qself_ref, kself_ref, vself_ref, wukt_ref, wuv_ref, *rest, nsteps):
    pages, a_ref = rest[:MLA_PG], rest[MLA_PG]
    qa_scr, m_scr, l_scr, acc_scr = rest[MLA_PG + 1:]
    j = pl.program_id(1)
    wukt = wukt_ref[...]

    @pl.when(j == 0)
    def _():
        qa_scr[...] = _dot(qbd_ref[...].astype(BF16), wukt)
        m_scr[...] = jnp.full(m_scr.shape, NEG, F32)
        l_scr[...] = jnp.zeros(l_scr.shape, F32)
        acc_scr[...] = jnp.zeros(acc_scr.shape, F32)

    tile = jnp.concatenate([p[...] for p in pages], axis=0)
    keys = tile.shape[0]
    cb = tile[:, 0:MLA_R].astype(BF16)
    krb = tile[:, MLA_R:MLA_R + MLA_DR].astype(BF16)
    knt = _dot_nt(wukt, cb)
    ss = jnp.sum((knt * knt).reshape(MLA_H, MLA_DN, keys), axis=1)
    rinv = lax.rsqrt(ss * (1.0 / MLA_DN) + EPS)
    s = rinv * _dot_nt(qa_scr[...].astype(BF16), cb) + _dot_nt(qr_ref[...].astype(BF16), krb)
    m_old = m_scr[...]
    m_new = jnp.maximum(m_old, jnp.max(s, axis=-1, keepdims=True))
    alpha = jnp.exp(m_old - m_new)
    p = jnp.exp(s - m_new)
    l_scr[...] = alpha * l_scr[...] + jnp.sum(p, axis=-1, keepdims=True)
    acc_scr[...] = alpha * acc_scr[...] + _dot(p.astype(BF16), cb)
    m_scr[...] = m_new

    @pl.when(j == nsteps - 1)
    def _():
        s_self = jnp.sum(qself_ref[...].astype(F32) * kself_ref[...].astype(F32), axis=-1, keepdims=True)
        m_old = m_scr[...]
        m_new = jnp.maximum(m_old, s_self)
        alpha = jnp.exp(m_old - m_new)
        p_self = jnp.exp(s_self - m_new)
        l = alpha * l_scr[...] + p_self
        acc = alpha * acc_scr[...] + p_self * vself_ref[...].astype(F32)
        r = _dot((acc / l).astype(BF16), wuv_ref[...])
        rh = lax.broadcasted_iota(I32, r.shape, 0)
        ch = lax.broadcasted_iota(I32, r.shape, 1) // MLA_DV
        a_ref[...] = jnp.sum(jnp.where(rh == ch, r, 0.0), axis=0, keepdims=True).astype(a_ref.dtype)


def _mla_sample(page_table, cache, e, qbd, qr, qself, kself, vself, wukt, wuv):
    bs = qbd.shape[0]
    n_pages = page_table.shape[1]
    assert n_pages % MLA_PG == 0
    nsteps = n_pages // MLA_PG
    width = MLA_R + MLA_DR
    seq3 = lambda n2, n3: pl.BlockSpec((None, n2, n3), lambda b, j, pt: (b, 0, 0))

    def page_spec(p):
        return pl.BlockSpec((None, None, PAGE, width),
                            lambda b, j, pt: (e, pt[b * n_pages + j * MLA_PG + p], 0, 0))

    grid_spec = pltpu.PrefetchScalarGridSpec(
        num_scalar_prefetch=1,
        grid=(bs, nsteps),
        in_specs=[seq3(MLA_H, MLA_H * MLA_DN), seq3(MLA_H, MLA_DR), seq3(MLA_H, LANES), seq3(MLA_H, LANES),
                  seq3(1, MLA_R),
                  pl.BlockSpec((None, MLA_H * MLA_DN, MLA_R), lambda b, j, pt: (e, 0, 0)),
                  pl.BlockSpec((None, MLA_R, MLA_H * MLA_DV), lambda b, j, pt: (e, 0, 0))]
                 + [page_spec(p) for p in range(MLA_PG)],
        out_specs=pl.BlockSpec((None, 1, MLA_H * MLA_DV), lambda b, j, pt: (b, 0, 0)),
        scratch_shapes=[pltpu.VMEM((MLA_H, MLA_R), F32), pltpu.VMEM((MLA_H, 1), F32),
                        pltpu.VMEM((MLA_H, 1), F32), pltpu.VMEM((MLA_H, MLA_R), F32)],
    )
    out = pl.pallas_call(
        functools.partial(_mla_s_body, nsteps=nsteps),
        grid_spec=grid_spec,
        out_shape=jax.ShapeDtypeStruct((bs, 1, MLA_H * MLA_DV), BF16),
        compiler_params=_params("parallel", "arbitrary"),
        name="mla_sample",
    )(page_table.reshape(-1), qbd, qr, qself, kself, vself, wukt, wuv, *([cache] * MLA_PG))
    return out.reshape(bs, MLA_H * MLA_DV)


HG_BS = 32


def _hgrn_body(q_ref, k_ref, lf_ref, v_ref, g_ref, s0_ref, gain_ref, wind_ref, tri_ref,
               o_ref, sout_ref, st_scr, z_scr, *, c_len, bs, nc):
    c = pl.program_id(1)
    nb = c_len // bs

    @pl.when(c == 0)
    def _():
        for h in range(HG_H):
            st_scr[h] = s0_ref[h].T

    b = jnp.dot(tri_ref[...], lf_ref[...], precision=HI, preferred_element_type=F32)
    q = q_ref[...].astype(F32)
    k = k_ref[...]
    v = v_ref[...]
    b_last = b[c_len - 1:c_len, :]

    b3, q3, k3 = (t.reshape(nb, bs, HG_W) for t in (b, q, k))
    for s in range(bs):
        z = q3 * k3[:, s:s + 1, :] * jnp.exp(jnp.minimum(b3 - b3[:, s:s + 1, :], 0.0))
        z = z.reshape(c_len, HG_W).astype(BF16)
        for h in range(HG_H):
            z_scr[h * c_len:(h + 1) * c_len, s * LANES:(s + 1) * LANES] = z[:, h * LANES:(h + 1) * LANES]
    a_diag = _dot(z_scr[...], wind_ref[...])
    t_loc = lax.broadcasted_iota(I32, a_diag.shape, 0) % bs
    s_loc = lax.broadcasted_iota(I32, a_diag.shape, 1)
    a_diag = jnp.where(t_loc >= s_loc, a_diag, 0.0).astype(BF16)

    qe = (q * jnp.exp(b)).astype(BF16)
    kd = (k * jnp.exp(b_last - b)).astype(BF16)
    gain = gain_ref[...]
    g = g_ref[...].astype(F32)
    for h in range(HG_H):
        hs = slice(h * LANES, (h + 1) * LANES)
        st = st_scr[h]
        o_h = _dot_nt(qe[:, hs], st.astype(BF16))
        blocks = []
        for ib in range(nb):
            rs = slice(ib * bs, (ib + 1) * bs)
            oi = _dot(a_diag[h * c_len + ib * bs:h * c_len + (ib + 1) * bs, :], v[rs, hs])
            if ib > 0:
                ref = b[ib * bs - 1:ib * bs, hs]
                qa = (q[rs, hs] * jnp.exp(b[rs, hs] - ref)).astype(BF16)
                ka = (k[0:ib * bs, hs] * jnp.exp(ref - b[0:ib * bs, hs])).astype(BF16)
                oi = oi + _dot(_dot_nt(qa, ka).astype(BF16), v[0:ib * bs, hs])
            blocks.append(oi)
        o_h = o_h + (blocks[0] if nb == 1 else jnp.concatenate(blocks, axis=0))
        st_scr[h] = jnp.exp(b_last[:, hs]) * st + _dot_tn(v[:, hs], kd[:, hs])
        o_ref[:, hs] = (_rms(o_h, gain[:, hs]) * g[:, hs]).astype(o_ref.dtype)

    @pl.when(c == nc - 1)
    def _():
        for h in range(HG_H):
            sout_ref[h] = st_scr[h].T


def _hgrn_scan(hq, hk, hlf, hi, hg, s0, gain, b, l, c_len):
    bs = min(HG_BS, c_len)
    nc = l // c_len
    chunk = lambda: pl.BlockSpec((c_len, HG_W), lambda bi, ci: (bi * nc + ci, 0))
    state = lambda: pl.BlockSpec((None, HG_H, HG_DK, HG_DV), lambda bi, ci: (bi, 0, 0, 0))
    wind = jnp.repeat(jnp.eye(bs, dtype=BF16), LANES, axis=0)
    tri = jnp.tril(jnp.ones((c_len, c_len), F32))
    return pl.pallas_call(
        functools.partial(_hgrn_body, c_len=c_len, bs=bs, nc=nc),
        grid=(b, nc),
        in_specs=[chunk(), chunk(), chunk(), chunk(), chunk(), state(), _const((1, HG_W)),
                  _const((bs * LANES, bs)), _const((c_len, c_len))],
        out_specs=[chunk(), state()],
        out_shape=[jax.ShapeDtypeStruct((b * l, HG_W), BF16),
                   jax.ShapeDtypeStruct((b, HG_H, HG_DK, HG_DV), F32)],
        scratch_shapes=[pltpu.VMEM((HG_H, HG_DV, HG_DK), F32), pltpu.VMEM((HG_H * c_len, bs * LANES), BF16)],
        compiler_params=_params("parallel", "arbitrary"),
        name="hgrn_scan",
    )(hq, hk, hlf, hi, hg, s0, gain, wind, tri)


def _out_body(x_ref, a_ref, o_ref, w_ref, y_ref):
    half = a_ref.shape[1]
    y_ref[...] = x_ref[...] + _dot(a_ref[...], w_ref[0:half, :]) + _dot(o_ref[...], w_ref[half:2 * half, :])


def _out_proj(x, a, o, w, e):
    m = x.shape[0]
    tm = _row_tile(m)
    half = a.shape[1]
    return pl.pallas_call(
        _out_body,
        grid=(m // tm,),
        in_specs=[_rows(tm, D_MODEL), _rows(tm, half), _rows(tm, half), _layer((2 * half, D_MODEL), e)],
        out_specs=_rows(tm, D_MODEL),
        out_shape=jax.ShapeDtypeStruct((m, D_MODEL), F32),
        compiler_params=_params("parallel"),
        name="out_proj",
    )(x, a, o, w)


OD_Q, OD_K, OD_V = 0, 8 * LANES, 12 * LANES
OD_XQK, OD_XV, OD_OG, OD_G = 16 * LANES, 20 * LANES, 24 * LANES, 28 * LANES
OD_COLS = 29 * LANES


def _odd_in_body(x_ref, g_ref, w_ref, gq_ref, gk_ref, gb_ref,
                 qpad_ref, kpad_ref, vpad_ref, xqk_ref, xv_ref, og_ref, gates_ref):
    hn = _rms(x_ref[...], g_ref[...]).astype(BF16)

    def headnorm(t, gain):
        ms = jnp.sum(t * t, axis=-1, keepdims=True) * (1.0 / MB_DH)
        return t * lax.rsqrt(ms + EPS) * gain

    gq, gk = gq_ref[...], gk_ref[...]
    for h in range(MB_H):
        t = _dot(hn, w_ref[:, OD_Q + h * LANES:OD_Q + (h + 1) * LANES])
        qpad_ref[:, h * LANES:(h + 1) * LANES] = (headnorm(t, gq) * MB_SCALE).astype(BF16)
    for h in range(MB_KVH):
        t = _dot(hn, w_ref[:, OD_K + h * LANES:OD_K + (h + 1) * LANES])
        kpad_ref[:, h * LANES:(h + 1) * LANES] = headnorm(t, gk)
    vpad_ref[...] = _dot(hn, w_ref[:, OD_V:OD_V + 4 * LANES])
    xqk_ref[...] = _dot(hn, w_ref[:, OD_XQK:OD_XQK + 4 * LANES])
    xv_ref[...] = _dot(hn, w_ref[:, OD_XV:OD_XV + 4 * LANES]).astype(xv_ref.dtype)
    og_ref[...] = _sigmoid(_dot(hn, w_ref[:, OD_OG:OD_OG + 4 * LANES])).astype(og_ref.dtype)
    gt = _dot(hn, w_ref[:, OD_G:OD_G + LANES]) + gb_ref[...]
    lane = lax.broadcasted_iota(I32, gt.shape, 1)
    logsig = jnp.minimum(gt, 0.0) - jnp.log(1.0 + jnp.exp(-jnp.abs(gt)))
    gates_ref[...] = jnp.where(lane < ML_H, gt, logsig)


def _odd_in(x, P, j, l):
    m = x.shape[0]
    tm = _row_tile(m)
    outs = [(8 * LANES, BF16), (4 * LANES, F32), (4 * LANES, F32), (2 * ML_QK, F32),
            (ML_H * ML_DV, BF16), (ML_H * ML_DV, BF16), (LANES, F32)]
    return pl.pallas_call(
        _odd_in_body,
        grid=(m // tm,),
        in_specs=[_rows(tm, D_MODEL), _layer((1, D_MODEL), l), _layer((D_MODEL, OD_COLS), j),
                  _layer((1, LANES), j), _layer((1, LANES), j), _layer((1, LANES), j)],
        out_specs=[_rows(tm, n) for n, _ in outs],
        out_shape=[jax.ShapeDtypeStruct((m, n), dt) for n, dt in outs],
        compiler_params=_params("parallel"),
        name="odd_in",
    )(x, P["norm_mix"], P["odd_w"], P["moba_gq"], P["moba_gk"], P["mlstm_gb"])


def _moba_p_body(q_ref, k_ref, v_ref, a_ref, *, nb):
    i = pl.program_id(1)
    blk = MB_BLOCK
    row = lax.broadcasted_iota(I32, (blk, blk), 0)
    col = lax.broadcasted_iota(I32, (blk, blk), 1)
    causal = col <= row
    cand_lane = lax.broadcasted_iota(I32, (blk, nb), 1)
    kmean = jnp.mean(k_ref[...].reshape(nb, blk, MB_KVH * LANES), axis=1)
    own = pl.multiple_of(i * blk, blk)

    def select(q, km):
        gate = lax.dot_general(q.astype(F32), km, (((1,), (1,)), ((), ())), precision=HI,
                               preferred_element_type=F32)
        cand = jnp.where(lax.broadcasted_iota(I32, (1, nb), 1) < i, 1.0, 0.0)
        rank = jnp.zeros((blk, nb), F32)
        for c in range(nb - 1):
            gc = gate[:, c:c + 1]
            better = (gc > gate) | ((gc == gate) & (c < cand_lane))
            rank = rank + jnp.where(better, cand[:, c:c + 1], 0.0)
        return jnp.where(rank < MB_TOPK, cand, 0.0)

    def update(carry, s, vb):
        m, l, acc = carry
        m_new = jnp.maximum(m, jnp.max(s, axis=-1, keepdims=True))
        alpha = jnp.exp(m - m_new)
        p = jnp.exp(s - m_new)
        return m_new, alpha * l + jnp.sum(p, axis=-1, keepdims=True), alpha * acc + _dot(p.astype(BF16), vb)

    for g in range(MB_KVH):
        gs = slice(g * LANES, (g + 1) * LANES)
        qs = [q_ref[:, (2 * g + hh) * LANES:(2 * g + hh + 1) * LANES] for hh in range(2)]
        sels = [select(q, kmean[:, gs]) for q in qs]
        kb = k_ref[pl.ds(own, blk), gs].astype(BF16)
        vb = v_ref[pl.ds(own, blk), gs].astype(BF16)
        init = (jnp.full((blk, 1), NEG, F32), jnp.zeros((blk, 1), F32), jnp.zeros((blk, LANES), F32))
        carries = [update(init, jnp.where(causal, _dot_nt(q, kb), NEG), vb) for q in qs]

        def step(j, carries):
            start = pl.multiple_of(j * blk, blk)
            kj = k_ref[pl.ds(start, blk), gs].astype(BF16)
            vj = v_ref[pl.ds(start, blk), gs].astype(BF16)
            out = []
            for q, sel, carry in zip(qs, sels, carries):
                sel_j = jnp.max(jnp.where(cand_lane == j, sel, 0.0), axis=-1, keepdims=True)
                out.append(update(carry, jnp.where(sel_j > 0.0, _dot_nt(q, kj), NEG), vj))
            return tuple(out)

        carries = lax.fori_loop(0, i, step, tuple(carries))
        for hh, (_, l, acc) in enumerate(carries):
            h = 2 * g + hh
            a_ref[:, h * MB_DH:(h + 1) * MB_DH] = (acc / l)[:, 0:MB_DH].astype(a_ref.dtype)


def _moba_prompt(qpad, kpad, vpad, b, l):
    assert l % MB_BLOCK == 0
    nb = l // MB_BLOCK
    return pl.pallas_call(
        functools.partial(_moba_p_body, nb=nb),
        grid=(b, nb),
        in_specs=[pl.BlockSpec((MB_BLOCK, 8 * LANES), lambda bi, i: (bi * nb + i, 0)),
                  pl.BlockSpec((l, 4 * LANES), lambda bi, i: (bi, 0)),
                  pl.BlockSpec((l, 4 * LANES), lambda bi, i: (bi, 0))],
        out_specs=pl.BlockSpec((MB_BLOCK, MB_H * MB_DH), lambda bi, i: (bi * nb + i, 0)),
        out_shape=jax.ShapeDtypeStruct((b * l, MB_H * MB_DH), BF16),
        compiler_params=_params("parallel", "arbitrary"),
        name="moba_prompt",
    )(qpad, kpad, vpad)


MB_SEL_PG = 16


def _moba_sel_body(pt_ref, qsel_ref, *rest, nsteps, n_cand):
    pages, sel_ref, km_scr = rest[:MB_SEL_PG], rest[MB_SEL_PG], rest[MB_SEL_PG + 1]
    j = pl.program_id(1)
    ppb = MB_BLOCK // PAGE
    nblk = MB_SEL_PG // ppb
    means = []
    for ib in range(nblk):
        tot = sum(jnp.sum(pages[ib * ppb + p][...], axis=0, keepdims=True) for p in range(ppb))
        means.append(tot * (1.0 / MB_BLOCK))
    km_scr[pl.ds(pl.multiple_of(j * nblk, nblk), nblk), :] = jnp.concatenate(means, axis=0)

    @pl.when(j == nsteps - 1)
    def _():
        gate = lax.dot_general(qsel_ref[...], km_scr[...], (((1,), (1,)), ((), ())), precision=HI,
                               preferred_element_type=F32)
        lane = lax.broadcasted_iota(I32, gate.shape, 1)
        rank = jnp.zeros(gate.shape, I32)
        for c in range(n_cand):
            gc = gate[:, c:c + 1]
            better = (gc > gate) | ((gc == gate) & (c < lane))
            rank = rank + jnp.where(better, 1, 0)
        out_lane = lax.broadcasted_iota(I32, sel_ref.shape, 1)
        out = jnp.zeros(sel_ref.shape, I32)
        for r in range(MB_TOPK):
            idx = jnp.sum(jnp.where(rank == r, lane, 0), axis=-1, keepdims=True)
            out = jnp.where(out_lane == r, idx, out)
        sel_ref[...] = out


def _moba_select(page_table, cache_k, jl, qsel):
    bs = qsel.shape[0]
    n_pages = page_table.shape[1]
    assert n_pages % MB_SEL_PG == 0 and (MB_SEL_PG * PAGE // MB_BLOCK) % 8 == 0
    nsteps = n_pages // MB_SEL_PG
    n_cand = n_pages * PAGE // MB_BLOCK
    assert n_cand >= MB_TOPK
    width = MB_KVH * MB_DH

    def page_spec(p):
        return pl.BlockSpec((None, None, PAGE, width),
                            lambda b, j, pt: (jl, pt[b * n_pages + j * MB_SEL_PG + p], 0, 0))

    grid_spec = pltpu.PrefetchScalarGridSpec(
        num_scalar_prefetch=1,
        grid=(bs, nsteps),
        in_specs=[pl.BlockSpec((None, MB_H, width), lambda b, j, pt: (b, 0, 0))]
                 + [page_spec(p) for p in range(MB_SEL_PG)],
        out_specs=pl.BlockSpec((None, MB_H, LANES), lambda b, j, pt: (b, 0, 0)),
        scratch_shapes=[pltpu.VMEM((n_cand, width), F32)],
    )
    sel = pl.pallas_call(
        functools.partial(_moba_sel_body, nsteps=nsteps, n_cand=n_cand),
        grid_spec=grid_spec,
        out_shape=jax.ShapeDtypeStruct((bs, MB_H, LANES), I32),
        compiler_params=_params("parallel", "arbitrary"),
        name="moba_select",
    )(page_table.reshape(-1), qsel, *([cache_k] * MB_SEL_PG))
    return sel[:, :, :MB_TOPK]


def _moba_s_body(pt_ref, sel_ref, q_ref, kself_ref, vself_ref, *rest):
    n = MB_TOPK * (MB_BLOCK // PAGE)
    kpages, vpages, o_ref = rest[:n], rest[n:2 * n], rest[2 * n]
    q1 = q_ref[...]
    q = jnp.broadcast_to(q1, (8, LANES)).astype(BF16)
    scores = [_dot_nt(q, kp[...].astype(BF16)) for kp in kpages]
    s_self = jnp.sum(q1 * kself_ref[...], axis=-1, keepdims=True)
    m = s_self
    for s in scores:
        m = jnp.maximum(m, jnp.max(s, axis=-1, keepdims=True))
    p_self = jnp.exp(s_self - m)
    l = p_self
    acc = p_self * vself_ref[...]
    for s, vp in zip(scores, vpages):
        p = jnp.exp(s - m)
        l = l + jnp.sum(p, axis=-1, keepdims=True)
        acc = acc + _dot(p.astype(BF16), vp[...].astype(BF16))
    o_ref[...] = (acc / l)[0:1, :]


def _moba_sample(page_table, sel, cache_k, cache_v, jl, qm, kself, vself):
    bs = qm.shape[0]
    n_pages = page_table.shape[1]
    ppb = MB_BLOCK // PAGE
    one = lambda: pl.BlockSpec((None, None, 1, LANES), lambda b, h, pt, sl: (b, h, 0, 0))

    def page_spec(r, p):
        return pl.BlockSpec((None, None, PAGE, LANES),
                            lambda b, h, pt, sl: (jl, pt[b * n_pages + sl[(b * MB_H + h) * MB_TOPK + r] * ppb + p],
                                                  0, h // 4))

    specs = [page_spec(r, p) for r in range(MB_TOPK) for p in range(ppb)]
    grid_spec = pltpu.PrefetchScalarGridSpec(
        num_scalar_prefetch=2,
        grid=(bs, MB_H),
        in_specs=[one(), one(), one()] + specs + specs,
        out_specs=one(),
    )
    n = len(specs)
    return pl.pallas_call(
        _moba_s_body,
        grid_spec=grid_spec,
        out_shape=jax.ShapeDtypeStruct((bs, MB_H, 1, LANES), F32),
        compiler_params=_params("parallel", "arbitrary"),
        name="moba_sample",
    )(page_table.reshape(-1), sel.reshape(-1), qm, kself, vself, *([cache_k] * n), *([cache_v] * n))


def _mlstm_body(xqk_ref, xv_ref, og_ref, gcol_ref, grow_ref, cw_ref, cb_ref, prev_ref, c0_ref, m0_ref,
                gain_ref, tri_ref, triu_ref,
                h_ref, cout_ref, mout_ref, convout_ref, full_scr, c_scr, m_scr, *, c_len, nc, l_last):
    c = pl.program_id(1)
    pad = 8
    w = 2 * ML_QK

    @pl.when(c == 0)
    def _():
        full_scr[pad - (CONV_W - 1):pad, :] = prev_ref[...]
        c_scr[...] = c0_ref[...]
        m_scr[...] = m0_ref[...]

    full_scr[pad:pad + c_len, :] = xqk_ref[...]
    conv = cb_ref[...]
    for t in range(CONV_W):
        o = pad - (CONV_W - 1) + t
        conv = conv + full_scr[o:o + c_len, :] * cw_ref[t:t + 1, :]

    @pl.when(c == nc - 1)
    def _():
        convout_ref[...] = full_scr[pad + l_last - (CONV_W - 1):pad + l_last, :]

    full_scr[pad - (CONV_W - 1):pad, :] = full_scr[pad + c_len - (CONV_W - 1):pad + c_len, :]

    qk = _silu(conv)
    qf = qk[:, 0:ML_QK]
    kf = qk[:, ML_QK:w] * (ML_DK ** -0.5)
    kb = kf.astype(BF16)
    gcol, grow = gcol_ref[...], grow_ref[...]
    f_col = jnp.dot(tri_ref[...], gcol, precision=HI, preferred_element_type=F32)
    f_row = jnp.dot(grow, triu_ref[...], precision=HI, preferred_element_type=F32)
    row = lax.broadcasted_iota(I32, (c_len, c_len), 0)
    col = lax.broadcasted_iota(I32, (c_len, c_len), 1)
    causal = col <= row
    head_of_lane = lax.broadcasted_iota(I32, (1, ML_QK), 1) // ML_DK
    head_of_row = lax.broadcasted_iota(I32, (ML_QK, 1), 0) // ML_DK
    m_lane = lax.broadcasted_iota(I32, (1, ML_H), 1)
    one_col = jnp.where(lax.broadcasted_iota(I32, (c_len, LANES), 1) == 0, 1.0, 0.0).astype(BF16)
    state = c_scr[...]
    state_b = state.astype(BF16)
    m_all = m_scr[...]
    gain = gain_ref[...]
    xv = xv_ref[...]
    og = og_ref[...].astype(F32)
    decay_col = jnp.zeros((ML_QK, 1), F32)
    upd = jnp.zeros((ML_QK, 2 * LANES), F32)
    m_next = jnp.zeros((1, ML_H), F32)
    for h in range(ML_H):
        fc = f_col[:, ML_H + h:ML_H + h + 1]
        ic = gcol[:, h:h + 1]
        fr = f_row[ML_H + h:ML_H + h + 1, :]
        ir = grow[h:h + 1, :]
        m_prev = m_all[:, h:h + 1]
        dmat = jnp.where(causal, fc - fr + ir, NEG)
        inter = fc + m_prev
        mt = jnp.maximum(jnp.max(dmat, axis=-1, keepdims=True), inter)
        qh = jnp.where(head_of_lane == h, qf, 0.0).astype(BF16)
        wd = jnp.exp(dmat - mt) * _dot_nt(qh, kb)
        a = jnp.exp(inter - mt)
        vaug = jnp.concatenate([xv[:, h * ML_DV:(h + 1) * ML_DV], one_col], axis=-1)
        num = _dot(wd.astype(BF16), vaug) + a * _dot(qh, state_b)
        den = num[:, ML_DV:ML_DV + 1]
        hh = num[:, 0:ML_DV] / jnp.maximum(jnp.abs(den), jnp.exp(-mt))
        hs = slice(h * ML_DV, (h + 1) * ML_DV)
        h_ref[:, hs] = (_rms(hh, gain[:, hs]) * og[:, hs]).astype(h_ref.dtype)
        m_new = mt[c_len - 1:c_len, :]
        f_last = fc[c_len - 1:c_len, :]
        gk = (jnp.where(head_of_lane == h, kf, 0.0) * jnp.exp(f_last - fc + ic - m_new)).astype(BF16)
        upd = upd + _dot_tn(gk, vaug)
        decay_col = decay_col + jnp.where(head_of_row == h, jnp.exp(f_last + m_prev - m_new), 0.0)
        m_next = m_next + jnp.where(m_lane == h, m_new, 0.0)
    c_scr[...] = decay_col * state + upd
    m_scr[...] = m_next

    @pl.when(c == nc - 1)
    def _():
        cout_ref[...] = c_scr[...]
        mout_ref[...] = m_scr[...]


def _mlstm_scan(xqk, xv, og, gcol, grow, cw, cb, prev, c0, m0, gain, j, b, l_pad, c_len, l_last):
    nc = l_pad // c_len
    w = 2 * ML_QK
    chunk = lambda n: pl.BlockSpec((c_len, n), lambda bi, ci: (bi * nc + ci, 0))
    per_b = lambda n2, n3: pl.BlockSpec((None, n2, n3), lambda bi, ci: (bi, 0, 0))
    tri = jnp.tril(jnp.ones((c_len, c_len), F32))
    return pl.pallas_call(
        functools.partial(_mlstm_body, c_len=c_len, nc=nc, l_last=l_last),
        grid=(b, nc),
        in_specs=[chunk(w), chunk(ML_H * ML_DV), chunk(ML_H * ML_DV), chunk(LANES),
                  pl.BlockSpec((None, None, 8, c_len), lambda bi, ci: (bi, ci, 0, 0)),
                  _layer((CONV_W, w), j), _layer((1, w), j), per_b(CONV_W - 1, w),
                  per_b(ML_QK, 2 * LANES), per_b(1, ML_H), _layer((1, ML_H * ML_DV), j),
                  _const((c_len, c_len)), _const((c_len, c_len))],
        out_specs=[chunk(ML_H * ML_DV), per_b(ML_QK, 2 * LANES), per_b(1, ML_H), per_b(CONV_W - 1, w)],
        out_shape=[jax.ShapeDtypeStruct((b * l_pad, ML_H * ML_DV), BF16),
                   jax.ShapeDtypeStruct((b, ML_QK, 2 * LANES), F32),
                   jax.ShapeDtypeStruct((b, 1, ML_H), F32),
                   jax.ShapeDtypeStruct((b, CONV_W - 1, w), F32)],
        scratch_shapes=[pltpu.VMEM((c_len + 8, w), F32), pltpu.VMEM((ML_QK, 2 * LANES), F32),
                        pltpu.VMEM((1, ML_H), F32)],
        compiler_params=_params("parallel", "arbitrary"),
        name="mlstm_scan",
    )(xqk, xv, og, gcol, grow, cw, cb, prev, c0, m0, gain, tri, tri.T)


def _prepare(norm_ffn1, norm_mix, norm_ffn2, ffn1_w_in, ffn1_w_out, ffn2_w_in, ffn2_w_out, even_w_in, even_w_out,
             mla_q_norm, mla_w_uq, mla_kv_norm, mla_w_uk, mla_w_uv, mla_qn_norm, mla_qr_norm, mla_kn_norm,
             mla_kr_norm, hgrn_lb_raw, hgrn_out_norm, odd_w_in, odd_w_out, moba_q_norm, moba_k_norm,
             mlstm_conv_w, mlstm_conv_b, mlstm_b_i, mlstm_b_f, mlstm_out_norm):
    ne, no = even_w_in.shape[0], odd_w_in.shape[0]
    row = lambda t: t[:, None, :].astype(F32)
    P = dict(norm_ffn1=row(norm_ffn1), norm_mix=row(norm_mix), norm_ffn2=row(norm_ffn2),
             ffn1_w_in=ffn1_w_in.astype(BF16), ffn1_w_out=ffn1_w_out.astype(BF16),
             ffn2_w_in=ffn2_w_in.astype(BF16), ffn2_w_out=ffn2_w_out.astype(BF16),
             even_w_out=even_w_out.astype(BF16), odd_w_out=odd_w_out.astype(BF16))
    w = even_w_in
    kr = jnp.zeros((ne, D_MODEL, LANES), F32).at[:, :, 64:96].set(w[:, :, 384:416])
    P["even_w"] = jnp.concatenate([w[:, :, 0:384], kr, w[:, :, 416:]], axis=-1).astype(BF16)
    pad_heads = lambda t, n: jnp.pad(t, ((0, 0), (0, 0), (0, 0), (0, LANES - n))).reshape(ne, t.shape[1], 8 * LANES)
    P["mla_wuq"] = pad_heads(mla_w_uq, MLA_DN + MLA_DR).astype(BF16)
    P["mla_wuk"] = pad_heads(mla_w_uk, MLA_DN).astype(BF16)
    P["mla_wukt"] = jnp.swapaxes(mla_w_uk.reshape(ne, MLA_R, MLA_H * MLA_DN), 1, 2).astype(BF16)
    P["mla_wuv_flat"] = mla_w_uv.reshape(ne, MLA_R, MLA_H * MLA_DV).astype(BF16)
    wuv = jnp.transpose(mla_w_uv, (0, 2, 1, 3))
    odd_head = (jnp.arange(MLA_H) % 2 == 1)[None, :, None, None]
    P["mla_wuv"] = jnp.where(odd_head, jnp.pad(wuv, ((0, 0),) * 3 + ((MLA_DV, 0),)),
                             jnp.pad(wuv, ((0, 0),) * 3 + ((0, MLA_DV),))).astype(BF16)
    z32 = jnp.zeros((ne, LANES - MLA_DN - MLA_DR), F32)
    P["mla_q_norm"], P["mla_kv_norm"] = row(mla_q_norm), row(mla_kv_norm)
    P["mla_gq"] = row(jnp.concatenate([mla_qn_norm, mla_qr_norm, z32], axis=-1))
    P["mla_gk"] = row(jnp.concatenate([mla_kn_norm, jnp.zeros((ne, LANES - MLA_DN), F32)], axis=-1))
    P["mla_gkr"] = row(jnp.concatenate([jnp.zeros((ne, MLA_DN), F32), mla_kr_norm, z32], axis=-1))
    P["mla_kn_gain"] = mla_kn_norm.astype(F32)
    seg = jnp.arange(LANES)
    seg_id = jnp.where(seg < MLA_DN, 0, jnp.where(seg < MLA_DN + MLA_DR, 1, 2 + seg))
    seg_len = jnp.where(seg < MLA_DN, MLA_DN, MLA_DR).astype(F32)
    P["bseg"] = jnp.where(seg_id[:, None] == seg_id[None, :], 1.0 / seg_len[None, :], 0.0).astype(BF16)
    p_lb = jax.nn.softmax(hgrn_lb_raw.astype(F32), axis=0)
    lb = jnp.maximum(jnp.cumsum(p_lb, axis=0) - p_lb[0], 0.0)
    P["hg_loglb"], P["hg_log1mlb"], P["hg_1mlb"] = row(jnp.log(lb)), row(jnp.log1p(-lb)), row(1.0 - lb)
    P["hg_gain"] = row(jnp.tile(hgrn_out_norm, (1, HG_H)))
    w = odd_w_in
    c0 = MB_H * MB_DH
    c1 = c0 + MB_KVH * MB_DH
    c2 = c1 + MB_KVH * MB_DH
    c3 = c2 + 2 * ML_QK
    c4 = c3 + ML_H * ML_DV
    c5 = c4 + 2 * ML_H
    pad_h = lambda t, nh: jnp.pad(t.reshape(no, D_MODEL, nh, MB_DH),
                                  ((0, 0), (0, 0), (0, 0), (0, LANES - MB_DH))).reshape(no, D_MODEL, nh * LANES)
    gates = jnp.pad(w[:, :, c4:c5], ((0, 0), (0, 0), (0, LANES - 2 * ML_H)))
    P["odd_w"] = jnp.concatenate([pad_h(w[:, :, 0:c0], MB_H), pad_h(w[:, :, c0:c1], MB_KVH),
                                  pad_h(w[:, :, c1:c2], MB_KVH), w[:, :, c2:c4], w[:, :, c5:], gates],
                                 axis=-1).astype(BF16)
    zpad = jnp.zeros((no, LANES - MB_DH), F32)
    P["moba_gq"] = row(jnp.concatenate([moba_q_norm, zpad], axis=-1))
    P["moba_gk"] = row(jnp.concatenate([moba_k_norm, zpad], axis=-1))
    P["mlstm_gb"] = row(jnp.concatenate([mlstm_b_i, mlstm_b_f, jnp.zeros((no, LANES - 2 * ML_H), F32)], axis=-1))
    P["mlstm_cw"] = mlstm_conv_w.astype(F32)
    P["mlstm_cb"] = row(mlstm_conv_b)
    P["mlstm_gain"] = row(jnp.tile(mlstm_out_norm, (1, ML_H)))
    return P


def _rope_tables(pos, n_rows):
    half = MLA_DR // 2
    freqs = ROPE_THETA ** (-jnp.arange(half, dtype=F32) / half)
    ang = pos.astype(F32)[:, None] * freqs[None, :]
    cos, sin = jnp.cos(ang), jnp.sin(ang)
    n = pos.shape[0]
    z = lambda k: jnp.zeros((n, k), F32)
    cos_t = jnp.concatenate([jnp.ones((n, MLA_DN), F32), cos, cos, z(LANES - MLA_DN - MLA_DR)], axis=-1)
    sin_a = jnp.concatenate([z(MLA_DN + half), sin, z(LANES - MLA_DN - MLA_DR)], axis=-1)
    sin_b = jnp.concatenate([z(MLA_DN), -sin, z(LANES - MLA_DN - half)], axis=-1)
    rep = n_rows // n
    return tuple(jnp.tile(t, (rep, 1)) for t in (cos_t, sin_a, sin_b))


def _pad_time(t, b, l, l_pad, value=0.0):
    if l_pad == l:
        return t
    n = t.shape[-1]
    fill = jnp.broadcast_to(jnp.asarray(value, t.dtype), (b, l_pad - l, n))
    return jnp.concatenate([t.reshape(b, l, n), fill], axis=1).reshape(b * l_pad, n)


def _even_layer(x, P, e, l, b, seq, tabs, ctx):
    qcat, kcat, rows, vlat, hq, hk, hlf, hi, hg = _even_in(x, P, e, l, tabs)
    if ctx is None:
        a = _mla_prompt(qcat, kcat, vlat, P["mla_wuv"], e, b, seq)
        s0 = jnp.zeros((b, HG_H, HG_DK, HG_DV), F32)
        c_len, l_pad = min(64, seq), seq
    else:
        q3 = qcat.reshape(b, MLA_H, LANES).astype(F32)
        eye = jnp.eye(MLA_H, dtype=F32)
        qbd = jnp.einsum("bhd,hg->bhgd", q3[:, :, :MLA_DN] * P["mla_kn_gain"][e][None, None, :], eye)
        a = _mla_sample(ctx["page_table"], ctx["cache_mla"], e, qbd.reshape(b, MLA_H, MLA_H * MLA_DN),
                        q3[:, :, MLA_DN:MLA_DN + MLA_DR], qcat.reshape(b, MLA_H, LANES),
                        kcat.reshape(b, MLA_H, LANES), vlat.reshape(b, 1, MLA_R),
                        P["mla_wukt"], P["mla_wuv_flat"])
        s0 = ctx["state_hgrn"][e].astype(F32)
        c_len = l_pad = 16
    pad = lambda t: _pad_time(t, b, seq, l_pad)
    o, s_new = _hgrn_scan(pad(hq), pad(hk), pad(hlf), pad(hi), pad(hg), s0, P["hg_gain"][e], b, l_pad, c_len)
    if l_pad != seq:
        o = o.reshape(b, l_pad, HG_W)[:, :seq].reshape(b * seq, HG_W)
    y = _out_proj(x, a, o, P["even_w_out"], e)
    return y, rows.reshape(b, seq, MLA_R + MLA_DR), s_new


def _odd_layer(x, P, j, l, b, seq, ctx):
    qpad, kpad, vpad, xqk, xv, og, gates = _odd_in(x, P, j, l)
    unpad = lambda t: t.reshape(b, seq, MB_KVH, LANES)[..., :MB_DH]
    k_new, v_new = unpad(kpad), unpad(vpad)
    if ctx is None:
        a = _moba_prompt(qpad, kpad, vpad, b, seq)
        prev = jnp.zeros((b, CONV_W - 1, 2 * ML_QK), F32)
        caug0 = jnp.zeros((b, ML_QK, 2 * LANES), F32)
        m0 = jnp.zeros((b, 1, ML_H), F32)
        c_len, l_pad = min(64, seq), seq
    else:
        assert seq == 1
        pt = ctx["page_table"]
        n_pool = ctx["cache_moba_k"].shape[1]
        ck = ctx["cache_moba_k"].reshape(-1, n_pool, PAGE, MB_KVH * MB_DH)
        cv = ctx["cache_moba_v"].reshape(-1, n_pool, PAGE, MB_KVH * MB_DH)
        qh = qpad.reshape(b, MB_H, LANES)[:, :, :MB_DH].astype(F32)
        kv_of_head = jnp.arange(MB_H) // (MB_H // MB_KVH)
        on_kv = jax.nn.one_hot(kv_of_head, MB_KVH, dtype=F32)
        qsel = jnp.einsum("bhd,hg->bhgd", qh, on_kv).reshape(b, MB_H, MB_KVH * MB_DH)
        sel = _moba_select(pt, ck, j, qsel)
        on_half = jax.nn.one_hot(kv_of_head % 2, 2, dtype=F32)
        place = lambda t: jnp.einsum("bhd,hg->bhgd", t, on_half).reshape(b, MB_H, 1, LANES)
        a = _moba_sample(pt, sel, ck, cv, j, place(qh), place(k_new[:, 0][:, kv_of_head]),
                         place(v_new[:, 0][:, kv_of_head]))
        a = a.reshape(b, MB_H, 2, MB_DH)
        a = jnp.take_along_axis(a, (kv_of_head % 2)[None, :, None, None], axis=2)
        a = a.reshape(b, MB_H * MB_DH).astype(BF16)
        prev = ctx["state_mlstm_conv"][j].astype(F32)
        caug0 = jnp.concatenate([ctx["state_mlstm_c"][j].astype(F32),
                                 ctx["state_mlstm_n"][j].astype(F32)[..., None],
                                 jnp.zeros((b, ML_H, ML_DK, LANES - 1), F32)], axis=-1).reshape(b, ML_QK, 2 * LANES)
        m0 = ctx["state_mlstm_m"][j].astype(F32).reshape(b, 1, ML_H)
        c_len = l_pad = 16
    nc = l_pad // c_len
    gate_fill = jnp.where(jnp.arange(LANES) < ML_H, NEG, 0.0)
    gcol = _pad_time(gates, b, seq, l_pad, gate_fill)
    grow = jnp.swapaxes(gcol[:, :8].reshape(b, nc, c_len, 8), 2, 3)
    pad = lambda t: _pad_time(t, b, seq, l_pad)
    l_last = seq - (nc - 1) * c_len
    h, caug, m_new, conv_new = _mlstm_scan(pad(xqk), pad(xv), pad(og), gcol, grow, P["mlstm_cw"], P["mlstm_cb"],
                                           prev, caug0, m0, P["mlstm_gain"], j, b, l_pad, c_len, l_last)
    if l_pad != seq:
        h = h.reshape(b, l_pad, ML_H * ML_DV)[:, :seq].reshape(b * seq, ML_H * ML_DV)
    y = _out_proj(x, a, h, P["odd_w_out"], j)
    caug = caug.reshape(b, ML_H, ML_DK, 2 * LANES)
    return y, k_new, v_new, caug[..., :ML_DV], caug[..., ML_DV], m_new.reshape(b, ML_H), conv_new


def _trunk(x, pos, P, ctx):
    b, seq, _ = x.shape
    m = b * seq
    x = x.reshape(m, D_MODEL).astype(F32)
    depth = P["norm_mix"].shape[0]
    tabs = _rope_tables(pos, max(_row_tile(m), seq))
    new = {k: [] for k in ("mla", "hgrn", "moba_k", "moba_v", "mlstm_c", "mlstm_n", "mlstm_m", "mlstm_conv")}
    for l in range(depth):
        x = _ffn(x, P["norm_ffn1"], P["ffn1_w_in"], P["ffn1_w_out"], l)
        if l % 2 == 0:
            x, rows, s = _even_layer(x, P, l // 2, l, b, seq, tabs, ctx)
            new["mla"].append(rows)
            new["hgrn"].append(s)
        else:
            x, k, v, c, n, mm, cv = _odd_layer(x, P, l // 2, l, b, seq, ctx)
            for name, val in zip(("moba_k", "moba_v", "mlstm_c", "mlstm_n", "mlstm_m", "mlstm_conv"),
                                 (k, v, c, n, mm, cv)):
                new[name].append(val)
        x = _ffn(x, P["norm_ffn2"], P["ffn2_w_in"], P["ffn2_w_out"], l)
    return x.reshape(b, seq, D_MODEL), {k: jnp.stack(v).astype(F32) for k, v in new.items()}


def kernel(x_prompt, x_sample, cache_mla, state_hgrn, cache_moba_k, cache_moba_v, state_mlstm_c, state_mlstm_n, state_mlstm_m, state_mlstm_conv, page_table, norm_ffn1, norm_mix, norm_ffn2, ffn1_w_in, ffn1_w_out, ffn2_w_in, ffn2_w_out, even_w_in, even_w_out, mla_q_norm, mla_w_uq, mla_kv_norm, mla_w_uk, mla_w_uv, mla_qn_norm, mla_qr_norm, mla_kn_norm, mla_kr_norm, hgrn_lb_raw, hgrn_out_norm, odd_w_in, odd_w_out, moba_q_norm, moba_k_norm, mlstm_conv_w, mlstm_conv_b, mlstm_b_i, mlstm_b_f, mlstm_out_norm):
    P = _prepare(norm_ffn1, norm_mix, norm_ffn2, ffn1_w_in, ffn1_w_out, ffn2_w_in, ffn2_w_out, even_w_in,
                 even_w_out, mla_q_norm, mla_w_uq, mla_kv_norm, mla_w_uk, mla_w_uv, mla_qn_norm, mla_qr_norm,
                 mla_kn_norm, mla_kr_norm, hgrn_lb_raw, hgrn_out_norm, odd_w_in, odd_w_out, moba_q_norm,
                 moba_k_norm, mlstm_conv_w, mlstm_conv_b, mlstm_b_i, mlstm_b_f, mlstm_out_norm)
    ctx = dict(page_table=page_table.astype(I32), cache_mla=cache_mla, state_hgrn=state_hgrn,
               cache_moba_k=cache_moba_k, cache_moba_v=cache_moba_v, state_mlstm_c=state_mlstm_c,
               state_mlstm_n=state_mlstm_n, state_mlstm_m=state_mlstm_m, state_mlstm_conv=state_mlstm_conv)
    y_p, sp = _trunk(x_prompt, jnp.arange(x_prompt.shape[1]), P, None)
    past_len = page_table.shape[1] * PAGE
    y_s, ss = _trunk(x_sample, past_len + jnp.arange(x_sample.shape[1]), P, ctx)
    names = ("mla", "hgrn", "moba_k", "moba_v", "mlstm_c", "mlstm_n", "mlstm_m", "mlstm_conv")
    out = [y_p, y_s]
    for name in names:
        out += [sp[name], ss[name]]
    return tuple(out)
```

```python
import functools

import jax
import jax.numpy as jnp
from jax import lax
from jax.experimental import pallas as pl
from jax.experimental.pallas import tpu as pltpu

F32, BF16, I32 = jnp.float32, jnp.bfloat16, jnp.int32
HI = lax.Precision.HIGHEST
EPS = 1e-6
NEG = -1e30

D_MODEL = 1024
FFN_HIDDEN = 2048
PAGE = 128
MLA_H, MLA_QR, MLA_R, MLA_DN, MLA_DR, MLA_DV = 8, 256, 128, 64, 32, 64
ROPE_THETA = 10000.0
MLA_SCALE = (MLA_DN + MLA_DR) ** -0.5
HG_H, HG_DK, HG_DV = 4, 128, 128
HG_W = HG_H * HG_DK
MB_H, MB_KVH, MB_DH, MB_BLOCK, MB_TOPK = 8, 4, 64, 256, 3
MB_SCALE = MB_DH ** -0.5
ML_H, ML_DK, ML_DV, CONV_W = 4, 64, 128, 4
ML_QK = ML_H * ML_DK

LANES = 128
VMEM_LIMIT = 56 << 20


def _dot(a, b):
    return jnp.dot(a, b, preferred_element_type=F32)


def _dot_nt(a, b):
    return lax.dot_general(a, b, (((1,), (1,)), ((), ())), preferred_element_type=F32)


def _dot_tn(a, b):
    return lax.dot_general(a, b, (((0,), (0,)), ((), ())), preferred_element_type=F32)


def _rms(x, g):
    return x * lax.rsqrt(jnp.mean(x * x, axis=-1, keepdims=True) + EPS) * g


def _sigmoid(x):
    return 1.0 / (1.0 + jnp.exp(-x))


def _silu(x):
    return x * _sigmoid(x)


def _rows(tm, n):
    return pl.BlockSpec((tm, n), lambda i: (i, 0))


def _layer(shape, l):
    zeros = (0,) * len(shape)
    return pl.BlockSpec((None,) + tuple(shape), lambda *_: (l,) + zeros)


def _const(shape):
    zeros = (0,) * len(shape)
    return pl.BlockSpec(tuple(shape), lambda *_: zeros)


def _params(*sem):
    return pltpu.CompilerParams(dimension_semantics=sem, vmem_limit_bytes=VMEM_LIMIT)


def _row_tile(m):
    return min(256, m)


def _ffn_body(x_ref, g_ref, win_ref, wout_ref, o_ref, *, fc):
    x = x_ref[...]
    hn = _rms(x, g_ref[...]).astype(BF16)
    acc = jnp.zeros_like(x)
    for c in range(FFN_HIDDEN // fc):
        a = _dot(hn, win_ref[:, c * fc:(c + 1) * fc])
        b = _dot(hn, win_ref[:, FFN_HIDDEN + c * fc:FFN_HIDDEN + (c + 1) * fc])
        acc = acc + _dot((_silu(a) * b).astype(BF16), wout_ref[c * fc:(c + 1) * fc, :])
    o_ref[...] = x + 0.5 * acc


def _ffn(x, g, w_in, w_out, l):
    m = x.shape[0]
    tm = _row_tile(m)
    return pl.pallas_call(
        functools.partial(_ffn_body, fc=512),
        grid=(m // tm,),
        in_specs=[_rows(tm, D_MODEL), _layer((1, D_MODEL), l),
                  _layer((D_MODEL, 2 * FFN_HIDDEN), l), _layer((FFN_HIDDEN, D_MODEL), l)],
        out_specs=_rows(tm, D_MODEL),
        out_shape=jax.ShapeDtypeStruct((m, D_MODEL), F32),
        compiler_params=_params("parallel"),
        name="ffn",
    )(x, g, w_in, w_out)


EV_COLS = 256 + 128 + 128 + 4 * HG_W


def _even_in_body(x_ref, g_ref, w_ref, qg_ref, wuq_ref, gq_ref, kvg_ref, wuk_ref, gk_ref, gkr_ref, bseg_ref,
                  cos_ref, sa_ref, sb_ref, llb_ref, l1m_ref, omlb_ref,
                  qcat_ref, kcat_ref, rows_ref, vlat_ref, hq_ref, hk_ref, hlf_ref, hi_ref, hg_ref):
    hn = _rms(x_ref[...], g_ref[...]).astype(BF16)
    cos_t, sin_a, sin_b = cos_ref[...], sa_ref[...], sb_ref[...]
    bseg = bseg_ref[...]

    def rope(t):
        return t * cos_t + pltpu.roll(t, 16, 1) * sin_a + pltpu.roll(t, LANES - 16, 1) * sin_b

    def segnorm(t, gain):
        ms = _dot((t * t).astype(BF16), bseg)
        return t * lax.rsqrt(ms + EPS) * gain

    ql = _dot(hn, w_ref[:, 0:256])
    qh = _dot(_rms(ql, qg_ref[...]).astype(BF16), wuq_ref[...])
    gq = gq_ref[...]
    for h in range(MLA_H):
        t = segnorm(qh[:, h * LANES:(h + 1) * LANES], gq)
        qcat_ref[:, h * LANES:(h + 1) * LANES] = (rope(t) * MLA_SCALE).astype(BF16)

    cn = _rms(_dot(hn, w_ref[:, 256:384]), kvg_ref[...])
    krl = _dot(hn, w_ref[:, 384:512])
    ms = jnp.sum(krl * krl, axis=-1, keepdims=True) * (1.0 / MLA_DR)
    krr = rope(krl * lax.rsqrt(ms + EPS) * gkr_ref[...])
    rows_ref[:, 0:MLA_R] = cn
    rows_ref[:, MLA_R:MLA_R + MLA_DR] = pltpu.roll(krr, 64, 1)[:, 0:MLA_DR]
    cnb = cn.astype(BF16)
    vlat_ref[...] = cnb
    kn = _dot(cnb, wuk_ref[...])
    gk = gk_ref[...]
    for h in range(MLA_H):
        t = segnorm(kn[:, h * LANES:(h + 1) * LANES], gk)
        kcat_ref[:, h * LANES:(h + 1) * LANES] = (t + krr).astype(BF16)

    o = 512
    hq_ref[...] = _silu(_dot(hn, w_ref[:, o:o + HG_W])).astype(hq_ref.dtype)
    hf = _dot(hn, w_ref[:, o + HG_W:o + 2 * HG_W])
    logsig = jnp.minimum(hf, 0.0) - jnp.log(1.0 + jnp.exp(-jnp.abs(hf)))
    aa, bb = llb_ref[...], l1m_ref[...] + logsig
    hlf_ref[...] = jnp.maximum(aa, bb) + jnp.log(1.0 + jnp.exp(-jnp.abs(aa - bb)))
    hk_ref[...] = omlb_ref[...] * _sigmoid(-hf)
    hi_ref[...] = _dot(hn, w_ref[:, o + 2 * HG_W:o + 3 * HG_W]).astype(hi_ref.dtype)
    hg_ref[...] = _silu(_dot(hn, w_ref[:, o + 3 * HG_W:o + 4 * HG_W])).astype(hg_ref.dtype)


def _even_in(x, P, e, l, tabs):
    m = x.shape[0]
    tm = _row_tile(m)
    cos_t, sin_a, sin_b = tabs
    nt = cos_t.shape[0] // tm
    tab = pl.BlockSpec((tm, LANES), lambda i: (i % nt, 0))
    outs = [(8 * LANES, BF16), (8 * LANES, BF16), (MLA_R + MLA_DR, F32), (MLA_R, BF16),
            (HG_W, BF16), (HG_W, F32), (HG_W, F32), (HG_W, BF16), (HG_W, BF16)]
    return pl.pallas_call(
        _even_in_body,
        grid=(m // tm,),
        in_specs=[_rows(tm, D_MODEL), _layer((1, D_MODEL), l), _layer((D_MODEL, EV_COLS), e),
                  _layer((1, MLA_QR), e), _layer((MLA_QR, 8 * LANES), e), _layer((1, LANES), e),
                  _layer((1, MLA_R), e), _layer((MLA_R, 8 * LANES), e), _layer((1, LANES), e),
                  _layer((1, LANES), e), _const((LANES, LANES)), tab, tab, tab,
                  _layer((1, HG_W), e), _layer((1, HG_W), e), _layer((1, HG_W), e)],
        out_specs=[_rows(tm, n) for n, _ in outs],
        out_shape=[jax.ShapeDtypeStruct((m, n), dt) for n, dt in outs],
        compiler_params=_params("parallel"),
        name="even_in",
    )(x, P["norm_mix"], P["even_w"], P["mla_q_norm"], P["mla_wuq"], P["mla_gq"], P["mla_kv_norm"],
      P["mla_wuk"], P["mla_gk"], P["mla_gkr"], P["bseg"], cos_t, sin_a, sin_b,
      P["hg_loglb"], P["hg_log1mlb"], P["hg_1mlb"])


def _mla_p_body(q_ref, k_ref, v_ref, wuv_ref, a_ref, m_scr, l_scr, acc_scr, *, tq):
    i = pl.program_id(1)
    row = lax.broadcasted_iota(I32, (tq, tq), 0)
    col = lax.broadcasted_iota(I32, (tq, tq), 1)
    causal = col <= row
    m_scr[...] = jnp.full(m_scr.shape, NEG, F32)
    l_scr[...] = jnp.zeros(l_scr.shape, F32)
    acc_scr[...] = jnp.zeros(acc_scr.shape, F32)

    def step(j, diag):
        start = pl.multiple_of(j * tq, tq)
        v = v_ref[pl.ds(start, tq), :]
        for h in range(MLA_H):
            q = q_ref[:, h * LANES:(h + 1) * LANES]
            k = k_ref[pl.ds(start, tq), h * LANES:(h + 1) * LANES]
            s = _dot_nt(q, k)
            if diag:
                s = jnp.where(causal, s, NEG)
            m_old = m_scr[h]
            m_new = jnp.maximum(m_old, jnp.max(s, axis=-1, keepdims=True))
            alpha = jnp.exp(m_old - m_new)
            p = jnp.exp(s - jnp.concatenate([m_new] * (tq // LANES), axis=-1))
            l_scr[h] = alpha * l_scr[h] + jnp.sum(p, axis=-1, keepdims=True)
            acc_scr[h] = alpha * acc_scr[h] + _dot(p.astype(BF16), v)
            m_scr[h] = m_new

    def body(j, c):
        step(j, False)
        return c

    lax.fori_loop(0, i, body, 0)
    step(i, True)
    for hp in range(MLA_H // 2):
        pair = [_dot((acc_scr[h] / l_scr[h]).astype(BF16), wuv_ref[h]) for h in (2 * hp, 2 * hp + 1)]
        a_ref[:, hp * LANES:(hp + 1) * LANES] = (pair[0] + pair[1]).astype(a_ref.dtype)


def _mla_prompt(qcat, kcat, vlat, wuv, e, b, l):
    tq = min(256, l)
    assert tq % LANES == 0 and l % tq == 0
    nq = l // tq
    stat = pltpu.VMEM((MLA_H, tq, LANES), F32)
    return pl.pallas_call(
        functools.partial(_mla_p_body, tq=tq),
        grid=(b, nq),
        in_specs=[pl.BlockSpec((tq, 8 * LANES), lambda bi, i: (bi * nq + i, 0)),
                  pl.BlockSpec((l, 8 * LANES), lambda bi, i: (bi, 0)),
                  pl.BlockSpec((l, MLA_R), lambda bi, i: (bi, 0)),
                  _layer((MLA_H, MLA_R, LANES), e)],
        out_specs=pl.BlockSpec((tq, MLA_H * MLA_DV), lambda bi, i: (bi * nq + i, 0)),
        out_shape=jax.ShapeDtypeStruct((b * l, MLA_H * MLA_DV), BF16),
        scratch_shapes=[stat, stat, stat],
        compiler_params=_params("parallel", "arbitrary"),
        name="mla_prompt",
    )(qcat, kcat, vlat, wuv)


MLA_PG = 16


def _mla_s_body(pt_ref, qbd_ref, qr_ref, qself_ref, kself_ref, vself_ref, wukt_ref, wuv_ref, *rest, nsteps):
    pages, a_ref = rest[:MLA_PG], rest[MLA_PG]
    lhs_scr, m_scr, l_scr, acc_scr = rest[MLA_PG + 1:]
    j = pl.program_id(1)
    hd = MLA_H * MLA_DN

    @pl.when(j == 0)
    def _():
        wukt = wukt_ref[...]
        lhs_scr[0:hd, :] = wukt
        qa = _dot(qbd_ref[...].astype(BF16), wukt)
        lhs_scr[hd:hd + 16, :] = jnp.concatenate([qa, jnp.zeros_like(qa)], axis=0).astype(BF16)
        m_scr[...] = jnp.full(m_scr.shape, NEG, F32)
        l_scr[...] = jnp.zeros(l_scr.shape, F32)
        acc_scr[...] = jnp.zeros(acc_scr.shape, F32)

    ct = jnp.concatenate([p[0:MLA_R, :] for p in pages], axis=1).astype(BF16)
    krt = jnp.concatenate([p[MLA_R:MLA_R + MLA_DR, :] for p in pages], axis=1).astype(BF16)
    keys = ct.shape[1]
    both = _dot(lhs_scr[...], ct)
    knt = both[0:hd, :]
    ss = jnp.sum((knt * knt).reshape(MLA_H, MLA_DN, keys), axis=1)
    rinv = lax.rsqrt(ss * (1.0 / MLA_DN) + EPS)
    s = rinv * both[hd:hd + MLA_H, :] + _dot(qr_ref[...].astype(BF16), krt)
    m_old = m_scr[...]
    m_new = jnp.maximum(m_old, jnp.max(s, axis=-1, keepdims=True))
    alpha = jnp.exp(m_old - m_new)
    p = jnp.exp(s - m_new)
    l_scr[...] = alpha * l_scr[...] + jnp.sum(p, axis=-1, keepdims=True)
    acc_scr[...] = alpha * acc_scr[...] + _dot_nt(p.astype(BF16), ct)
    m_scr[...] = m_new

    @pl.when(j == nsteps - 1)
    def _():
        s_self = jnp.sum(qself_ref[...].astype(F32) * kself_ref[...].astype(F32), axis=-1, keepdims=True)
        m_old = m_scr[...]
        m_new = jnp.maximum(m_old, s_self)
        alpha = jnp.exp(m_old - m_new)
        p_self = jnp.exp(s_self - m_new)
        l = alpha * l_scr[...] + p_self
        acc = alpha * acc_scr[...] + p_self * vself_ref[...].astype(F32)
        r = _dot((acc / l).astype(BF16), wuv_ref[...])
        rh = lax.broadcasted_iota(I32, r.shape, 0)
        ch = lax.broadcasted_iota(I32, r.shape, 1) // MLA_DV
        a_ref[...] = jnp.sum(jnp.where(rh == ch, r, 0.0), axis=0, keepdims=True).astype(a_ref.dtype)


def _mla_sample(page_table, cache_t, e, qbd, qr, qself, kself, vself, wukt, wuv):
    bs = qbd.shape[0]
    n_pages = page_table.shape[1]
    pg = MLA_PG
    assert n_pages % pg == 0
    nsteps = n_pages // pg
    width = MLA_R + MLA_DR
    hd = MLA_H * MLA_DN
    seq3 = lambda n2, n3: pl.BlockSpec((None, n2, n3), lambda b, j, pt: (b, 0, 0))

    def page_spec(p):
        return pl.BlockSpec((None, None, width, PAGE),
                            lambda b, j, pt: (e, pt[b * n_pages + j * pg + p], 0, 0))

    grid_spec = pltpu.PrefetchScalarGridSpec(
        num_scalar_prefetch=1,
        grid=(bs, nsteps),
        in_specs=[seq3(MLA_H, hd), seq3(MLA_H, MLA_DR), seq3(MLA_H, LANES), seq3(MLA_H, LANES),
                  seq3(1, MLA_R),
                  pl.BlockSpec((None, hd, MLA_R), lambda b, j, pt: (e, 0, 0)),
                  pl.BlockSpec((None, MLA_R, MLA_H * MLA_DV), lambda b, j, pt: (e, 0, 0))]
                 + [page_spec(p) for p in range(pg)],
        out_specs=pl.BlockSpec((None, 1, MLA_H * MLA_DV), lambda b, j, pt: (b, 0, 0)),
        scratch_shapes=[pltpu.VMEM((hd + 16, MLA_R), BF16), pltpu.VMEM((MLA_H, 1), F32),
                        pltpu.VMEM((MLA_H, 1), F32), pltpu.VMEM((MLA_H, MLA_R), F32)],
    )
    out = pl.pallas_call(
        functools.partial(_mla_s_body, nsteps=nsteps),
        grid_spec=grid_spec,
        out_shape=jax.ShapeDtypeStruct((bs, 1, MLA_H * MLA_DV), BF16),
        compiler_params=_params("parallel", "arbitrary"),
        name="mla_sample",
    )(page_table.reshape(-1), qbd, qr, qself, kself, vself, wukt, wuv, *([cache_t] * MLA_PG))
    return out.reshape(bs, MLA_H * MLA_DV)


HG_BS = 32


def _hgrn_body(q_ref, k_ref, lf_ref, v_ref, g_ref, s0_ref, gain_ref, wind_ref, tri_ref,
               o_ref, sout_ref, st_scr, z_scr, *, c_len, bs, nc):
    c = pl.program_id(1)
    nb = c_len // bs

    @pl.when(c == 0)
    def _():
        for h in range(HG_H):
            st_scr[h] = s0_ref[h].T

    b = jnp.dot(tri_ref[...], lf_ref[...], precision=HI, preferred_element_type=F32)
    q = q_ref[...].astype(F32)
    k = k_ref[...]
    v = v_ref[...]
    b_last = b[c_len - 1:c_len, :]

    b3, q3, k3 = (t.reshape(nb, bs, HG_W) for t in (b, q, k))
    for s in range(bs):
        z = q3 * k3[:, s:s + 1, :] * jnp.exp(jnp.minimum(b3 - b3[:, s:s + 1, :], 0.0))
        z = z.reshape(c_len, HG_W).astype(BF16)
        for h in range(HG_H):
            z_scr[h * c_len:(h + 1) * c_len, s * LANES:(s + 1) * LANES] = z[:, h * LANES:(h + 1) * LANES]
    a_diag = _dot(z_scr[...], wind_ref[...])
    t_loc = lax.broadcasted_iota(I32, a_diag.shape, 0) % bs
    s_loc = lax.broadcasted_iota(I32, a_diag.shape, 1)
    a_diag = jnp.where(t_loc >= s_loc, a_diag, 0.0).astype(BF16)

    qe = (q * jnp.exp(b)).astype(BF16)
    kd = (k * jnp.exp(b_last - b)).astype(BF16)
    gain = gain_ref[...]
    g = g_ref[...].astype(F32)
    for h in range(HG_H):
        hs = slice(h * LANES, (h + 1) * LANES)
        st = st_scr[h]
        o_h = _dot_nt(qe[:, hs], st.astype(BF16))
        blocks = []
        for ib in range(nb):
            rs = slice(ib * bs, (ib + 1) * bs)
            oi = _dot(a_diag[h * c_len + ib * bs:h * c_len + (ib + 1) * bs, :], v[rs, hs])
            if ib > 0:
                ref = b[ib * bs - 1:ib * bs, hs]
                qa = (q[rs, hs] * jnp.exp(b[rs, hs] - ref)).astype(BF16)
                ka = (k[0:ib * bs, hs] * jnp.exp(ref - b[0:ib * bs, hs])).astype(BF16)
                oi = oi + _dot(_dot_nt(qa, ka).astype(BF16), v[0:ib * bs, hs])
            blocks.append(oi)
        o_h = o_h + (blocks[0] if nb == 1 else jnp.concatenate(blocks, axis=0))
        st_scr[h] = jnp.exp(b_last[:, hs]) * st + _dot_tn(v[:, hs], kd[:, hs])
        o_ref[:, hs] = (_rms(o_h, gain[:, hs]) * g[:, hs]).astype(o_ref.dtype)

    @pl.when(c == nc - 1)
    def _():
        for h in range(HG_H):
            sout_ref[h] = st_scr[h].T


def _hgrn_scan(hq, hk, hlf, hi, hg, s0, gain, b, l, c_len):
    bs = min(HG_BS, c_len)
    nc = l // c_len
    chunk = lambda: pl.BlockSpec((c_len, HG_W), lambda bi, ci: (bi * nc + ci, 0))
    state = lambda: pl.BlockSpec((None, HG_H, HG_DK, HG_DV), lambda bi, ci: (bi, 0, 0, 0))
    wind = jnp.repeat(jnp.eye(bs, dtype=BF16), LANES, axis=0)
    tri = jnp.tril(jnp.ones((c_len, c_len), F32))
    return pl.pallas_call(
        functools.partial(_hgrn_body, c_len=c_len, bs=bs, nc=nc),
        grid=(b, nc),
        in_specs=[chunk(), chunk(), chunk(), chunk(), chunk(), state(), _const((1, HG_W)),
                  _const((bs * LANES, bs)), _const((c_len, c_len))],
        out_specs=[chunk(), state()],
        out_shape=[jax.ShapeDtypeStruct((b * l, HG_W), BF16),
                   jax.ShapeDtypeStruct((b, HG_H, HG_DK, HG_DV), F32)],
        scratch_shapes=[pltpu.VMEM((HG_H, HG_DV, HG_DK), F32), pltpu.VMEM((HG_H * c_len, bs * LANES), BF16)],
        compiler_params=_params("parallel", "arbitrary"),
        name="hgrn_scan",
    )(hq, hk, hlf, hi, hg, s0, gain, wind, tri)


def _out_body(x_ref, a_ref, o_ref, w_ref, y_ref):
    half = a_ref.shape[1]
    y_ref[...] = x_ref[...] + _dot(a_ref[...], w_ref[0:half, :]) + _dot(o_ref[...], w_ref[half:2 * half, :])


def _out_proj(x, a, o, w, e):
    m = x.shape[0]
    tm = _row_tile(m)
    half = a.shape[1]
    return pl.pallas_call(
        _out_body,
        grid=(m // tm,),
        in_specs=[_rows(tm, D_MODEL), _rows(tm, half), _rows(tm, half), _layer((2 * half, D_MODEL), e)],
        out_specs=_rows(tm, D_MODEL),
        out_shape=jax.ShapeDtypeStruct((m, D_MODEL), F32),
        compiler_params=_params("parallel"),
        name="out_proj",
    )(x, a, o, w)


OD_Q, OD_K, OD_V = 0, 8 * LANES, 12 * LANES
OD_XQK, OD_XV, OD_OG, OD_G = 16 * LANES, 20 * LANES, 24 * LANES, 28 * LANES
OD_COLS = 29 * LANES


def _odd_in_body(x_ref, g_ref, w_ref, gq_ref, gk_ref, gb_ref,
                 qpad_ref, kpad_ref, vpad_ref, xqk_ref, xv_ref, og_ref, gates_ref):
    hn = _rms(x_ref[...], g_ref[...]).astype(BF16)

    def headnorm(t, gain):
        ms = jnp.sum(t * t, axis=-1, keepdims=True) * (1.0 / MB_DH)
        return t * lax.rsqrt(ms + EPS) * gain

    gq, gk = gq_ref[...], gk_ref[...]
    for h in range(MB_H):
        t = _dot(hn, w_ref[:, OD_Q + h * LANES:OD_Q + (h + 1) * LANES])
        qpad_ref[:, h * LANES:(h + 1) * LANES] = (headnorm(t, gq) * MB_SCALE).astype(BF16)
    for h in range(MB_KVH):
        t = _dot(hn, w_ref[:, OD_K + h * LANES:OD_K + (h + 1) * LANES])
        kpad_ref[:, h * LANES:(h + 1) * LANES] = headnorm(t, gk)
    vpad_ref[...] = _dot(hn, w_ref[:, OD_V:OD_V + 4 * LANES])
    xqk_ref[...] = _dot(hn, w_ref[:, OD_XQK:OD_XQK + 4 * LANES])
    xv_ref[...] = _dot(hn, w_ref[:, OD_XV:OD_XV + 4 * LANES]).astype(xv_ref.dtype)
    og_ref[...] = _sigmoid(_dot(hn, w_ref[:, OD_OG:OD_OG + 4 * LANES])).astype(og_ref.dtype)
    gt = _dot(hn, w_ref[:, OD_G:OD_G + LANES]) + gb_ref[...]
    lane = lax.broadcasted_iota(I32, gt.shape, 1)
    logsig = jnp.minimum(gt, 0.0) - jnp.log(1.0 + jnp.exp(-jnp.abs(gt)))
    gates_ref[...] = jnp.where(lane < ML_H, gt, logsig)


def _odd_in(x, P, j, l):
    m = x.shape[0]
    tm = _row_tile(m)
    outs = [(8 * LANES, BF16), (4 * LANES, F32), (4 * LANES, F32), (2 * ML_QK, F32),
            (ML_H * ML_DV, BF16), (ML_H * ML_DV, BF16), (LANES, F32)]
    return pl.pallas_call(
        _odd_in_body,
        grid=(m // tm,),
        in_specs=[_rows(tm, D_MODEL), _layer((1, D_MODEL), l), _layer((D_MODEL, OD_COLS), j),
                  _layer((1, LANES), j), _layer((1, LANES), j), _layer((1, LANES), j)],
        out_specs=[_rows(tm, n) for n, _ in outs],
        out_shape=[jax.ShapeDtypeStruct((m, n), dt) for n, dt in outs],
        compiler_params=_params("parallel"),
        name="odd_in",
    )(x, P["norm_mix"], P["odd_w"], P["moba_gq"], P["moba_gk"], P["mlstm_gb"])


def _moba_p_body(q_ref, k_ref, v_ref, a_ref, kmean_scr, qa_scr, m_scr, l_scr, acc_scr, *, nb):
    i = pl.program_id(1)
    blk = MB_BLOCK
    row = lax.broadcasted_iota(I32, (blk, blk), 0)
    col = lax.broadcasted_iota(I32, (blk, blk), 1)
    causal = col <= row
    lane = lax.broadcasted_iota(I32, (blk, LANES), 1)

    @pl.when(i == 0)
    def _():
        kmean_scr[...] = jnp.mean(k_ref[...].reshape(nb, blk, MB_KVH * LANES), axis=1)

    kmean = kmean_scr[...]
    blk_id = lax.broadcasted_iota(I32, (nb, blk), 0)
    cand = jnp.where(lax.broadcasted_iota(I32, (nb, 1), 0) < i, 1.0, 0.0)
    place = jnp.where(lax.broadcasted_iota(I32, (nb, LANES), 1) == lax.broadcasted_iota(I32, (nb, LANES), 0) + MB_DH,
                      1.0, 0.0).astype(BF16)
    for h in range(MB_H):
        g = h // (MB_H // MB_KVH)
        q = q_ref[:, h * LANES:(h + 1) * LANES]
        gate = lax.dot_general(kmean[:, g * LANES:(g + 1) * LANES], q.astype(F32), (((1,), (1,)), ((), ())),
                               precision=HI, preferred_element_type=F32)
        rank = jnp.zeros((nb, blk), F32)
        for c in range(nb - 1):
            gc = gate[c:c + 1, :]
            better = (gc > gate) | ((gc == gate) & (c < blk_id))
            rank = rank + jnp.where(better, cand[c:c + 1, :], 0.0)
        bias = jnp.where(rank < MB_TOPK, cand, 0.0) * (-NEG) + NEG
        qa_scr[h] = (q.astype(F32) + _dot_tn(bias.astype(BF16), place)).astype(BF16)
    m_scr[...] = jnp.full(m_scr.shape, NEG, F32)
    l_scr[...] = jnp.zeros(l_scr.shape, F32)
    acc_scr[...] = jnp.zeros(acc_scr.shape, F32)

    def step(j, diag):
        start = pl.multiple_of(j * blk, blk)
        for g in range(MB_KVH):
            gs = slice(g * LANES, (g + 1) * LANES)
            kj = k_ref[pl.ds(start, blk), gs]
            if not diag:
                kj = jnp.where(lane == MB_DH + j, 1.0, kj)
            kj = kj.astype(BF16)
            vj = v_ref[pl.ds(start, blk), gs].astype(BF16)
            for h in range(g * (MB_H // MB_KVH), (g + 1) * (MB_H // MB_KVH)):
                s = _dot_nt(qa_scr[h], kj)
                if diag:
                    s = jnp.where(causal, s, NEG)
                m_old = m_scr[h]
                m_new = jnp.maximum(m_old, jnp.max(s, axis=-1, keepdims=True))
                alpha = jnp.exp(m_old - m_new)
                p = jnp.exp(s - jnp.concatenate([m_new] * (blk // LANES), axis=-1))
                l_scr[h] = alpha * l_scr[h] + jnp.sum(p, axis=-1, keepdims=True)
                acc_scr[h] = alpha * acc_scr[h] + _dot(p.astype(BF16), vj)
                m_scr[h] = m_new

    def body(j, c):
        step(j, False)
        return c

    step(i, True)
    lax.fori_loop(0, i, body, 0)
    for h in range(MB_H):
        a_ref[:, h * MB_DH:(h + 1) * MB_DH] = (acc_scr[h] / l_scr[h])[:, 0:MB_DH].astype(a_ref.dtype)


def _moba_prompt(qpad, kpad, vpad, b, l):
    assert l % MB_BLOCK == 0 and l // MB_BLOCK <= LANES - MB_DH
    nb = l // MB_BLOCK
    stat = pltpu.VMEM((MB_H, MB_BLOCK, LANES), F32)
    return pl.pallas_call(
        functools.partial(_moba_p_body, nb=nb),
        grid=(b, nb),
        in_specs=[pl.BlockSpec((MB_BLOCK, 8 * LANES), lambda bi, i: (bi * nb + i, 0)),
                  pl.BlockSpec((l, 4 * LANES), lambda bi, i: (bi, 0)),
                  pl.BlockSpec((l, 4 * LANES), lambda bi, i: (bi, 0))],
        out_specs=pl.BlockSpec((MB_BLOCK, MB_H * MB_DH), lambda bi, i: (bi * nb + i, 0)),
        out_shape=jax.ShapeDtypeStruct((b * l, MB_H * MB_DH), BF16),
        scratch_shapes=[pltpu.VMEM((nb, MB_KVH * LANES), F32), pltpu.VMEM((MB_H, MB_BLOCK, LANES), BF16),
                        stat, stat, stat],
        compiler_params=_params("parallel", "arbitrary"),
        name="moba_prompt",
    )(qpad, kpad, vpad)


MB_SEL_PG = 16


def _moba_sel_body(pt_ref, qsel_ref, *rest, nsteps, n_cand):
    pages, sel_ref, kmt_scr = rest[:MB_SEL_PG], rest[MB_SEL_PG], rest[MB_SEL_PG + 1]
    j = pl.program_id(1)
    ppb = MB_BLOCK // PAGE
    nblk = MB_SEL_PG // ppb
    lane = lax.broadcasted_iota(I32, kmt_scr.shape, 1)

    @pl.when(j == 0)
    def _():
        kmt_scr[...] = jnp.zeros(kmt_scr.shape, F32)

    kmt = kmt_scr[...]
    for ib in range(nblk):
        tot = pages[ib * ppb][...]
        for p in range(1, ppb):
            tot = tot + pages[ib * ppb + p][...]
        mean = jnp.sum(tot, axis=-1, keepdims=True) * (1.0 / MB_BLOCK)
        kmt = jnp.where(lane == j * nblk + ib, mean, kmt)
    kmt_scr[...] = kmt

    @pl.when(j == nsteps - 1)
    def _():
        gate = jnp.dot(qsel_ref[...], kmt_scr[...], precision=HI, preferred_element_type=F32)
        lane_g = lax.broadcasted_iota(I32, gate.shape, 1)
        rank = jnp.zeros(gate.shape, I32)
        for c in range(n_cand):
            gc = gate[:, c:c + 1]
            better = (gc > gate) | ((gc == gate) & (c < lane_g))
            rank = rank + jnp.where(better, 1, 0)
        out_lane = lax.broadcasted_iota(I32, sel_ref.shape, 1)
        out = jnp.zeros(sel_ref.shape, I32)
        for r in range(MB_TOPK):
            idx = jnp.sum(jnp.where((rank == r) & (lane_g < n_cand), lane_g, 0), axis=-1, keepdims=True)
            out = jnp.where(out_lane == r, idx, out)
        sel_ref[...] = out


def _moba_select(page_table, cache_kt, jl, qsel):
    bs = qsel.shape[0]
    n_pages = page_table.shape[1]
    assert n_pages % MB_SEL_PG == 0
    nsteps = n_pages // MB_SEL_PG
    n_cand = n_pages * PAGE // MB_BLOCK
    assert MB_TOPK <= n_cand <= LANES
    width = MB_KVH * MB_DH

    def page_spec(p):
        return pl.BlockSpec((None, None, width, PAGE),
                            lambda b, j, pt: (jl, pt[b * n_pages + j * MB_SEL_PG + p], 0, 0))

    grid_spec = pltpu.PrefetchScalarGridSpec(
        num_scalar_prefetch=1,
        grid=(bs, nsteps),
        in_specs=[pl.BlockSpec((None, MB_H, width), lambda b, j, pt: (b, 0, 0))]
                 + [page_spec(p) for p in range(MB_SEL_PG)],
        out_specs=pl.BlockSpec((None, MB_H, LANES), lambda b, j, pt: (b, 0, 0)),
        scratch_shapes=[pltpu.VMEM((width, LANES), F32)],
    )
    sel = pl.pallas_call(
        functools.partial(_moba_sel_body, nsteps=nsteps, n_cand=n_cand),
        grid_spec=grid_spec,
        out_shape=jax.ShapeDtypeStruct((bs, MB_H, LANES), I32),
        compiler_params=_params("parallel", "arbitrary"),
        name="moba_select",
    )(page_table.reshape(-1), qsel, *([cache_kt] * MB_SEL_PG))
    return sel[:, :, :MB_TOPK]


MB_GRP = MB_H // MB_KVH


def _moba_s_body(pt_ref, sel_ref, q_ref, kself_ref, vself_ref, *rest):
    n = MB_TOPK * (MB_BLOCK // PAGE)
    pages, o_ref = rest[:2 * MB_GRP * n], rest[2 * MB_GRP * n]
    kself, vself = kself_ref[...], vself_ref[...]
    for hh in range(MB_GRP):
        kpages = pages[hh * n:(hh + 1) * n]
        vpages = pages[(MB_GRP + hh) * n:(MB_GRP + hh + 1) * n]
        q1 = q_ref[hh]
        q = jnp.broadcast_to(q1, (8, MB_DH)).astype(BF16)
        scores = [_dot(q, kp[...].astype(BF16)) for kp in kpages]
        s_self = jnp.sum(q1 * kself, axis=-1, keepdims=True)
        m = s_self
        for s in scores:
            m = jnp.maximum(m, jnp.max(s, axis=-1, keepdims=True))
        p_self = jnp.exp(s_self - m)
        l = p_self
        acc = p_self * vself
        for s, vp in zip(scores, vpages):
            p = jnp.exp(s - m)
            l = l + jnp.sum(p, axis=-1, keepdims=True)
            acc = acc + _dot_nt(p.astype(BF16), vp[...].astype(BF16))
        o_ref[hh] = (acc / l)[0:1, :]


def _moba_sample(page_table, sel, cache_kt, cache_vt, jl, q, kself, vself):
    bs = q.shape[0]
    n_pages = page_table.shape[1]
    ppb = MB_BLOCK // PAGE
    kv = lambda: pl.BlockSpec((None, None, 1, MB_DH), lambda b, g, pt, sl: (b, g, 0, 0))
    qo = lambda: pl.BlockSpec((None, MB_GRP, 1, MB_DH), lambda b, g, pt, sl: (b, g, 0, 0))

    def page_spec(hh, r, p):
        return pl.BlockSpec(
            (None, None, MB_DH, PAGE),
            lambda b, g, pt, sl: (jl, pt[b * n_pages + sl[(b * MB_H + g * MB_GRP + hh) * MB_TOPK + r] * ppb + p], g, 0))

    specs = [page_spec(hh, r, p) for hh in range(MB_GRP) for r in range(MB_TOPK) for p in range(ppb)]
    grid_spec = pltpu.PrefetchScalarGridSpec(
        num_scalar_prefetch=2,
        grid=(bs, MB_KVH),
        in_specs=[qo(), kv(), kv()] + specs + specs,
        out_specs=qo(),
    )
    n = len(specs)
    return pl.pallas_call(
        _moba_s_body,
        grid_spec=grid_spec,
        out_shape=jax.ShapeDtypeStruct((bs, MB_H, 1, MB_DH), F32),
        compiler_params=_params("parallel", "arbitrary"),
        name="moba_sample",
    )(page_table.reshape(-1), sel.reshape(-1), q, kself, vself, *([cache_kt] * n), *([cache_vt] * n))


def _mlstm_body(xqk_ref, xv_ref, og_ref, gcol_ref, grow_ref, cw_ref, cb_ref, prev_ref, c0_ref, m0_ref,
                gain_ref, tri_ref, triu_ref,
                h_ref, cout_ref, mout_ref, convout_ref, full_scr, c_scr, m_scr, *, c_len, nc, l_last):
    c = pl.program_id(1)
    pad = 8
    w = 2 * ML_QK

    @pl.when(c == 0)
    def _():
        full_scr[pad - (CONV_W - 1):pad, :] = prev_ref[...]
        c_scr[...] = c0_ref[...]
        m_scr[...] = m0_ref[...]

    full_scr[pad:pad + c_len, :] = xqk_ref[...]
    conv = cb_ref[...]
    for t in range(CONV_W):
        o = pad - (CONV_W - 1) + t
        conv = conv + full_scr[o:o + c_len, :] * cw_ref[t:t + 1, :]

    @pl.when(c == nc - 1)
    def _():
        convout_ref[...] = full_scr[pad + l_last - (CONV_W - 1):pad + l_last, :]

    full_scr[pad - (CONV_W - 1):pad, :] = full_scr[pad + c_len - (CONV_W - 1):pad + c_len, :]

    qk = _silu(conv)
    qf = qk[:, 0:ML_QK]
    kf = qk[:, ML_QK:w] * (ML_DK ** -0.5)
    kb = kf.astype(BF16)
    gcol, grow = gcol_ref[...], grow_ref[...]
    f_col = jnp.dot(tri_ref[...], gcol, precision=HI, preferred_element_type=F32)
    f_row = jnp.dot(grow, triu_ref[...], precision=HI, preferred_element_type=F32)
    row = lax.broadcasted_iota(I32, (c_len, c_len), 0)
    col = lax.broadcasted_iota(I32, (c_len, c_len), 1)
    causal = col <= row
    head_of_lane = lax.broadcasted_iota(I32, (1, ML_QK), 1) // ML_DK
    head_of_row = lax.broadcasted_iota(I32, (ML_QK, 1), 0) // ML_DK
    m_lane = lax.broadcasted_iota(I32, (1, ML_H), 1)
    one_col = jnp.where(lax.broadcasted_iota(I32, (c_len, LANES), 1) == 0, 1.0, 0.0).astype(BF16)
    state = c_scr[...]
    state_b = state.astype(BF16)
    m_all = m_scr[...]
    gain = gain_ref[...]
    xv = xv_ref[...]
    og = og_ref[...].astype(F32)
    decay_col = jnp.zeros((ML_QK, 1), F32)
    upd = jnp.zeros((ML_QK, 2 * LANES), F32)
    m_next = jnp.zeros((1, ML_H), F32)
    for h in range(ML_H):
        fc = f_col[:, ML_H + h:ML_H + h + 1]
        ic = gcol[:, h:h + 1]
        fr = f_row[ML_H + h:ML_H + h + 1, :]
        ir = grow[h:h + 1, :]
        m_prev = m_all[:, h:h + 1]
        dmat = jnp.where(causal, fc - fr + ir, NEG)
        inter = fc + m_prev
        mt = jnp.maximum(jnp.max(dmat, axis=-1, keepdims=True), inter)
        qh = jnp.where(head_of_lane == h, qf, 0.0).astype(BF16)
        wd = jnp.exp(dmat - mt) * _dot_nt(qh, kb)
        a = jnp.exp(inter - mt)
        vaug = jnp.concatenate([xv[:, h * ML_DV:(h + 1) * ML_DV], one_col], axis=-1)
        num = _dot(wd.astype(BF16), vaug) + a * _dot(qh, state_b)
        den = num[:, ML_DV:ML_DV + 1]
        hh = num[:, 0:ML_DV] / jnp.maximum(jnp.abs(den), jnp.exp(-mt))
        hs = slice(h * ML_DV, (h + 1) * ML_DV)
        h_ref[:, hs] = (_rms(hh, gain[:, hs]) * og[:, hs]).astype(h_ref.dtype)
        m_new = mt[c_len - 1:c_len, :]
        f_last = fc[c_len - 1:c_len, :]
        gk = (jnp.where(head_of_lane == h, kf, 0.0) * jnp.exp(f_last - fc + ic - m_new)).astype(BF16)
        upd = upd + _dot_tn(gk, vaug)
        decay_col = decay_col + jnp.where(head_of_row == h, jnp.exp(f_last + m_prev - m_new), 0.0)
        m_next = m_next + jnp.where(m_lane == h, m_new, 0.0)
    c_scr[...] = decay_col * state + upd
    m_scr[...] = m_next

    @pl.when(c == nc - 1)
    def _():
        cout_ref[...] = c_scr[...]
        mout_ref[...] = m_scr[...]


def _mlstm_scan(xqk, xv, og, gcol, grow, cw, cb, prev, c0, m0, gain, j, b, l_pad, c_len, l_last):
    nc = l_pad // c_len
    w = 2 * ML_QK
    chunk = lambda n: pl.BlockSpec((c_len, n), lambda bi, ci: (bi * nc + ci, 0))
    per_b = lambda n2, n3: pl.BlockSpec((None, n2, n3), lambda bi, ci: (bi, 0, 0))
    tri = jnp.tril(jnp.ones((c_len, c_len), F32))
    return pl.pallas_call(
        functools.partial(_mlstm_body, c_len=c_len, nc=nc, l_last=l_last),
        grid=(b, nc),
        in_specs=[chunk(w), chunk(ML_H * ML_DV), chunk(ML_H * ML_DV), chunk(LANES),
                  pl.BlockSpec((None, None, 8, c_len), lambda bi, ci: (bi, ci, 0, 0)),
                  _layer((CONV_W, w), j), _layer((1, w), j), per_b(CONV_W - 1, w),
                  per_b(ML_QK, 2 * LANES), per_b(1, ML_H), _layer((1, ML_H * ML_DV), j),
                  _const((c_len, c_len)), _const((c_len, c_len))],
        out_specs=[chunk(ML_H * ML_DV), per_b(ML_QK, 2 * LANES), per_b(1, ML_H), per_b(CONV_W - 1, w)],
        out_shape=[jax.ShapeDtypeStruct((b * l_pad, ML_H * ML_DV), BF16),
                   jax.ShapeDtypeStruct((b, ML_QK, 2 * LANES), F32),
                   jax.ShapeDtypeStruct((b, 1, ML_H), F32),
                   jax.ShapeDtypeStruct((b, CONV_W - 1, w), F32)],
        scratch_shapes=[pltpu.VMEM((c_len + 8, w), F32), pltpu.VMEM((ML_QK, 2 * LANES), F32),
                        pltpu.VMEM((1, ML_H), F32)],
        compiler_params=_params("parallel", "arbitrary"),
        name="mlstm_scan",
    )(xqk, xv, og, gcol, grow, cw, cb, prev, c0, m0, gain, tri, tri.T)


def _prepare(norm_ffn1, norm_mix, norm_ffn2, ffn1_w_in, ffn1_w_out, ffn2_w_in, ffn2_w_out, even_w_in, even_w_out,
             mla_q_norm, mla_w_uq, mla_kv_norm, mla_w_uk, mla_w_uv, mla_qn_norm, mla_qr_norm, mla_kn_norm,
             mla_kr_norm, hgrn_lb_raw, hgrn_out_norm, odd_w_in, odd_w_out, moba_q_norm, moba_k_norm,
             mlstm_conv_w, mlstm_conv_b, mlstm_b_i, mlstm_b_f, mlstm_out_norm):
    ne, no = even_w_in.shape[0], odd_w_in.shape[0]
    row = lambda t: t[:, None, :].astype(F32)
    P = dict(norm_ffn1=row(norm_ffn1), norm_mix=row(norm_mix), norm_ffn2=row(norm_ffn2),
             ffn1_w_in=ffn1_w_in.astype(BF16), ffn1_w_out=ffn1_w_out.astype(BF16),
             ffn2_w_in=ffn2_w_in.astype(BF16), ffn2_w_out=ffn2_w_out.astype(BF16),
             even_w_out=even_w_out.astype(BF16), odd_w_out=odd_w_out.astype(BF16))
    w = even_w_in
    kr = jnp.zeros((ne, D_MODEL, LANES), F32).at[:, :, 64:96].set(w[:, :, 384:416])
    P["even_w"] = jnp.concatenate([w[:, :, 0:384], kr, w[:, :, 416:]], axis=-1).astype(BF16)
    pad_heads = lambda t, n: jnp.pad(t, ((0, 0), (0, 0), (0, 0), (0, LANES - n))).reshape(ne, t.shape[1], 8 * LANES)
    P["mla_wuq"] = pad_heads(mla_w_uq, MLA_DN + MLA_DR).astype(BF16)
    P["mla_wuk"] = pad_heads(mla_w_uk, MLA_DN).astype(BF16)
    P["mla_wukt"] = jnp.swapaxes(mla_w_uk.reshape(ne, MLA_R, MLA_H * MLA_DN), 1, 2).astype(BF16)
    P["mla_wuv_flat"] = mla_w_uv.reshape(ne, MLA_R, MLA_H * MLA_DV).astype(BF16)
    wuv = jnp.transpose(mla_w_uv, (0, 2, 1, 3))
    odd_head = (jnp.arange(MLA_H) % 2 == 1)[None, :, None, None]
    P["mla_wuv"] = jnp.where(odd_head, jnp.pad(wuv, ((0, 0),) * 3 + ((MLA_DV, 0),)),
                             jnp.pad(wuv, ((0, 0),) * 3 + ((0, MLA_DV),))).astype(BF16)
    z32 = jnp.zeros((ne, LANES - MLA_DN - MLA_DR), F32)
    P["mla_q_norm"], P["mla_kv_norm"] = row(mla_q_norm), row(mla_kv_norm)
    P["mla_gq"] = row(jnp.concatenate([mla_qn_norm, mla_qr_norm, z32], axis=-1))
    P["mla_gk"] = row(jnp.concatenate([mla_kn_norm, jnp.zeros((ne, LANES - MLA_DN), F32)], axis=-1))
    P["mla_gkr"] = row(jnp.concatenate([jnp.zeros((ne, MLA_DN), F32), mla_kr_norm, z32], axis=-1))
    P["mla_kn_gain"] = mla_kn_norm.astype(F32)
    seg = jnp.arange(LANES)
    seg_id = jnp.where(seg < MLA_DN, 0, jnp.where(seg < MLA_DN + MLA_DR, 1, 2 + seg))
    seg_len = jnp.where(seg < MLA_DN, MLA_DN, MLA_DR).astype(F32)
    P["bseg"] = jnp.where(seg_id[:, None] == seg_id[None, :], 1.0 / seg_len[None, :], 0.0).astype(BF16)
    p_lb = jax.nn.softmax(hgrn_lb_raw.astype(F32), axis=0)
    lb = jnp.maximum(jnp.cumsum(p_lb, axis=0) - p_lb[0], 0.0)
    P["hg_loglb"], P["hg_log1mlb"], P["hg_1mlb"] = row(jnp.log(lb)), row(jnp.log1p(-lb)), row(1.0 - lb)
    P["hg_gain"] = row(jnp.tile(hgrn_out_norm, (1, HG_H)))
    w = odd_w_in
    c0 = MB_H * MB_DH
    c1 = c0 + MB_KVH * MB_DH
    c2 = c1 + MB_KVH * MB_DH
    c3 = c2 + 2 * ML_QK
    c4 = c3 + ML_H * ML_DV
    c5 = c4 + 2 * ML_H
    pad_h = lambda t, nh: jnp.pad(t.reshape(no, D_MODEL, nh, MB_DH),
                                  ((0, 0), (0, 0), (0, 0), (0, LANES - MB_DH))).reshape(no, D_MODEL, nh * LANES)
    gates = jnp.pad(w[:, :, c4:c5], ((0, 0), (0, 0), (0, LANES - 2 * ML_H)))
    P["odd_w"] = jnp.concatenate([pad_h(w[:, :, 0:c0], MB_H), pad_h(w[:, :, c0:c1], MB_KVH),
                                  pad_h(w[:, :, c1:c2], MB_KVH), w[:, :, c2:c4], w[:, :, c5:], gates],
                                 axis=-1).astype(BF16)
    zpad = jnp.zeros((no, LANES - MB_DH), F32)
    P["moba_gq"] = row(jnp.concatenate([moba_q_norm, zpad], axis=-1))
    P["moba_gk"] = row(jnp.concatenate([moba_k_norm, zpad], axis=-1))
    P["mlstm_gb"] = row(jnp.concatenate([mlstm_b_i, mlstm_b_f, jnp.zeros((no, LANES - 2 * ML_H), F32)], axis=-1))
    P["mlstm_cw"] = mlstm_conv_w.astype(F32)
    P["mlstm_cb"] = row(mlstm_conv_b)
    P["mlstm_gain"] = row(jnp.tile(mlstm_out_norm, (1, ML_H)))
    return P


def _rope_tables(pos, n_rows):
    half = MLA_DR // 2
    freqs = ROPE_THETA ** (-jnp.arange(half, dtype=F32) / half)
    ang = pos.astype(F32)[:, None] * freqs[None, :]
    cos, sin = jnp.cos(ang), jnp.sin(ang)
    n = pos.shape[0]
    z = lambda k: jnp.zeros((n, k), F32)
    cos_t = jnp.concatenate([jnp.ones((n, MLA_DN), F32), cos, cos, z(LANES - MLA_DN - MLA_DR)], axis=-1)
    sin_a = jnp.concatenate([z(MLA_DN + half), sin, z(LANES - MLA_DN - MLA_DR)], axis=-1)
    sin_b = jnp.concatenate([z(MLA_DN), -sin, z(LANES - MLA_DN - half)], axis=-1)
    rep = n_rows // n
    return tuple(jnp.tile(t, (rep, 1)) for t in (cos_t, sin_a, sin_b))


def _pad_time(t, b, l, l_pad, value=0.0):
    if l_pad == l:
        return t
    n = t.shape[-1]
    fill = jnp.broadcast_to(jnp.asarray(value, t.dtype), (b, l_pad - l, n))
    return jnp.concatenate([t.reshape(b, l, n), fill], axis=1).reshape(b * l_pad, n)


def _even_layer(x, P, e, l, b, seq, tabs, ctx):
    qcat, kcat, rows, vlat, hq, hk, hlf, hi, hg = _even_in(x, P, e, l, tabs)
    if ctx is None:
        a = _mla_prompt(qcat, kcat, vlat, P["mla_wuv"], e, b, seq)
        s0 = jnp.zeros((b, HG_H, HG_DK, HG_DV), F32)
        c_len, l_pad = min(64, seq), seq
    else:
        q3 = qcat.reshape(b, MLA_H, LANES).astype(F32)
        eye = jnp.eye(MLA_H, dtype=F32)
        qbd = jnp.einsum("bhd,hg->bhgd", q3[:, :, :MLA_DN] * P["mla_kn_gain"][e][None, None, :], eye)
        a = _mla_sample(ctx["page_table"], ctx["cache_mla_t"], e, qbd.reshape(b, MLA_H, MLA_H * MLA_DN),
                        q3[:, :, MLA_DN:MLA_DN + MLA_DR], qcat.reshape(b, MLA_H, LANES),
                        kcat.reshape(b, MLA_H, LANES), vlat.reshape(b, 1, MLA_R),
                        P["mla_wukt"], P["mla_wuv_flat"])
        s0 = ctx["state_hgrn"][e].astype(F32)
        c_len = l_pad = 16
    pad = lambda t: _pad_time(t, b, seq, l_pad)
    o, s_new = _hgrn_scan(pad(hq), pad(hk), pad(hlf), pad(hi), pad(hg), s0, P["hg_gain"][e], b, l_pad, c_len)
    if l_pad != seq:
        o = o.reshape(b, l_pad, HG_W)[:, :seq].reshape(b * seq, HG_W)
    y = _out_proj(x, a, o, P["even_w_out"], e)
    return y, rows.reshape(b, seq, MLA_R + MLA_DR), s_new


def _odd_layer(x, P, j, l, b, seq, ctx):
    qpad, kpad, vpad, xqk, xv, og, gates = _odd_in(x, P, j, l)
    unpad = lambda t: t.reshape(b, seq, MB_KVH, LANES)[..., :MB_DH]
    k_new, v_new = unpad(kpad), unpad(vpad)
    if ctx is None:
        a = _moba_prompt(qpad, kpad, vpad, b, seq)
        prev = jnp.zeros((b, CONV_W - 1, 2 * ML_QK), F32)
        caug0 = jnp.zeros((b, ML_QK, 2 * LANES), F32)
        m0 = jnp.zeros((b, 1, ML_H), F32)
        c_len, l_pad = min(64, seq), seq
    else:
        assert seq == 1
        pt = ctx["page_table"]
        qh = qpad.reshape(b, MB_H, LANES)[:, :, :MB_DH].astype(F32)
        on_kv = jax.nn.one_hot(jnp.arange(MB_H) // MB_GRP, MB_KVH, dtype=F32)
        qsel = jnp.einsum("bhd,hg->bhgd", qh, on_kv).reshape(b, MB_H, MB_KVH * MB_DH)
        sel = _moba_select(pt, ctx["cache_moba_kt"], j, qsel)
        a = _moba_sample(pt, sel, ctx["cache_moba_kt"], ctx["cache_moba_vt"], j, qh[:, :, None, :],
                         k_new.reshape(b, MB_KVH, 1, MB_DH), v_new.reshape(b, MB_KVH, 1, MB_DH))
        a = a.reshape(b, MB_H * MB_DH).astype(BF16)
        prev = ctx["state_mlstm_conv"][j].astype(F32)
        caug0 = jnp.concatenate([ctx["state_mlstm_c"][j].astype(F32),
                                 ctx["state_mlstm_n"][j].astype(F32)[..., None],
                                 jnp.zeros((b, ML_H, ML_DK, LANES - 1), F32)], axis=-1).reshape(b, ML_QK, 2 * LANES)
        m0 = ctx["state_mlstm_m"][j].astype(F32).reshape(b, 1, ML_H)
        c_len = l_pad = 16
    nc = l_pad // c_len
    gate_fill = jnp.where(jnp.arange(LANES) < ML_H, NEG, 0.0)
    gcol = _pad_time(gates, b, seq, l_pad, gate_fill)
    grow = jnp.swapaxes(gcol[:, :8].reshape(b, nc, c_len, 8), 2, 3)
    pad = lambda t: _pad_time(t, b, seq, l_pad)
    l_last = seq - (nc - 1) * c_len
    h, caug, m_new, conv_new = _mlstm_scan(pad(xqk), pad(xv), pad(og), gcol, grow, P["mlstm_cw"], P["mlstm_cb"],
                                           prev, caug0, m0, P["mlstm_gain"], j, b, l_pad, c_len, l_last)
    if l_pad != seq:
        h = h.reshape(b, l_pad, ML_H * ML_DV)[:, :seq].reshape(b * seq, ML_H * ML_DV)
    y = _out_proj(x, a, h, P["odd_w_out"], j)
    caug = caug.reshape(b, ML_H, ML_DK, 2 * LANES)
    return y, k_new, v_new, caug[..., :ML_DV], caug[..., ML_DV], m_new.reshape(b, ML_H), conv_new


def _trunk(x, pos, P, ctx):
    b, seq, _ = x.shape
    m = b * seq
    x = x.reshape(m, D_MODEL).astype(F32)
    depth = P["norm_mix"].shape[0]
    tabs = _rope_tables(pos, max(_row_tile(m), seq))
    new = {k: [] for k in ("mla", "hgrn", "moba_k", "moba_v", "mlstm_c", "mlstm_n", "mlstm_m", "mlstm_conv")}
    for l in range(depth):
        x = _ffn(x, P["norm_ffn1"], P["ffn1_w_in"], P["ffn1_w_out"], l)
        if l % 2 == 0:
            x, rows, s = _even_layer(x, P, l // 2, l, b, seq, tabs, ctx)
            new["mla"].append(rows)
            new["hgrn"].append(s)
        else:
            x, k, v, c, n, mm, cv = _odd_layer(x, P, l // 2, l, b, seq, ctx)
            for name, val in zip(("moba_k", "moba_v", "mlstm_c", "mlstm_n", "mlstm_m", "mlstm_conv"),
                                 (k, v, c, n, mm, cv)):
                new[name].append(val)
        x = _ffn(x, P["norm_ffn2"], P["ffn2_w_in"], P["ffn2_w_out"], l)
    return x.reshape(b, seq, D_MODEL), {k: jnp.stack(v).astype(F32) for k, v in new.items()}


def kernel(x_prompt, x_sample, cache_mla, state_hgrn, cache_moba_k, cache_moba_v, state_mlstm_c, state_mlstm_n, state_mlstm_m, state_mlstm_conv, page_table, norm_ffn1, norm_mix, norm_ffn2, ffn1_w_in, ffn1_w_out, ffn2_w_in, ffn2_w_out, even_w_in, even_w_out, mla_q_norm, mla_w_uq, mla_kv_norm, mla_w_uk, mla_w_uv, mla_qn_norm, mla_qr_norm, mla_kn_norm, mla_kr_norm, hgrn_lb_raw, hgrn_out_norm, odd_w_in, odd_w_out, moba_q_norm, moba_k_norm, mlstm_conv_w, mlstm_conv_b, mlstm_b_i, mlstm_b_f, mlstm_out_norm):
    P = _prepare(norm_ffn1, norm_mix, norm_ffn2, ffn1_w_in, ffn1_w_out, ffn2_w_in, ffn2_w_out, even_w_in,
                 even_w_out, mla_q_norm, mla_w_uq, mla_kv_norm, mla_w_uk, mla_w_uv, mla_qn_norm, mla_qr_norm,
                 mla_kn_norm, mla_kr_norm, hgrn_lb_raw, hgrn_out_norm, odd_w_in, odd_w_out, moba_q_norm,
                 moba_k_norm, mlstm_conv_w, mlstm_conv_b, mlstm_b_i, mlstm_b_f, mlstm_out_norm)
    n_pool = cache_moba_k.shape[1]
    page_t = lambda c: jnp.transpose(c, (0, 1, 3, 4, 2)).reshape(-1, n_pool, MB_KVH * MB_DH, PAGE)
    ctx = dict(page_table=page_table.astype(I32), cache_mla_t=jnp.swapaxes(cache_mla, 2, 3), state_hgrn=state_hgrn,
               cache_moba_kt=page_t(cache_moba_k), cache_moba_vt=page_t(cache_moba_v), state_mlstm_c=state_mlstm_c,
               state_mlstm_n=state_mlstm_n, state_mlstm_m=state_mlstm_m, state_mlstm_conv=state_mlstm_conv)
    y_p, sp = _trunk(x_prompt, jnp.arange(x_prompt.shape[1]), P, None)
    past_len = page_table.shape[1] * PAGE
    y_s, ss = _trunk(x_sample, past_len + jnp.arange(x_sample.shape[1]), P, ctx)
    names = ("mla", "hgrn", "moba_k", "moba_v", "mlstm_c", "mlstm_n", "mlstm_m", "mlstm_conv")
    out = [y_p, y_s]
    for name in names:
        out += [sp[name], ss[name]]
    return tuple(out)
```

```python
import functools

import jax
import jax.numpy as jnp
from jax import lax
from jax.experimental import pallas as pl
from jax.experimental.pallas import tpu as pltpu

F32, BF16, I32 = jnp.float32, jnp.bfloat16, jnp.int32
HI = lax.Precision.HIGHEST
EPS = 1e-6
NEG = -1e30
LOG2E = 1.4426950408889634

D_MODEL = 1024
FFN_HIDDEN = 2048
PAGE = 128
MLA_H, MLA_QR, MLA_R, MLA_DN, MLA_DR, MLA_DV = 8, 256, 128, 64, 32, 64
ROPE_THETA = 10000.0
MLA_SCALE = (MLA_DN + MLA_DR) ** -0.5
HG_H, HG_DK, HG_DV = 4, 128, 128
HG_W = HG_H * HG_DK
MB_H, MB_KVH, MB_DH, MB_BLOCK, MB_TOPK = 8, 4, 64, 256, 3
MB_SCALE = MB_DH ** -0.5
ML_H, ML_DK, ML_DV, CONV_W = 4, 64, 128, 4
ML_QK = ML_H * ML_DK

LANES = 128
VMEM_LIMIT = 56 << 20


def _dot(a, b):
    return jnp.dot(a, b, preferred_element_type=F32)


def _dot_nt(a, b):
    return lax.dot_general(a, b, (((1,), (1,)), ((), ())), preferred_element_type=F32)


def _dot_tn(a, b):
    return lax.dot_general(a, b, (((0,), (0,)), ((), ())), preferred_element_type=F32)


def _rms(x, g):
    return x * lax.rsqrt(jnp.mean(x * x, axis=-1, keepdims=True) + EPS) * g


def _sigmoid(x):
    return 1.0 / (1.0 + jnp.exp(-x))


def _silu(x):
    return x * _sigmoid(x)


def _rows(tm, n):
    return pl.BlockSpec((tm, n), lambda i: (i, 0))


def _layer(shape, l):
    zeros = (0,) * len(shape)
    return pl.BlockSpec((None,) + tuple(shape), lambda *_: (l,) + zeros)


def _const(shape):
    zeros = (0,) * len(shape)
    return pl.BlockSpec(tuple(shape), lambda *_: zeros)


def _params(*sem):
    return pltpu.CompilerParams(dimension_semantics=sem, vmem_limit_bytes=VMEM_LIMIT)


def _row_tile(m, cap=256):
    return min(cap, m)


def _ffn_body(x_ref, g_ref, win_ref, wout_ref, o_ref, *, fc):
    x = x_ref[...]
    hn = _rms(x, g_ref[...]).astype(BF16)
    acc = jnp.zeros_like(x)
    for c in range(FFN_HIDDEN // fc):
        a = _dot(hn, win_ref[:, c * fc:(c + 1) * fc])
        b = _dot(hn, win_ref[:, FFN_HIDDEN + c * fc:FFN_HIDDEN + (c + 1) * fc])
        acc = acc + _dot((_silu(a) * b).astype(BF16), wout_ref[c * fc:(c + 1) * fc, :])
    o_ref[...] = x + 0.5 * acc


def _ffn(x, g, w_in, w_out, l):
    m = x.shape[0]
    tm = _row_tile(m, 512)
    return pl.pallas_call(
        functools.partial(_ffn_body, fc=512),
        grid=(m // tm,),
        in_specs=[_rows(tm, D_MODEL), _layer((1, D_MODEL), l),
                  _layer((D_MODEL, 2 * FFN_HIDDEN), l), _layer((FFN_HIDDEN, D_MODEL), l)],
        out_specs=_rows(tm, D_MODEL),
        out_shape=jax.ShapeDtypeStruct((m, D_MODEL), F32),
        compiler_params=_params("parallel"),
        name="ffn",
    )(x, g, w_in, w_out)


EV_COLS = 256 + 128 + 128 + 4 * HG_W


def _even_in_body(x_ref, g_ref, w_ref, qg_ref, wuq_ref, gq_ref, kvg_ref, wuk_ref, gk_ref, gkr_ref, bseg_ref,
                  cos_ref, sa_ref, sb_ref, llb_ref, l1m_ref, omlb_ref,
                  qcat_ref, kcat_ref, rows_ref, vlat_ref, hq_ref, hk_ref, hlf_ref, hi_ref, hg_ref):
    hn = _rms(x_ref[...], g_ref[...]).astype(BF16)
    cos_t, sin_a, sin_b = cos_ref[...], sa_ref[...], sb_ref[...]
    bseg = bseg_ref[...]

    def rope(t):
        return t * cos_t + pltpu.roll(t, 16, 1) * sin_a + pltpu.roll(t, LANES - 16, 1) * sin_b

    def segnorm(t, gain):
        ms = _dot((t * t).astype(BF16), bseg)
        return t * lax.rsqrt(ms + EPS) * gain

    ql = _dot(hn, w_ref[:, 0:256])
    qh = _dot(_rms(ql, qg_ref[...]).astype(BF16), wuq_ref[...])
    gq = gq_ref[...]
    for h in range(MLA_H):
        t = segnorm(qh[:, h * LANES:(h + 1) * LANES], gq)
        qcat_ref[:, h * LANES:(h + 1) * LANES] = (rope(t) * MLA_SCALE).astype(BF16)

    cn = _rms(_dot(hn, w_ref[:, 256:384]), kvg_ref[...])
    krl = _dot(hn, w_ref[:, 384:512])
    ms = jnp.sum(krl * krl, axis=-1, keepdims=True) * (1.0 / MLA_DR)
    krr = rope(krl * lax.rsqrt(ms + EPS) * gkr_ref[...])
    rows_ref[:, 0:MLA_R] = cn
    rows_ref[:, MLA_R:MLA_R + MLA_DR] = pltpu.roll(krr, 64, 1)[:, 0:MLA_DR]
    cnb = cn.astype(BF16)
    vlat_ref[...] = cnb
    kn = _dot(cnb, wuk_ref[...])
    gk = gk_ref[...]
    for h in range(MLA_H):
        t = segnorm(kn[:, h * LANES:(h + 1) * LANES], gk)
        kcat_ref[:, h * LANES:(h + 1) * LANES] = (t + krr).astype(BF16)

    o = 512
    hq_ref[...] = _silu(_dot(hn, w_ref[:, o:o + HG_W])).astype(hq_ref.dtype)
    hf = _dot(hn, w_ref[:, o + HG_W:o + 2 * HG_W])
    logsig = jnp.minimum(hf, 0.0) - jnp.log(1.0 + jnp.exp(-jnp.abs(hf)))
    aa, bb = llb_ref[...], l1m_ref[...] + logsig
    hlf_ref[...] = jnp.maximum(aa, bb) + jnp.log(1.0 + jnp.exp(-jnp.abs(aa - bb)))
    hk_ref[...] = omlb_ref[...] * _sigmoid(-hf)
    hi_ref[...] = _dot(hn, w_ref[:, o + 2 * HG_W:o + 3 * HG_W]).astype(hi_ref.dtype)
    hg_ref[...] = _silu(_dot(hn, w_ref[:, o + 3 * HG_W:o + 4 * HG_W])).astype(hg_ref.dtype)


def _even_in(x, P, e, l, tabs):
    m = x.shape[0]
    tm = _row_tile(m)
    cos_t, sin_a, sin_b = tabs
    nt = cos_t.shape[0] // tm
    tab = pl.BlockSpec((tm, LANES), lambda i: (i % nt, 0))
    outs = [(8 * LANES, BF16), (8 * LANES, BF16), (MLA_R + MLA_DR, F32), (MLA_R, BF16),
            (HG_W, BF16), (HG_W, F32), (HG_W, F32), (HG_W, BF16), (HG_W, BF16)]
    return pl.pallas_call(
        _even_in_body,
        grid=(m // tm,),
        in_specs=[_rows(tm, D_MODEL), _layer((1, D_MODEL), l), _layer((D_MODEL, EV_COLS), e),
                  _layer((1, MLA_QR), e), _layer((MLA_QR, 8 * LANES), e), _layer((1, LANES), e),
                  _layer((1, MLA_R), e), _layer((MLA_R, 8 * LANES), e), _layer((1, LANES), e),
                  _layer((1, LANES), e), _const((LANES, LANES)), tab, tab, tab,
                  _layer((1, HG_W), e), _layer((1, HG_W), e), _layer((1, HG_W), e)],
        out_specs=[_rows(tm, n) for n, _ in outs],
        out_shape=[jax.ShapeDtypeStruct((m, n), dt) for n, dt in outs],
        compiler_params=_params("parallel"),
        name="even_in",
    )(x, P["norm_mix"], P["even_w"], P["mla_q_norm"], P["mla_wuq"], P["mla_gq"], P["mla_kv_norm"],
      P["mla_wuk"], P["mla_gk"], P["mla_gkr"], P["bseg"], cos_t, sin_a, sin_b,
      P["hg_loglb"], P["hg_log1mlb"], P["hg_1mlb"])


def _mla_p_body(q_ref, k_ref, v_ref, wuv_ref, a_ref, m_scr, l_scr, acc_scr, *, tq):
    i = pl.program_id(1)
    row = lax.broadcasted_iota(I32, (tq, tq), 0)
    col = lax.broadcasted_iota(I32, (tq, tq), 1)
    causal = col <= row
    m_scr[...] = jnp.full(m_scr.shape, NEG, F32)
    l_scr[...] = jnp.zeros(l_scr.shape, F32)
    acc_scr[...] = jnp.zeros(acc_scr.shape, F32)

    def step(j, diag):
        start = pl.multiple_of(j * tq, tq)
        v = v_ref[pl.ds(start, tq), :]
        for h in range(MLA_H):
            q = q_ref[:, h * LANES:(h + 1) * LANES]
            k = k_ref[pl.ds(start, tq), h * LANES:(h + 1) * LANES]
            s = _dot_nt(q, k)
            if diag:
                s = jnp.where(causal, s, NEG)
            m_old = m_scr[h]
            m_new = jnp.maximum(m_old, jnp.max(s, axis=-1, keepdims=True))
            alpha = jnp.exp(m_old - m_new)
            p = jnp.exp(s - jnp.concatenate([m_new] * (tq // LANES), axis=-1))
            l_scr[h] = alpha * l_scr[h] + jnp.sum(p, axis=-1, keepdims=True)
            acc_scr[h] = alpha * acc_scr[h] + _dot(p.astype(BF16), v)
            m_scr[h] = m_new

    def body(j, c):
        step(j, False)
        return c

    lax.fori_loop(0, i, body, 0)
    step(i, True)
    for hp in range(MLA_H // 2):
        pair = [_dot((acc_scr[h] / l_scr[h]).astype(BF16), wuv_ref[h]) for h in (2 * hp, 2 * hp + 1)]
        a_ref[:, hp * LANES:(hp + 1) * LANES] = (pair[0] + pair[1]).astype(a_ref.dtype)


def _mla_prompt(qcat, kcat, vlat, wuv, e, b, l):
    tq = min(256, l)
    assert tq % LANES == 0 and l % tq == 0
    nq = l // tq
    stat = pltpu.VMEM((MLA_H, tq, LANES), F32)
    return pl.pallas_call(
        functools.partial(_mla_p_body, tq=tq),
        grid=(b, nq),
        in_specs=[pl.BlockSpec((tq, 8 * LANES), lambda bi, i: (bi * nq + i, 0)),
                  pl.BlockSpec((l, 8 * LANES), lambda bi, i: (bi, 0)),
                  pl.BlockSpec((l, MLA_R), lambda bi, i: (bi, 0)),
                  _layer((MLA_H, MLA_R, LANES), e)],
        out_specs=pl.BlockSpec((tq, MLA_H * MLA_DV), lambda bi, i: (bi * nq + i, 0)),
        out_shape=jax.ShapeDtypeStruct((b * l, MLA_H * MLA_DV), BF16),
        scratch_shapes=[stat, stat, stat],
        compiler_params=_params("parallel", "arbitrary"),
        name="mla_prompt",
    )(qcat, kcat, vlat, wuv)


MLA_PG = 16
MLA_SUB = 16


def _mla_s_body(pt_ref, qbd_ref, qr_ref, qself_ref, kself_ref, vself_ref, wukt_ref, wuv_ref, *rest, nsteps):
    pages, a_ref = rest[:MLA_PG], rest[MLA_PG]
    lhs_scr, m_scr, l_scr, acc_scr = rest[MLA_PG + 1:]
    j = pl.program_id(1)
    hd = MLA_H * MLA_DN

    @pl.when(j == 0)
    def _():
        wukt = wukt_ref[...]
        lhs_scr[0:hd, :] = wukt
        qa = _dot(qbd_ref[...].astype(BF16), wukt)
        lhs_scr[hd:hd + 16, :] = jnp.concatenate([qa, jnp.zeros_like(qa)], axis=0).astype(BF16)
        m_scr[...] = jnp.full(m_scr.shape, NEG, F32)
        l_scr[...] = jnp.zeros(l_scr.shape, F32)
        acc_scr[...] = jnp.zeros(acc_scr.shape, F32)

    lhs = lhs_scr[...]
    qrb = qr_ref[...].astype(BF16)

    def partial_softmax(pgs):
        ct = jnp.concatenate([p[0:MLA_R, :] for p in pgs], axis=1).astype(BF16)
        krt = jnp.concatenate([p[MLA_R:MLA_R + MLA_DR, :] for p in pgs], axis=1).astype(BF16)
        keys = ct.shape[1]
        both = _dot(lhs, ct)
        knt = both[0:hd, :]
        ss = jnp.sum((knt * knt).reshape(MLA_H, MLA_DN, keys), axis=1)
        rinv = lax.rsqrt(ss * (1.0 / MLA_DN) + EPS)
        s = rinv * both[hd:hd + MLA_H, :] + _dot(qrb, krt)
        m = jnp.max(s, axis=-1, keepdims=True)
        p = jnp.exp(s - m)
        return m, jnp.sum(p, axis=-1, keepdims=True), _dot_nt(p.astype(BF16), ct)

    n_sub = len(pages) // MLA_SUB
    parts = [partial_softmax(pages[i * MLA_SUB:(i + 1) * MLA_SUB]) for i in range(n_sub)]
    m_old = m_scr[...]
    m_new = m_old
    for m, _, _ in parts:
        m_new = jnp.maximum(m_new, m)
    alpha = jnp.exp(m_old - m_new)
    l_new = alpha * l_scr[...]
    acc_new = alpha * acc_scr[...]
    for m, l, acc in parts:
        w = jnp.exp(m - m_new)
        l_new = l_new + w * l
        acc_new = acc_new + w * acc
    l_scr[...] = l_new
    acc_scr[...] = acc_new
    m_scr[...] = m_new

    @pl.when(j == nsteps - 1)
    def _():
        s_self = jnp.sum(qself_ref[...].astype(F32) * kself_ref[...].astype(F32), axis=-1, keepdims=True)
        m_old = m_scr[...]
        m_new = jnp.maximum(m_old, s_self)
        alpha = jnp.exp(m_old - m_new)
        p_self = jnp.exp(s_self - m_new)
        l = alpha * l_scr[...] + p_self
        acc = alpha * acc_scr[...] + p_self * vself_ref[...].astype(F32)
        r = _dot((acc / l).astype(BF16), wuv_ref[...])
        rh = lax.broadcasted_iota(I32, r.shape, 0)
        ch = lax.broadcasted_iota(I32, r.shape, 1) // MLA_DV
        a_ref[...] = jnp.sum(jnp.where(rh == ch, r, 0.0), axis=0, keepdims=True).astype(a_ref.dtype)


def _mla_sample(page_table, cache_t, e, qbd, qr, qself, kself, vself, wukt, wuv):
    bs = qbd.shape[0]
    n_pages = page_table.shape[1]
    pg = MLA_PG
    assert n_pages % pg == 0
    nsteps = n_pages // pg
    width = MLA_R + MLA_DR
    hd = MLA_H * MLA_DN
    seq3 = lambda n2, n3: pl.BlockSpec((None, n2, n3), lambda b, j, pt: (b, 0, 0))

    def page_spec(p):
        return pl.BlockSpec((None, None, width, PAGE),
                            lambda b, j, pt: (e, pt[b * n_pages + j * pg + p], 0, 0))

    grid_spec = pltpu.PrefetchScalarGridSpec(
        num_scalar_prefetch=1,
        grid=(bs, nsteps),
        in_specs=[seq3(MLA_H, hd), seq3(MLA_H, MLA_DR), seq3(MLA_H, LANES), seq3(MLA_H, LANES),
                  seq3(1, MLA_R),
                  pl.BlockSpec((None, hd, MLA_R), lambda b, j, pt: (e, 0, 0)),
                  pl.BlockSpec((None, MLA_R, MLA_H * MLA_DV), lambda b, j, pt: (e, 0, 0))]
                 + [page_spec(p) for p in range(pg)],
        out_specs=pl.BlockSpec((None, 1, MLA_H * MLA_DV), lambda b, j, pt: (b, 0, 0)),
        scratch_shapes=[pltpu.VMEM((hd + 16, MLA_R), BF16), pltpu.VMEM((MLA_H, 1), F32),
                        pltpu.VMEM((MLA_H, 1), F32), pltpu.VMEM((MLA_H, MLA_R), F32)],
    )
    out = pl.pallas_call(
        functools.partial(_mla_s_body, nsteps=nsteps),
        grid_spec=grid_spec,
        out_shape=jax.ShapeDtypeStruct((bs, 1, MLA_H * MLA_DV), BF16),
        compiler_params=_params("parallel", "arbitrary"),
        name="mla_sample",
    )(page_table.reshape(-1), qbd, qr, qself, kself, vself, wukt, wuv, *([cache_t] * MLA_PG))
    return out.reshape(bs, MLA_H * MLA_DV)


HG_BS = 32
SCAN_GROUP = 2


def _hgrn_body(q_ref, k_ref, lf_ref, v_ref, g_ref, s0_ref, gain_ref, wind_ref, tri_ref,
               o_ref, sout_ref, st_scr, z_scr, *, c_len, bs, nc, grp):
    c = pl.program_id(1)
    nb = c_len // bs
    hc = HG_H * c_len
    pk = 16

    @pl.when(c == 0)
    def _():
        z_scr[...] = jnp.zeros(z_scr.shape, BF16)
        for gi in range(grp):
            for h in range(HG_H):
                st_scr[gi, h] = s0_ref[gi, h].T

    tri = tri_ref[...]
    gain = gain_ref[...]
    wind = wind_ref[...]
    row = lax.broadcasted_iota(I32, (c_len, c_len), 0)
    col = lax.broadcasted_iota(I32, (c_len, c_len), 1)
    keep = (col <= row) & ((row >= bs) == (col >= bs))
    late = lax.broadcasted_iota(I32, (c_len, HG_W), 0) >= bs
    for gi in range(grp):
        b = jnp.dot(tri, lf_ref[gi], precision=HI, preferred_element_type=F32)
        q = q_ref[gi].astype(F32)
        k = k_ref[gi]
        v = v_ref[gi]
        b_last = b[c_len - 1:c_len, :]
        qe = (q * jnp.exp(b)).astype(BF16)
        kd = (k * jnp.exp(b_last - b)).astype(BF16)
        if nb == 2:
            ref = b[bs - 1:bs, :]
            q_off = jnp.where(late, q * jnp.exp(jnp.minimum(b - ref, 0.0)), 0.0).astype(BF16)
            k_off = jnp.where(late, 0.0, k * jnp.exp(jnp.minimum(ref - b, 0.0))).astype(BF16)
        early = []
        for h in range(HG_H):
            hs = slice(h * LANES, (h + 1) * LANES)
            st = st_scr[gi, h]
            o_h = _dot_nt(qe[:, hs], st.astype(BF16))
            a_off = _dot_nt(q_off[:, hs], k_off[:, hs]) if nb == 2 else None
            st_scr[gi, h] = jnp.exp(b_last[:, hs]) * st + _dot_tn(v[:, hs], kd[:, hs])
            early.append((o_h, a_off))
        b3, q3, k3 = (t.reshape(nb, bs, HG_W) for t in (b * LOG2E, q, k))
        for s in range(bs):
            t0 = (s // pk) * pk
            z = q3[:, t0:, :] * k3[:, s:s + 1, :] * jnp.exp2(jnp.minimum(b3[:, t0:, :] - b3[:, s:s + 1, :], 0.0))
            z = z.astype(BF16)
            for ib in range(nb):
                for h in range(HG_H):
                    r0 = gi * hc + h * c_len + ib * bs
                    z_scr[r0 + t0:r0 + bs, s * LANES:(s + 1) * LANES] = z[ib, :, h * LANES:(h + 1) * LANES]
        g = g_ref[gi].astype(F32)
        for h in range(HG_H):
            hs = slice(h * LANES, (h + 1) * LANES)
            o_h, a_off = early[h]
            r0 = gi * hc + h * c_len
            a = jnp.where(keep, _dot(z_scr[r0:r0 + c_len, :], wind), 0.0)
            if nb == 2:
                a = a + a_off
            o_h = o_h + _dot(a.astype(BF16), v[:, hs])
            o_ref[gi, :, hs] = (_rms(o_h, gain[:, hs]) * g[:, hs]).astype(o_ref.dtype)

    @pl.when(c == nc - 1)
    def _():
        for gi in range(grp):
            for h in range(HG_H):
                sout_ref[gi, h] = st_scr[gi, h].T


def _hgrn_scan(hq, hk, hlf, hi, hg, s0, gain, b, l, c_len):
    bs = min(HG_BS, c_len)
    nc = l // c_len
    nb = c_len // bs
    assert nb in (1, 2) and c_len == nb * bs
    grp = SCAN_GROUP if b % SCAN_GROUP == 0 else 1
    chunk = lambda: pl.BlockSpec((grp, c_len, HG_W), lambda bi, ci: (bi, ci, 0))
    state = lambda: pl.BlockSpec((grp, HG_H, HG_DK, HG_DV), lambda bi, ci: (bi, 0, 0, 0))
    wind = jnp.tile(jnp.repeat(jnp.eye(bs, dtype=BF16), LANES, axis=0), (1, nb))
    tri = jnp.tril(jnp.ones((c_len, c_len), F32))
    seq = lambda t: t.reshape(b, l, HG_W)
    o, s_new = pl.pallas_call(
        functools.partial(_hgrn_body, c_len=c_len, bs=bs, nc=nc, grp=grp),
        grid=(b // grp, nc),
        in_specs=[chunk(), chunk(), chunk(), chunk(), chunk(), state(), _const((1, HG_W)),
                  _const((bs * LANES, c_len)), _const((c_len, c_len))],
        out_specs=[chunk(), state()],
        out_shape=[jax.ShapeDtypeStruct((b, l, HG_W), BF16),
                   jax.ShapeDtypeStruct((b, HG_H, HG_DK, HG_DV), F32)],
        scratch_shapes=[pltpu.VMEM((grp, HG_H, HG_DV, HG_DK), F32),
                        pltpu.VMEM((grp * HG_H * c_len, bs * LANES), BF16)],
        compiler_params=_params("parallel", "arbitrary"),
        name="hgrn_scan",
    )(seq(hq), seq(hk), seq(hlf), seq(hi), seq(hg), s0, gain, wind, tri)
    return o.reshape(b * l, HG_W), s_new


def _out_body(x_ref, a_ref, o_ref, w_ref, y_ref):
    half = a_ref.shape[1]
    y_ref[...] = x_ref[...] + _dot(a_ref[...], w_ref[0:half, :]) + _dot(o_ref[...], w_ref[half:2 * half, :])


def _out_proj(x, a, o, w, e):
    m = x.shape[0]
    tm = _row_tile(m, 512)
    half = a.shape[1]
    return pl.pallas_call(
        _out_body,
        grid=(m // tm,),
        in_specs=[_rows(tm, D_MODEL), _rows(tm, half), _rows(tm, half), _layer((2 * half, D_MODEL), e)],
        out_specs=_rows(tm, D_MODEL),
        out_shape=jax.ShapeDtypeStruct((m, D_MODEL), F32),
        compiler_params=_params("parallel"),
        name="out_proj",
    )(x, a, o, w)


OD_Q, OD_K, OD_V = 0, 8 * LANES, 12 * LANES
OD_XQK, OD_XV, OD_OG, OD_G = 16 * LANES, 20 * LANES, 24 * LANES, 28 * LANES
OD_COLS = 29 * LANES


def _odd_in_body(x_ref, g_ref, w_ref, gq_ref, gk_ref, gb_ref,
                 qpad_ref, kpad_ref, vpad_ref, xqk_ref, xv_ref, og_ref, gates_ref):
    hn = _rms(x_ref[...], g_ref[...]).astype(BF16)

    def headnorm(t, gain):
        ms = jnp.sum(t * t, axis=-1, keepdims=True) * (1.0 / MB_DH)
        return t * lax.rsqrt(ms + EPS) * gain

    gq, gk = gq_ref[...], gk_ref[...]
    for h in range(MB_H):
        t = _dot(hn, w_ref[:, OD_Q + h * LANES:OD_Q + (h + 1) * LANES])
        qpad_ref[:, h * LANES:(h + 1) * LANES] = (headnorm(t, gq) * MB_SCALE).astype(BF16)
    for h in range(MB_KVH):
        t = _dot(hn, w_ref[:, OD_K + h * LANES:OD_K + (h + 1) * LANES])
        kpad_ref[:, h * LANES:(h + 1) * LANES] = headnorm(t, gk)
    vpad_ref[...] = _dot(hn, w_ref[:, OD_V:OD_V + 4 * LANES])
    xqk_ref[...] = _dot(hn, w_ref[:, OD_XQK:OD_XQK + 4 * LANES])
    xv_ref[...] = _dot(hn, w_ref[:, OD_XV:OD_XV + 4 * LANES]).astype(xv_ref.dtype)
    og_ref[...] = _sigmoid(_dot(hn, w_ref[:, OD_OG:OD_OG + 4 * LANES])).astype(og_ref.dtype)
    gt = _dot(hn, w_ref[:, OD_G:OD_G + LANES]) + gb_ref[...]
    lane = lax.broadcasted_iota(I32, gt.shape, 1)
    logsig = jnp.minimum(gt, 0.0) - jnp.log(1.0 + jnp.exp(-jnp.abs(gt)))
    gates_ref[...] = jnp.where(lane < ML_H, gt, logsig)


def _odd_in(x, P, j, l):
    m = x.shape[0]
    tm = _row_tile(m)
    outs = [(8 * LANES, BF16), (4 * LANES, F32), (4 * LANES, F32), (2 * ML_QK, F32),
            (ML_H * ML_DV, BF16), (ML_H * ML_DV, BF16), (LANES, F32)]
    return pl.pallas_call(
        _odd_in_body,
        grid=(m // tm,),
        in_specs=[_rows(tm, D_MODEL), _layer((1, D_MODEL), l), _layer((D_MODEL, OD_COLS), j),
                  _layer((1, LANES), j), _layer((1, LANES), j), _layer((1, LANES), j)],
        out_specs=[_rows(tm, n) for n, _ in outs],
        out_shape=[jax.ShapeDtypeStruct((m, n), dt) for n, dt in outs],
        compiler_params=_params("parallel"),
        name="odd_in",
    )(x, P["norm_mix"], P["odd_w"], P["moba_gq"], P["moba_gk"], P["mlstm_gb"])


def _moba_p_body(q_ref, k_ref, v_ref, a_ref, kmean_scr, qa_scr, m_scr, l_scr, acc_scr, *, nb):
    i = pl.program_id(1)
    blk = MB_BLOCK
    row = lax.broadcasted_iota(I32, (blk, blk), 0)
    col = lax.broadcasted_iota(I32, (blk, blk), 1)
    causal = col <= row
    lane = lax.broadcasted_iota(I32, (blk, LANES), 1)

    @pl.when(i == 0)
    def _():
        kmean_scr[...] = jnp.mean(k_ref[...].reshape(nb, blk, MB_KVH * LANES), axis=1)

    kmean = kmean_scr[...]
    blk_id = lax.broadcasted_iota(I32, (nb, blk), 0)
    cand = jnp.where(lax.broadcasted_iota(I32, (nb, 1), 0) < i, 1.0, 0.0)
    place = jnp.where(lax.broadcasted_iota(I32, (nb, LANES), 1) == lax.broadcasted_iota(I32, (nb, LANES), 0) + MB_DH,
                      1.0, 0.0).astype(BF16)
    for h in range(MB_H):
        g = h // (MB_H // MB_KVH)
        q = q_ref[:, h * LANES:(h + 1) * LANES]
        gate = lax.dot_general(kmean[:, g * LANES:(g + 1) * LANES], q.astype(F32), (((1,), (1,)), ((), ())),
                               precision=HI, preferred_element_type=F32)
        rank = jnp.zeros((nb, blk), F32)
        for c in range(nb - 1):
            gc = gate[c:c + 1, :]
            better = (gc > gate) | ((gc == gate) & (c < blk_id))
            rank = rank + jnp.where(better, cand[c:c + 1, :], 0.0)
        bias = jnp.where(rank < MB_TOPK, cand, 0.0) * (-NEG) + NEG
        qa_scr[h] = (q.astype(F32) + _dot_tn(bias.astype(BF16), place)).astype(BF16)
    m_scr[...] = jnp.full(m_scr.shape, NEG, F32)
    l_scr[...] = jnp.zeros(l_scr.shape, F32)
    acc_scr[...] = jnp.zeros(acc_scr.shape, F32)

    def step(j, diag):
        start = pl.multiple_of(j * blk, blk)
        for g in range(MB_KVH):
            gs = slice(g * LANES, (g + 1) * LANES)
            kj = k_ref[pl.ds(start, blk), gs]
            if not diag:
                kj = jnp.where(lane == MB_DH + j, 1.0, kj)
            kj = kj.astype(BF16)
            vj = v_ref[pl.ds(start, blk), gs].astype(BF16)
            for h in range(g * (MB_H // MB_KVH), (g + 1) * (MB_H // MB_KVH)):
                s = _dot_nt(qa_scr[h], kj)
                if diag:
                    s = jnp.where(causal, s, NEG)
                m_old = m_scr[h]
                m_new = jnp.maximum(m_old, jnp.max(s, axis=-1, keepdims=True))
                alpha = jnp.exp(m_old - m_new)
                p = jnp.exp(s - jnp.concatenate([m_new] * (blk // LANES), axis=-1))
                l_scr[h] = alpha * l_scr[h] + jnp.sum(p, axis=-1, keepdims=True)
                acc_scr[h] = alpha * acc_scr[h] + _dot(p.astype(BF16), vj)
                m_scr[h] = m_new

    def body(j, c):
        step(j, False)
        return c

    step(i, True)
    lax.fori_loop(0, i, body, 0)
    for h in range(MB_H):
        a_ref[:, h * MB_DH:(h + 1) * MB_DH] = (acc_scr[h] / l_scr[h])[:, 0:MB_DH].astype(a_ref.dtype)


def _moba_prompt(qpad, kpad, vpad, b, l):
    assert l % MB_BLOCK == 0 and l // MB_BLOCK <= LANES - MB_DH
    nb = l // MB_BLOCK
    stat = pltpu.VMEM((MB_H, MB_BLOCK, LANES), F32)
    return pl.pallas_call(
        functools.partial(_moba_p_body, nb=nb),
        grid=(b, nb),
        in_specs=[pl.BlockSpec((MB_BLOCK, 8 * LANES), lambda bi, i: (bi * nb + i, 0)),
                  pl.BlockSpec((l, 4 * LANES), lambda bi, i: (bi, 0)),
                  pl.BlockSpec((l, 4 * LANES), lambda bi, i: (bi, 0))],
        out_specs=pl.BlockSpec((MB_BLOCK, MB_H * MB_DH), lambda bi, i: (bi * nb + i, 0)),
        out_shape=jax.ShapeDtypeStruct((b * l, MB_H * MB_DH), BF16),
        scratch_shapes=[pltpu.VMEM((nb, MB_KVH * LANES), F32), pltpu.VMEM((MB_H, MB_BLOCK, LANES), BF16),
                        stat, stat, stat],
        compiler_params=_params("parallel", "arbitrary"),
        name="moba_prompt",
    )(qpad, kpad, vpad)


MB_SEL_PG = 16


def _moba_sel_body(pt_ref, qsel_ref, *rest, nsteps, n_cand):
    pages, sel_ref, kmt_scr = rest[:MB_SEL_PG], rest[MB_SEL_PG], rest[MB_SEL_PG + 1]
    j = pl.program_id(1)
    ppb = MB_BLOCK // PAGE
    nblk = MB_SEL_PG // ppb
    lane = lax.broadcasted_iota(I32, kmt_scr.shape, 1)

    @pl.when(j == 0)
    def _():
        kmt_scr[...] = jnp.zeros(kmt_scr.shape, F32)

    kmt = kmt_scr[...]
    for ib in range(nblk):
        tot = pages[ib * ppb][...]
        for p in range(1, ppb):
            tot = tot + pages[ib * ppb + p][...]
        mean = jnp.sum(tot, axis=-1, keepdims=True) * (1.0 / MB_BLOCK)
        kmt = jnp.where(lane == j * nblk + ib, mean, kmt)
    kmt_scr[...] = kmt

    @pl.when(j == nsteps - 1)
    def _():
        gate = jnp.dot(qsel_ref[...], kmt_scr[...], precision=HI, preferred_element_type=F32)
        lane_g = lax.broadcasted_iota(I32, gate.shape, 1)
        rank = jnp.zeros(gate.shape, I32)
        for c in range(n_cand):
            gc = gate[:, c:c + 1]
            better = (gc > gate) | ((gc == gate) & (c < lane_g))
            rank = rank + jnp.where(better, 1, 0)
        out_lane = lax.broadcasted_iota(I32, sel_ref.shape, 1)
        out = jnp.zeros(sel_ref.shape, I32)
        for r in range(MB_TOPK):
            idx = jnp.sum(jnp.where((rank == r) & (lane_g < n_cand), lane_g, 0), axis=-1, keepdims=True)
            out = jnp.where(out_lane == r, idx, out)
        sel_ref[...] = out


def _moba_select(page_table, cache_kt, jl, qsel):
    bs = qsel.shape[0]
    n_pages = page_table.shape[1]
    assert n_pages % MB_SEL_PG == 0
    nsteps = n_pages // MB_SEL_PG
    n_cand = n_pages * PAGE // MB_BLOCK
    assert MB_TOPK <= n_cand <= LANES
    width = MB_KVH * MB_DH

    def page_spec(p):
        return pl.BlockSpec((None, None, width, PAGE),
                            lambda b, j, pt: (jl, pt[b * n_pages + j * MB_SEL_PG + p], 0, 0))

    grid_spec = pltpu.PrefetchScalarGridSpec(
        num_scalar_prefetch=1,
        grid=(bs, nsteps),
        in_specs=[pl.BlockSpec((None, MB_H, width), lambda b, j, pt: (b, 0, 0))]
                 + [page_spec(p) for p in range(MB_SEL_PG)],
        out_specs=pl.BlockSpec((None, MB_H, LANES), lambda b, j, pt: (b, 0, 0)),
        scratch_shapes=[pltpu.VMEM((width, LANES), F32)],
    )
    sel = pl.pallas_call(
        functools.partial(_moba_sel_body, nsteps=nsteps, n_cand=n_cand),
        grid_spec=grid_spec,
        out_shape=jax.ShapeDtypeStruct((bs, MB_H, LANES), I32),
        compiler_params=_params("parallel", "arbitrary"),
        name="moba_select",
    )(page_table.reshape(-1), qsel, *([cache_kt] * MB_SEL_PG))
    return sel[:, :, :MB_TOPK]


MB_GRP = MB_H // MB_KVH


def _moba_s_body(pt_ref, sel_ref, q_ref, kself_ref, vself_ref, *rest):
    n = MB_TOPK * (MB_BLOCK // PAGE)
    pages, o_ref = rest[:2 * MB_GRP * n], rest[2 * MB_GRP * n]
    kself, vself = kself_ref[...], vself_ref[...]
    for hh in range(MB_GRP):
        kpages = pages[hh * n:(hh + 1) * n]
        vpages = pages[(MB_GRP + hh) * n:(MB_GRP + hh + 1) * n]
        q1 = q_ref[hh]
        q = jnp.broadcast_to(q1, (8, MB_DH)).astype(BF16)
        scores = [_dot(q, kp[...].astype(BF16)) for kp in kpages]
        s_self = jnp.sum(q1 * kself, axis=-1, keepdims=True)
        m = s_self
        for s in scores:
            m = jnp.maximum(m, jnp.max(s, axis=-1, keepdims=True))
        p_self = jnp.exp(s_self - m)
        l = p_self
        acc = p_self * vself
        for s, vp in zip(scores, vpages):
            p = jnp.exp(s - m)
            l = l + jnp.sum(p, axis=-1, keepdims=True)
            acc = acc + _dot_nt(p.astype(BF16), vp[...].astype(BF16))
        o_ref[hh] = (acc / l)[0:1, :]


def _moba_sample(page_table, sel, cache_kt, cache_vt, jl, q, kself, vself):
    bs = q.shape[0]
    n_pages = page_table.shape[1]
    ppb = MB_BLOCK // PAGE
    kv = lambda: pl.BlockSpec((None, None, 1, MB_DH), lambda b, g, pt, sl: (b, g, 0, 0))
    qo = lambda: pl.BlockSpec((None, MB_GRP, 1, MB_DH), lambda b, g, pt, sl: (b, g, 0, 0))

    def page_spec(hh, r, p):
        return pl.BlockSpec(
            (None, None, MB_DH, PAGE),
            lambda b, g, pt, sl: (jl, pt[b * n_pages + sl[(b * MB_H + g * MB_GRP + hh) * MB_TOPK + r] * ppb + p], g, 0))

    specs = [page_spec(hh, r, p) for hh in range(MB_GRP) for r in range(MB_TOPK) for p in range(ppb)]
    grid_spec = pltpu.PrefetchScalarGridSpec(
        num_scalar_prefetch=2,
        grid=(bs, MB_KVH),
        in_specs=[qo(), kv(), kv()] + specs + specs,
        out_specs=qo(),
    )
    n = len(specs)
    return pl.pallas_call(
        _moba_s_body,
        grid_spec=grid_spec,
        out_shape=jax.ShapeDtypeStruct((bs, MB_H, 1, MB_DH), F32),
        compiler_params=_params("parallel", "arbitrary"),
        name="moba_sample",
    )(page_table.reshape(-1), sel.reshape(-1), q, kself, vself, *([cache_kt] * n), *([cache_vt] * n))


def _mlstm_body(xqk_ref, xv_ref, og_ref, gcol_ref, grow_ref, cw_ref, cb_ref, prev_ref, c0_ref, m0_ref,
                gain_ref, tri_ref, triu_ref,
                h_ref, cout_ref, mout_ref, convout_ref, full_scr, c_scr, m_scr, *, c_len, nc, l_last, grp):
    c = pl.program_id(1)
    pad = 8
    w = 2 * ML_QK

    @pl.when(c == 0)
    def _():
        full_scr[:, pad - (CONV_W - 1):pad, :] = prev_ref[...]
        c_scr[...] = c0_ref[...]
        m_scr[...] = m0_ref[...]

    row = lax.broadcasted_iota(I32, (c_len, c_len), 0)
    col = lax.broadcasted_iota(I32, (c_len, c_len), 1)
    causal = col <= row
    head_of_lane = lax.broadcasted_iota(I32, (1, ML_QK), 1) // ML_DK
    head_of_row = lax.broadcasted_iota(I32, (ML_QK, 1), 0) // ML_DK
    m_lane = lax.broadcasted_iota(I32, (1, ML_H), 1)
    one_col = jnp.where(lax.broadcasted_iota(I32, (c_len, LANES), 1) == 0, 1.0, 0.0).astype(BF16)
    gain = gain_ref[...]
    tri, triu = tri_ref[...], triu_ref[...]

    for gi in range(grp):
        full_scr[gi, pad:pad + c_len, :] = xqk_ref[gi]
        conv = cb_ref[...]
        for t in range(CONV_W):
            o = pad - (CONV_W - 1) + t
            conv = conv + full_scr[gi, o:o + c_len, :] * cw_ref[t:t + 1, :]

        @pl.when(c == nc - 1)
        def _():
            convout_ref[gi] = full_scr[gi, pad + l_last - (CONV_W - 1):pad + l_last, :]

        full_scr[gi, pad - (CONV_W - 1):pad, :] = full_scr[gi, pad + c_len - (CONV_W - 1):pad + c_len, :]

        qk = _silu(conv)
        qf = qk[:, 0:ML_QK]
        kf = qk[:, ML_QK:w] * (ML_DK ** -0.5)
        kb = kf.astype(BF16)
        gcol, grow = gcol_ref[gi], grow_ref[gi]
        f_col = jnp.dot(tri, gcol, precision=HI, preferred_element_type=F32)
        f_row = jnp.dot(grow, triu, precision=HI, preferred_element_type=F32)
        state = c_scr[gi]
        state_b = state.astype(BF16)
        m_all = m_scr[gi]
        xv = xv_ref[gi]
        og = og_ref[gi].astype(F32)
        decay_col = jnp.zeros((ML_QK, 1), F32)
        upd = jnp.zeros((ML_QK, 2 * LANES), F32)
        m_next = jnp.zeros((1, ML_H), F32)
        for h in range(ML_H):
            fc = f_col[:, ML_H + h:ML_H + h + 1]
            ic = gcol[:, h:h + 1]
            fr = f_row[ML_H + h:ML_H + h + 1, :]
            ir = grow[h:h + 1, :]
            m_prev = m_all[:, h:h + 1]
            dmat = jnp.where(causal, fc - fr + ir, NEG)
            inter = fc + m_prev
            mt = jnp.maximum(jnp.max(dmat, axis=-1, keepdims=True), inter)
            qh = jnp.where(head_of_lane == h, qf, 0.0).astype(BF16)
            wd = jnp.exp(dmat - mt) * _dot_nt(qh, kb)
            a = jnp.exp(inter - mt)
            vaug = jnp.concatenate([xv[:, h * ML_DV:(h + 1) * ML_DV], one_col], axis=-1)
            num = _dot(wd.astype(BF16), vaug) + a * _dot(qh, state_b)
            den = num[:, ML_DV:ML_DV + 1]
            hh = num[:, 0:ML_DV] / jnp.maximum(jnp.abs(den), jnp.exp(-mt))
            hs = slice(h * ML_DV, (h + 1) * ML_DV)
            h_ref[gi, :, hs] = (_rms(hh, gain[:, hs]) * og[:, hs]).astype(h_ref.dtype)
            m_new = mt[c_len - 1:c_len, :]
            f_last = fc[c_len - 1:c_len, :]
            gk = (jnp.where(head_of_lane == h, kf, 0.0) * jnp.exp(f_last - fc + ic - m_new)).astype(BF16)
            upd = upd + _dot_tn(gk, vaug)
            decay_col = decay_col + jnp.where(head_of_row == h, jnp.exp(f_last + m_prev - m_new), 0.0)
            m_next = m_next + jnp.where(m_lane == h, m_new, 0.0)
        c_scr[gi] = decay_col * state + upd
        m_scr[gi] = m_next

    @pl.when(c == nc - 1)
    def _():
        cout_ref[...] = c_scr[...]
        mout_ref[...] = m_scr[...]


def _mlstm_scan(xqk, xv, og, gcol, grow, cw, cb, prev, c0, m0, gain, j, b, l_pad, c_len, l_last):
    nc = l_pad // c_len
    w = 2 * ML_QK
    grp = SCAN_GROUP if b % SCAN_GROUP == 0 else 1
    chunk = lambda n: pl.BlockSpec((grp, c_len, n), lambda bi, ci: (bi, ci, 0))
    per_b = lambda n2, n3: pl.BlockSpec((grp, n2, n3), lambda bi, ci: (bi, 0, 0))
    tri = jnp.tril(jnp.ones((c_len, c_len), F32))
    seq = lambda t: t.reshape(b, l_pad, t.shape[-1])
    h, caug, m_new, conv_new = pl.pallas_call(
        functools.partial(_mlstm_body, c_len=c_len, nc=nc, l_last=l_last, grp=grp),
        grid=(b // grp, nc),
        in_specs=[chunk(w), chunk(ML_H * ML_DV), chunk(ML_H * ML_DV), chunk(LANES),
                  pl.BlockSpec((grp, None, 8, c_len), lambda bi, ci: (bi, ci, 0, 0)),
                  _layer((CONV_W, w), j), _layer((1, w), j), per_b(CONV_W - 1, w),
                  per_b(ML_QK, 2 * LANES), per_b(1, ML_H), _layer((1, ML_H * ML_DV), j),
                  _const((c_len, c_len)), _const((c_len, c_len))],
        out_specs=[chunk(ML_H * ML_DV), per_b(ML_QK, 2 * LANES), per_b(1, ML_H), per_b(CONV_W - 1, w)],
        out_shape=[jax.ShapeDtypeStruct((b, l_pad, ML_H * ML_DV), BF16),
                   jax.ShapeDtypeStruct((b, ML_QK, 2 * LANES), F32),
                   jax.ShapeDtypeStruct((b, 1, ML_H), F32),
                   jax.ShapeDtypeStruct((b, CONV_W - 1, w), F32)],
        scratch_shapes=[pltpu.VMEM((grp, c_len + 8, w), F32), pltpu.VMEM((grp, ML_QK, 2 * LANES), F32),
                        pltpu.VMEM((grp, 1, ML_H), F32)],
        compiler_params=_params("parallel", "arbitrary"),
        name="mlstm_scan",
    )(seq(xqk), seq(xv), seq(og), seq(gcol), grow, cw, cb, prev, c0, m0, gain, tri, tri.T)
    return h.reshape(b * l_pad, ML_H * ML_DV), caug, m_new, conv_new


def _prepare(norm_ffn1, norm_mix, norm_ffn2, ffn1_w_in, ffn1_w_out, ffn2_w_in, ffn2_w_out, even_w_in, even_w_out,
             mla_q_norm, mla_w_uq, mla_kv_norm, mla_w_uk, mla_w_uv, mla_qn_norm, mla_qr_norm, mla_kn_norm,
             mla_kr_norm, hgrn_lb_raw, hgrn_out_norm, odd_w_in, odd_w_out, moba_q_norm, moba_k_norm,
             mlstm_conv_w, mlstm_conv_b, mlstm_b_i, mlstm_b_f, mlstm_out_norm):
    ne, no = even_w_in.shape[0], odd_w_in.shape[0]
    row = lambda t: t[:, None, :].astype(F32)
    P = dict(norm_ffn1=row(norm_ffn1), norm_mix=row(norm_mix), norm_ffn2=row(norm_ffn2),
             ffn1_w_in=ffn1_w_in.astype(BF16), ffn1_w_out=ffn1_w_out.astype(BF16),
             ffn2_w_in=ffn2_w_in.astype(BF16), ffn2_w_out=ffn2_w_out.astype(BF16),
             even_w_out=even_w_out.astype(BF16), odd_w_out=odd_w_out.astype(BF16))
    w = even_w_in
    kr = jnp.zeros((ne, D_MODEL, LANES), F32).at[:, :, 64:96].set(w[:, :, 384:416])
    P["even_w"] = jnp.concatenate([w[:, :, 0:384], kr, w[:, :, 416:]], axis=-1).astype(BF16)
    pad_heads = lambda t, n: jnp.pad(t, ((0, 0), (0, 0), (0, 0), (0, LANES - n))).reshape(ne, t.shape[1], 8 * LANES)
    P["mla_wuq"] = pad_heads(mla_w_uq, MLA_DN + MLA_DR).astype(BF16)
    P["mla_wuk"] = pad_heads(mla_w_uk, MLA_DN).astype(BF16)
    P["mla_wukt"] = jnp.swapaxes(mla_w_uk.reshape(ne, MLA_R, MLA_H * MLA_DN), 1, 2).astype(BF16)
    P["mla_wuv_flat"] = mla_w_uv.reshape(ne, MLA_R, MLA_H * MLA_DV).astype(BF16)
    wuv = jnp.transpose(mla_w_uv, (0, 2, 1, 3))
    odd_head = (jnp.arange(MLA_H) % 2 == 1)[None, :, None, None]
    P["mla_wuv"] = jnp.where(odd_head, jnp.pad(wuv, ((0, 0),) * 3 + ((MLA_DV, 0),)),
                             jnp.pad(wuv, ((0, 0),) * 3 + ((0, MLA_DV),))).astype(BF16)
    z32 = jnp.zeros((ne, LANES - MLA_DN - MLA_DR), F32)
    P["mla_q_norm"], P["mla_kv_norm"] = row(mla_q_norm), row(mla_kv_norm)
    P["mla_gq"] = row(jnp.concatenate([mla_qn_norm, mla_qr_norm, z32], axis=-1))
    P["mla_gk"] = row(jnp.concatenate([mla_kn_norm, jnp.zeros((ne, LANES - MLA_DN), F32)], axis=-1))
    P["mla_gkr"] = row(jnp.concatenate([jnp.zeros((ne, MLA_DN), F32), mla_kr_norm, z32], axis=-1))
    P["mla_kn_gain"] = mla_kn_norm.astype(F32)
    seg = jnp.arange(LANES)
    seg_id = jnp.where(seg < MLA_DN, 0, jnp.where(seg < MLA_DN + MLA_DR, 1, 2 + seg))
    seg_len = jnp.where(seg < MLA_DN, MLA_DN, MLA_DR).astype(F32)
    P["bseg"] = jnp.where(seg_id[:, None] == seg_id[None, :], 1.0 / seg_len[None, :], 0.0).astype(BF16)
    p_lb = jax.nn.softmax(hgrn_lb_raw.astype(F32), axis=0)
    lb = jnp.maximum(jnp.cumsum(p_lb, axis=0) - p_lb[0], 0.0)
    P["hg_loglb"], P["hg_log1mlb"], P["hg_1mlb"] = row(jnp.log(lb)), row(jnp.log1p(-lb)), row(1.0 - lb)
    P["hg_gain"] = row(jnp.tile(hgrn_out_norm, (1, HG_H)))
    w = odd_w_in
    c0 = MB_H * MB_DH
    c1 = c0 + MB_KVH * MB_DH
    c2 = c1 + MB_KVH * MB_DH
    c3 = c2 + 2 * ML_QK
    c4 = c3 + ML_H * ML_DV
    c5 = c4 + 2 * ML_H
    pad_h = lambda t, nh: jnp.pad(t.reshape(no, D_MODEL, nh, MB_DH),
                                  ((0, 0), (0, 0), (0, 0), (0, LANES - MB_DH))).reshape(no, D_MODEL, nh * LANES)
    gates = jnp.pad(w[:, :, c4:c5], ((0, 0), (0, 0), (0, LANES - 2 * ML_H)))
    P["odd_w"] = jnp.concatenate([pad_h(w[:, :, 0:c0], MB_H), pad_h(w[:, :, c0:c1], MB_KVH),
                                  pad_h(w[:, :, c1:c2], MB_KVH), w[:, :, c2:c4], w[:, :, c5:], gates],
                                 axis=-1).astype(BF16)
    zpad = jnp.zeros((no, LANES - MB_DH), F32)
    P["moba_gq"] = row(jnp.concatenate([moba_q_norm, zpad], axis=-1))
    P["moba_gk"] = row(jnp.concatenate([moba_k_norm, zpad], axis=-1))
    P["mlstm_gb"] = row(jnp.concatenate([mlstm_b_i, mlstm_b_f, jnp.zeros((no, LANES - 2 * ML_H), F32)], axis=-1))
    P["mlstm_cw"] = mlstm_conv_w.astype(F32)
    P["mlstm_cb"] = row(mlstm_conv_b)
    P["mlstm_gain"] = row(jnp.tile(mlstm_out_norm, (1, ML_H)))
    return P


def _rope_tables(pos, n_rows):
    half = MLA_DR // 2
    freqs = ROPE_THETA ** (-jnp.arange(half, dtype=F32) / half)
    ang = pos.astype(F32)[:, None] * freqs[None, :]
    cos, sin = jnp.cos(ang), jnp.sin(ang)
    n = pos.shape[0]
    z = lambda k: jnp.zeros((n, k), F32)
    cos_t = jnp.concatenate([jnp.ones((n, MLA_DN), F32), cos, cos, z(LANES - MLA_DN - MLA_DR)], axis=-1)
    sin_a = jnp.concatenate([z(MLA_DN + half), sin, z(LANES - MLA_DN - MLA_DR)], axis=-1)
    sin_b = jnp.concatenate([z(MLA_DN), -sin, z(LANES - MLA_DN - half)], axis=-1)
    rep = n_rows // n
    return tuple(jnp.tile(t, (rep, 1)) for t in (cos_t, sin_a, sin_b))


def _pad_time(t, b, l, l_pad, value=0.0):
    if l_pad == l:
        return t
    n = t.shape[-1]
    fill = jnp.broadcast_to(jnp.asarray(value, t.dtype), (b, l_pad - l, n))
    return jnp.concatenate([t.reshape(b, l, n), fill], axis=1).reshape(b * l_pad, n)


def _even_layer(x, P, e, l, b, seq, tabs, ctx):
    qcat, kcat, rows, vlat, hq, hk, hlf, hi, hg = _even_in(x, P, e, l, tabs)
    if ctx is None:
        a = _mla_prompt(qcat, kcat, vlat, P["mla_wuv"], e, b, seq)
        s0 = jnp.zeros((b, HG_H, HG_DK, HG_DV), F32)
        c_len, l_pad = min(64, seq), seq
    else:
        q3 = qcat.reshape(b, MLA_H, LANES).astype(F32)
        eye = jnp.eye(MLA_H, dtype=F32)
        qbd = jnp.einsum("bhd,hg->bhgd", q3[:, :, :MLA_DN] * P["mla_kn_gain"][e][None, None, :], eye)
        a = _mla_sample(ctx["page_table"], ctx["cache_mla_t"], e, qbd.reshape(b, MLA_H, MLA_H * MLA_DN),
                        q3[:, :, MLA_DN:MLA_DN + MLA_DR], qcat.reshape(b, MLA_H, LANES),
                        kcat.reshape(b, MLA_H, LANES), vlat.reshape(b, 1, MLA_R),
                        P["mla_wukt"], P["mla_wuv_flat"])
        s0 = ctx["state_hgrn"][e].astype(F32)
        c_len = l_pad = 16
    pad = lambda t: _pad_time(t, b, seq, l_pad)
    o, s_new = _hgrn_scan(pad(hq), pad(hk), pad(hlf), pad(hi), pad(hg), s0, P["hg_gain"][e], b, l_pad, c_len)
    if l_pad != seq:
        o = o.reshape(b, l_pad, HG_W)[:, :seq].reshape(b * seq, HG_W)
    y = _out_proj(x, a, o, P["even_w_out"], e)
    return y, rows.reshape(b, seq, MLA_R + MLA_DR), s_new


def _odd_layer(x, P, j, l, b, seq, ctx):
    qpad, kpad, vpad, xqk, xv, og, gates = _odd_in(x, P, j, l)
    unpad = lambda t: t.reshape(b, seq, MB_KVH, LANES)[..., :MB_DH]
    k_new, v_new = unpad(kpad), unpad(vpad)
    if ctx is None:
        a = _moba_prompt(qpad, kpad, vpad, b, seq)
        prev = jnp.zeros((b, CONV_W - 1, 2 * ML_QK), F32)
        caug0 = jnp.zeros((b, ML_QK, 2 * LANES), F32)
        m0 = jnp.zeros((b, 1, ML_H), F32)
        c_len, l_pad = min(64, seq), seq
    else:
        assert seq == 1
        pt = ctx["page_table"]
        qh = qpad.reshape(b, MB_H, LANES)[:, :, :MB_DH].astype(F32)
        on_kv = jax.nn.one_hot(jnp.arange(MB_H) // MB_GRP, MB_KVH, dtype=F32)
        qsel = jnp.einsum("bhd,hg->bhgd", qh, on_kv).reshape(b, MB_H, MB_KVH * MB_DH)
        sel = _moba_select(pt, ctx["cache_moba_kt"], j, qsel)
        a = _moba_sample(pt, sel, ctx["cache_moba_kt"], ctx["cache_moba_vt"], j, qh[:, :, None, :],
                         k_new.reshape(b, MB_KVH, 1, MB_DH), v_new.reshape(b, MB_KVH, 1, MB_DH))
        a = a.reshape(b, MB_H * MB_DH).astype(BF16)
        prev = ctx["state_mlstm_conv"][j].astype(F32)
        caug0 = jnp.concatenate([ctx["state_mlstm_c"][j].astype(F32),
                                 ctx["state_mlstm_n"][j].astype(F32)[..., None],
                                 jnp.zeros((b, ML_H, ML_DK, LANES - 1), F32)], axis=-1).reshape(b, ML_QK, 2 * LANES)
        m0 = ctx["state_mlstm_m"][j].astype(F32).reshape(b, 1, ML_H)
        c_len = l_pad = 16
    nc = l_pad // c_len
    gate_fill = jnp.where(jnp.arange(LANES) < ML_H, NEG, 0.0)
    gcol = _pad_time(gates, b, seq, l_pad, gate_fill)
    grow = jnp.swapaxes(gcol[:, :8].reshape(b, nc, c_len, 8), 2, 3)
    pad = lambda t: _pad_time(t, b, seq, l_pad)
    l_last = seq - (nc - 1) * c_len
    h, caug, m_new, conv_new = _mlstm_scan(pad(xqk), pad(xv), pad(og), gcol, grow, P["mlstm_cw"], P["mlstm_cb"],
                                           prev, caug0, m0, P["mlstm_gain"], j, b, l_pad, c_len, l_last)
    if l_pad != seq:
        h = h.reshape(b, l_pad, ML_H * ML_DV)[:, :seq].reshape(b * seq, ML_H * ML_DV)
    y = _out_proj(x, a, h, P["odd_w_out"], j)
    caug = caug.reshape(b, ML_H, ML_DK, 2 * LANES)
    return y, k_new, v_new, caug[..., :ML_DV], caug[..., ML_DV], m_new.reshape(b, ML_H), conv_new


def _trunk(x, pos, P, ctx):
    b, seq, _ = x.shape
    m = b * seq
    x = x.reshape(m, D_MODEL).astype(F32)
    depth = P["norm_mix"].shape[0]
    tabs = _rope_tables(pos, max(_row_tile(m), seq))
    new = {k: [] for k in ("mla", "hgrn", "moba_k", "moba_v", "mlstm_c", "mlstm_n", "mlstm_m", "mlstm_conv")}
    for l in range(depth):
        x = _ffn(x, P["norm_ffn1"], P["ffn1_w_in"], P["ffn1_w_out"], l)
        if l % 2 == 0:
            x, rows, s = _even_layer(x, P, l // 2, l, b, seq, tabs, ctx)
            new["mla"].append(rows)
            new["hgrn"].append(s)
        else:
            x, k, v, c, n, mm, cv = _odd_layer(x, P, l // 2, l, b, seq, ctx)
            for name, val in zip(("moba_k", "moba_v", "mlstm_c", "mlstm_n", "mlstm_m", "mlstm_conv"),
                                 (k, v, c, n, mm, cv)):
                new[name].append(val)
        x = _ffn(x, P["norm_ffn2"], P["ffn2_w_in"], P["ffn2_w_out"], l)
    return x.reshape(b, seq, D_MODEL), {k: jnp.stack(v).astype(F32) for k, v in new.items()}


def kernel(x_prompt, x_sample, cache_mla, state_hgrn, cache_moba_k, cache_moba_v, state_mlstm_c, state_mlstm_n, state_mlstm_m, state_mlstm_conv, page_table, norm_ffn1, norm_mix, norm_ffn2, ffn1_w_in, ffn1_w_out, ffn2_w_in, ffn2_w_out, even_w_in, even_w_out, mla_q_norm, mla_w_uq, mla_kv_norm, mla_w_uk, mla_w_uv, mla_qn_norm, mla_qr_norm, mla_kn_norm, mla_kr_norm, hgrn_lb_raw, hgrn_out_norm, odd_w_in, odd_w_out, moba_q_norm, moba_k_norm, mlstm_conv_w, mlstm_conv_b, mlstm_b_i, mlstm_b_f, mlstm_out_norm):
    P = _prepare(norm_ffn1, norm_mix, norm_ffn2, ffn1_w_in, ffn1_w_out, ffn2_w_in, ffn2_w_out, even_w_in,
                 even_w_out, mla_q_norm, mla_w_uq, mla_kv_norm, mla_w_uk, mla_w_uv, mla_qn_norm, mla_qr_norm,
                 mla_kn_norm, mla_kr_norm, hgrn_lb_raw, hgrn_out_norm, odd_w_in, odd_w_out, moba_q_norm,
                 moba_k_norm, mlstm_conv_w, mlstm_conv_b, mlstm_b_i, mlstm_b_f, mlstm_out_norm)
    n_pool = cache_moba_k.shape[1]
    page_t = lambda c: jnp.transpose(c, (0, 1, 3, 4, 2)).reshape(-1, n_pool, MB_KVH * MB_DH, PAGE)
    ctx = dict(page_table=page_table.astype(I32), cache_mla_t=jnp.swapaxes(cache_mla, 2, 3), state_hgrn=state_hgrn,
               cache_moba_kt=page_t(cache_moba_k), cache_moba_vt=page_t(cache_moba_v), state_mlstm_c=state_mlstm_c,
               state_mlstm_n=state_mlstm_n, state_mlstm_m=state_mlstm_m, state_mlstm_conv=state_mlstm_conv)
    y_p, sp = _trunk(x_prompt, jnp.arange(x_prompt.shape[1]), P, None)
    past_len = page_table.shape[1] * PAGE
    y_s, ss = _trunk(x_sample, past_len + jnp.arange(x_sample.shape[1]), P, ctx)
    names = ("mla", "hgrn", "moba_k", "moba_v", "mlstm_c", "mlstm_n", "mlstm_m", "mlstm_conv")
    out = [y_p, y_s]
    for name in names:
        out += [sp[name], ss[name]]
    return tuple(out)
```

```python
import functools

import jax
import jax.numpy as jnp
from jax import lax
from jax.experimental import pallas as pl
from jax.experimental.pallas import tpu as pltpu

F32, BF16, I32 = jnp.float32, jnp.bfloat16, jnp.int32
HI = lax.Precision.HIGHEST
EPS = 1e-6
NEG = -1e30
LOG2E = 1.4426950408889634

D_MODEL = 1024
FFN_HIDDEN = 2048
PAGE = 128
MLA_H, MLA_QR, MLA_R, MLA_DN, MLA_DR, MLA_DV = 8, 256, 128, 64, 32, 64
ROPE_THETA = 10000.0
MLA_SCALE = (MLA_DN + MLA_DR) ** -0.5
HG_H, HG_DK, HG_DV = 4, 128, 128
HG_W = HG_H * HG_DK
MB_H, MB_KVH, MB_DH, MB_BLOCK, MB_TOPK = 8, 4, 64, 256, 3
MB_SCALE = MB_DH ** -0.5
ML_H, ML_DK, ML_DV, CONV_W = 4, 64, 128, 4
ML_QK = ML_H * ML_DK

LANES = 128
VMEM_LIMIT = 56 << 20


def _dot(a, b):
    return jnp.dot(a, b, preferred_element_type=F32)


def _dot_nt(a, b):
    return lax.dot_general(a, b, (((1,), (1,)), ((), ())), preferred_element_type=F32)


def _dot_tn(a, b):
    return lax.dot_general(a, b, (((0,), (0,)), ((), ())), preferred_element_type=F32)


def _rms(x, g):
    return x * lax.rsqrt(jnp.mean(x * x, axis=-1, keepdims=True) + EPS) * g


def _sigmoid(x):
    return 1.0 / (1.0 + jnp.exp(-x))


def _silu(x):
    return x * _sigmoid(x)


def _rows(tm, n):
    return pl.BlockSpec((tm, n), lambda i: (i, 0))


def _layer(shape, l):
    zeros = (0,) * len(shape)
    return pl.BlockSpec((None,) + tuple(shape), lambda *_: (l,) + zeros)


def _const(shape):
    zeros = (0,) * len(shape)
    return pl.BlockSpec(tuple(shape), lambda *_: zeros)


def _params(*sem):
    return pltpu.CompilerParams(dimension_semantics=sem, vmem_limit_bytes=VMEM_LIMIT)


def _row_tile(m, cap=256):
    return min(cap, m)


def _ffn_body(x_ref, g_ref, win_ref, wout_ref, o_ref, *, fc):
    x = x_ref[...]
    hn = _rms(x, g_ref[...]).astype(BF16)
    acc = jnp.zeros_like(x)
    for c in range(FFN_HIDDEN // fc):
        a = _dot(hn, win_ref[:, c * fc:(c + 1) * fc])
        b = _dot(hn, win_ref[:, FFN_HIDDEN + c * fc:FFN_HIDDEN + (c + 1) * fc])
        acc = acc + _dot((_silu(a) * b).astype(BF16), wout_ref[c * fc:(c + 1) * fc, :])
    o_ref[...] = x + 0.5 * acc


def _ffn(x, g, w_in, w_out, l):
    m = x.shape[0]
    tm = _row_tile(m, 512)
    return pl.pallas_call(
        functools.partial(_ffn_body, fc=512),
        grid=(m // tm,),
        in_specs=[_rows(tm, D_MODEL), _layer((1, D_MODEL), l),
                  _layer((D_MODEL, 2 * FFN_HIDDEN), l), _layer((FFN_HIDDEN, D_MODEL), l)],
        out_specs=_rows(tm, D_MODEL),
        out_shape=jax.ShapeDtypeStruct((m, D_MODEL), F32),
        compiler_params=_params("parallel"),
        name="ffn",
    )(x, g, w_in, w_out)


EV_COLS = 256 + 128 + 128 + 4 * HG_W


def _even_in_body(x_ref, g_ref, w_ref, qg_ref, wuq_ref, gq_ref, kvg_ref, wuk_ref, gk_ref, gkr_ref, bseg_ref,
                  cos_ref, sa_ref, sb_ref, llb_ref, l1m_ref, omlb_ref,
                  qcat_ref, kcat_ref, rows_ref, vlat_ref, hq_ref, hk_ref, hlf_ref, hi_ref, hg_ref):
    hn = _rms(x_ref[...], g_ref[...]).astype(BF16)
    cos_t, sin_a, sin_b = cos_ref[...], sa_ref[...], sb_ref[...]
    bseg = bseg_ref[...]

    def rope(t):
        return t * cos_t + pltpu.roll(t, 16, 1) * sin_a + pltpu.roll(t, LANES - 16, 1) * sin_b

    def segnorm(t, gain):
        ms = _dot((t * t).astype(BF16), bseg)
        return t * lax.rsqrt(ms + EPS) * gain

    ql = _dot(hn, w_ref[:, 0:256])
    qh = _dot(_rms(ql, qg_ref[...]).astype(BF16), wuq_ref[...])
    gq = gq_ref[...]
    for h in range(MLA_H):
        t = segnorm(qh[:, h * LANES:(h + 1) * LANES], gq)
        qcat_ref[:, h * LANES:(h + 1) * LANES] = (rope(t) * MLA_SCALE).astype(BF16)

    cn = _rms(_dot(hn, w_ref[:, 256:384]), kvg_ref[...])
    krl = _dot(hn, w_ref[:, 384:512])
    ms = jnp.sum(krl * krl, axis=-1, keepdims=True) * (1.0 / MLA_DR)
    krr = rope(krl * lax.rsqrt(ms + EPS) * gkr_ref[...])
    rows_ref[:, 0:MLA_R] = cn
    rows_ref[:, MLA_R:MLA_R + MLA_DR] = pltpu.roll(krr, 64, 1)[:, 0:MLA_DR]
    cnb = cn.astype(BF16)
    vlat_ref[...] = cnb
    kn = _dot(cnb, wuk_ref[...])
    gk = gk_ref[...]
    for h in range(MLA_H):
        t = segnorm(kn[:, h * LANES:(h + 1) * LANES], gk)
        kcat_ref[:, h * LANES:(h + 1) * LANES] = (t + krr).astype(BF16)

    o = 512
    hq_ref[...] = _silu(_dot(hn, w_ref[:, o:o + HG_W])).astype(hq_ref.dtype)
    hf = _dot(hn, w_ref[:, o + HG_W:o + 2 * HG_W])
    logsig = jnp.minimum(hf, 0.0) - jnp.log(1.0 + jnp.exp(-jnp.abs(hf)))
    aa, bb = llb_ref[...], l1m_ref[...] + logsig
    hlf_ref[...] = jnp.maximum(aa, bb) + jnp.log(1.0 + jnp.exp(-jnp.abs(aa - bb)))
    hk_ref[...] = omlb_ref[...] * _sigmoid(-hf)
    hi_ref[...] = _dot(hn, w_ref[:, o + 2 * HG_W:o + 3 * HG_W]).astype(hi_ref.dtype)
    hg_ref[...] = _silu(_dot(hn, w_ref[:, o + 3 * HG_W:o + 4 * HG_W])).astype(hg_ref.dtype)


def _even_in(x, P, e, l, tabs):
    m = x.shape[0]
    tm = _row_tile(m)
    cos_t, sin_a, sin_b = tabs
    nt = cos_t.shape[0] // tm
    tab = pl.BlockSpec((tm, LANES), lambda i: (i % nt, 0))
    outs = [(8 * LANES, BF16), (8 * LANES, BF16), (MLA_R + MLA_DR, F32), (MLA_R, BF16),
            (HG_W, BF16), (HG_W, F32), (HG_W, F32), (HG_W, BF16), (HG_W, BF16)]
    return pl.pallas_call(
        _even_in_body,
        grid=(m // tm,),
        in_specs=[_rows(tm, D_MODEL), _layer((1, D_MODEL), l), _layer((D_MODEL, EV_COLS), e),
                  _layer((1, MLA_QR), e), _layer((MLA_QR, 8 * LANES), e), _layer((1, LANES), e),
                  _layer((1, MLA_R), e), _layer((MLA_R, 8 * LANES), e), _layer((1, LANES), e),
                  _layer((1, LANES), e), _const((LANES, LANES)), tab, tab, tab,
                  _layer((1, HG_W), e), _layer((1, HG_W), e), _layer((1, HG_W), e)],
        out_specs=[_rows(tm, n) for n, _ in outs],
        out_shape=[jax.ShapeDtypeStruct((m, n), dt) for n, dt in outs],
        compiler_params=_params("parallel"),
        name="even_in",
    )(x, P["norm_mix"], P["even_w"], P["mla_q_norm"], P["mla_wuq"], P["mla_gq"], P["mla_kv_norm"],
      P["mla_wuk"], P["mla_gk"], P["mla_gkr"], P["bseg"], cos_t, sin_a, sin_b,
      P["hg_loglb"], P["hg_log1mlb"], P["hg_1mlb"])


def _mla_p_body(q_ref, k_ref, v_ref, wuv_ref, a_ref, m_scr, l_scr, acc_scr, *, tq):
    i = pl.program_id(1)
    row = lax.broadcasted_iota(I32, (tq, tq), 0)
    col = lax.broadcasted_iota(I32, (tq, tq), 1)
    causal = col <= row

    def step(j, diag):
        start = pl.multiple_of(j * tq, tq)
        v = v_ref[pl.ds(start, tq), :]
        for h in range(MLA_H):
            q = q_ref[:, h * LANES:(h + 1) * LANES]
            k = k_ref[pl.ds(start, tq), h * LANES:(h + 1) * LANES]
            s = _dot_nt(q, k)
            if diag:
                s = jnp.where(causal, s, NEG)
                m_new = jnp.broadcast_to(jnp.max(s, axis=-1, keepdims=True), (tq, LANES))
            else:
                m_old = m_scr[h]
                m_new = jnp.maximum(m_old, jnp.max(s, axis=-1, keepdims=True))
                alpha = jnp.exp(m_old - m_new)
            p = jnp.exp(s - jnp.concatenate([m_new] * (tq // LANES), axis=-1))
            l_new = jnp.broadcast_to(jnp.sum(p, axis=-1, keepdims=True), (tq, LANES))
            acc_new = _dot(p.astype(BF16), v)
            l_scr[h] = l_new if diag else alpha * l_scr[h] + l_new
            acc_scr[h] = acc_new if diag else alpha * acc_scr[h] + acc_new
            m_scr[h] = m_new

    def body(j, c):
        step(j, False)
        return c

    step(i, True)
    lax.fori_loop(0, i, body, 0)
    for hp in range(MLA_H // 2):
        pair = [_dot((acc_scr[h] / l_scr[h]).astype(BF16), wuv_ref[h]) for h in (2 * hp, 2 * hp + 1)]
        a_ref[:, hp * LANES:(hp + 1) * LANES] = (pair[0] + pair[1]).astype(a_ref.dtype)


def _mla_prompt(qcat, kcat, vlat, wuv, e, b, l):
    tq = min(256, l)
    assert tq % LANES == 0 and l % tq == 0
    nq = l // tq
    stat = pltpu.VMEM((MLA_H, tq, LANES), F32)
    return pl.pallas_call(
        functools.partial(_mla_p_body, tq=tq),
        grid=(b, nq),
        in_specs=[pl.BlockSpec((tq, 8 * LANES), lambda bi, i: (bi * nq + i, 0)),
                  pl.BlockSpec((l, 8 * LANES), lambda bi, i: (bi, 0)),
                  pl.BlockSpec((l, MLA_R), lambda bi, i: (bi, 0)),
                  _layer((MLA_H, MLA_R, LANES), e)],
        out_specs=pl.BlockSpec((tq, MLA_H * MLA_DV), lambda bi, i: (bi * nq + i, 0)),
        out_shape=jax.ShapeDtypeStruct((b * l, MLA_H * MLA_DV), BF16),
        scratch_shapes=[stat, stat, stat],
        compiler_params=_params("parallel", "arbitrary"),
        name="mla_prompt",
    )(qcat, kcat, vlat, wuv)


MLA_PG = 16


def _start_all(copies):
    for cp in copies:
        cp.start()


def _wait_all(copies):
    for cp in copies:
        cp.wait()


def _mla_s_body(pt_ref, qbd_ref, qr_ref, qself_ref, kself_ref, vself_ref, wukt_ref, wuv_ref, cache_hbm,
                a_ref, buf, lhs_scr, sem, *, n_pages, e):
    b = pl.program_id(0)
    slot = lax.rem(b, 2)
    hd = MLA_H * MLA_DN

    def page_copies(seq, slot_):
        return [pltpu.make_async_copy(cache_hbm.at[e, pt_ref[seq * n_pages + p]], buf.at[slot_, p], sem.at[slot_])
                for p in range(n_pages)]

    @pl.when(b == 0)
    def _():
        _start_all(page_copies(0, 0))

    @pl.when(b + 1 < pl.num_programs(0))
    def _():
        _start_all(page_copies(b + 1, 1 - slot))

    wukt = wukt_ref[...]
    lhs_scr[0:hd, :] = wukt
    qa = _dot(qbd_ref[...].astype(BF16), wukt)
    lhs_scr[hd:hd + 16, :] = jnp.concatenate([qa, jnp.zeros_like(qa)], axis=0).astype(BF16)
    lhs = lhs_scr[...]
    qrb = qr_ref[...].astype(BF16)
    _wait_all(page_copies(b, slot))

    def partial_softmax(g):
        pgs = [buf.at[slot, g * MLA_PG + p] for p in range(MLA_PG)]
        ct = jnp.concatenate([p[0:MLA_R, :] for p in pgs], axis=1).astype(BF16)
        krt = jnp.concatenate([p[MLA_R:MLA_R + MLA_DR, :] for p in pgs], axis=1).astype(BF16)
        keys = ct.shape[1]
        both = _dot(lhs, ct)
        knt = both[0:hd, :]
        ss = jnp.sum((knt * knt).reshape(MLA_H, MLA_DN, keys), axis=1)
        rinv = lax.rsqrt(ss * (1.0 / MLA_DN) + EPS)
        s = rinv * both[hd:hd + MLA_H, :] + _dot(qrb, krt)
        m = jnp.max(s, axis=-1, keepdims=True)
        p = jnp.exp(s - m)
        return m, jnp.sum(p, axis=-1, keepdims=True), _dot_nt(p.astype(BF16), ct)

    parts = [partial_softmax(g) for g in range(n_pages // MLA_PG)]
    s_self = jnp.sum(qself_ref[...].astype(F32) * kself_ref[...].astype(F32), axis=-1, keepdims=True)
    m_new = s_self
    for m, _, _ in parts:
        m_new = jnp.maximum(m_new, m)
    p_self = jnp.exp(s_self - m_new)
    l = p_self
    acc = p_self * vself_ref[...].astype(F32)
    for m, l_g, acc_g in parts:
        w = jnp.exp(m - m_new)
        l = l + w * l_g
        acc = acc + w * acc_g
    r = _dot((acc / l).astype(BF16), wuv_ref[...])
    rh = lax.broadcasted_iota(I32, r.shape, 0)
    ch = lax.broadcasted_iota(I32, r.shape, 1) // MLA_DV
    a_ref[...] = jnp.sum(jnp.where(rh == ch, r, 0.0), axis=0, keepdims=True).astype(a_ref.dtype)


def _mla_sample(page_table, cache_t, e, qbd, qr, qself, kself, vself, wukt, wuv):
    bs = qbd.shape[0]
    n_pages = page_table.shape[1]
    assert n_pages % MLA_PG == 0
    width = MLA_R + MLA_DR
    hd = MLA_H * MLA_DN
    seq3 = lambda n2, n3: pl.BlockSpec((None, n2, n3), lambda b, pt: (b, 0, 0))
    grid_spec = pltpu.PrefetchScalarGridSpec(
        num_scalar_prefetch=1,
        grid=(bs,),
        in_specs=[seq3(MLA_H, hd), seq3(MLA_H, MLA_DR), seq3(MLA_H, LANES), seq3(MLA_H, LANES),
                  seq3(1, MLA_R),
                  pl.BlockSpec((None, hd, MLA_R), lambda b, pt: (e, 0, 0)),
                  pl.BlockSpec((None, MLA_R, MLA_H * MLA_DV), lambda b, pt: (e, 0, 0)),
                  pl.BlockSpec(memory_space=pl.ANY)],
        out_specs=pl.BlockSpec((None, 1, MLA_H * MLA_DV), lambda b, pt: (b, 0, 0)),
        scratch_shapes=[pltpu.VMEM((2, n_pages, width, PAGE), F32), pltpu.VMEM((hd + 16, MLA_R), BF16),
                        pltpu.SemaphoreType.DMA((2,))],
    )
    out = pl.pallas_call(
        functools.partial(_mla_s_body, n_pages=n_pages, e=e),
        grid_spec=grid_spec,
        out_shape=jax.ShapeDtypeStruct((bs, 1, MLA_H * MLA_DV), BF16),
        compiler_params=_params("arbitrary"),
        name="mla_sample",
    )(page_table.reshape(-1), qbd, qr, qself, kself, vself, wukt, wuv, cache_t)
    return out.reshape(bs, MLA_H * MLA_DV)


HG_BS = 32
SCAN_GROUP = 2


def _hgrn_body(q_ref, k_ref, lf_ref, v_ref, g_ref, s0_ref, gain_ref, wind_ref, tri_ref,
               o_ref, sout_ref, st_scr, z_scr, *, c_len, bs, nc, grp):
    c = pl.program_id(1)
    nb = c_len // bs
    hc = HG_H * c_len
    pk = 16

    @pl.when(c == 0)
    def _():
        z_scr[...] = jnp.zeros(z_scr.shape, BF16)
        for gi in range(grp):
            for h in range(HG_H):
                st_scr[gi, h] = s0_ref[gi, h].T

    tri = tri_ref[...]
    gain = gain_ref[...]
    wind = wind_ref[...]
    row = lax.broadcasted_iota(I32, (c_len, c_len), 0)
    col = lax.broadcasted_iota(I32, (c_len, c_len), 1)
    keep = (col <= row) & ((row >= bs) == (col >= bs))
    late = lax.broadcasted_iota(I32, (c_len, HG_W), 0) >= bs
    for gi in range(grp):
        b = jnp.dot(tri, lf_ref[gi], precision=HI, preferred_element_type=F32)
        q = q_ref[gi].astype(F32)
        k = k_ref[gi]
        v = v_ref[gi]
        b_last = b[c_len - 1:c_len, :]
        qe = (q * jnp.exp(b)).astype(BF16)
        kd = (k * jnp.exp(b_last - b)).astype(BF16)
        if nb == 2:
            ref = b[bs - 1:bs, :]
            q_off = jnp.where(late, q * jnp.exp(jnp.minimum(b - ref, 0.0)), 0.0).astype(BF16)
            k_off = jnp.where(late, 0.0, k * jnp.exp(jnp.minimum(ref - b, 0.0))).astype(BF16)
        early = []
        for h in range(HG_H):
            hs = slice(h * LANES, (h + 1) * LANES)
            st = st_scr[gi, h]
            o_h = _dot_nt(qe[:, hs], st.astype(BF16))
            a_off = _dot_nt(q_off[:, hs], k_off[:, hs]) if nb == 2 else None
            st_scr[gi, h] = jnp.exp(b_last[:, hs]) * st + _dot_tn(v[:, hs], kd[:, hs])
            early.append((o_h, a_off))
        b3, q3, k3 = (t.reshape(nb, bs, HG_W) for t in (b * LOG2E, q, k))
        for s in range(bs):
            t0 = (s // pk) * pk
            z = q3[:, t0:, :] * k3[:, s:s + 1, :] * jnp.exp2(jnp.minimum(b3[:, t0:, :] - b3[:, s:s + 1, :], 0.0))
            z = z.astype(BF16)
            for ib in range(nb):
                for h in range(HG_H):
                    r0 = gi * hc + h * c_len + ib * bs
                    z_scr[r0 + t0:r0 + bs, s * LANES:(s + 1) * LANES] = z[ib, :, h * LANES:(h + 1) * LANES]
        g = g_ref[gi].astype(F32)
        for h in range(HG_H):
            hs = slice(h * LANES, (h + 1) * LANES)
            o_h, a_off = early[h]
            r0 = gi * hc + h * c_len
            a = jnp.where(keep, _dot(z_scr[r0:r0 + c_len, :], wind), 0.0)
            if nb == 2:
                a = a + a_off
            o_h = o_h + _dot(a.astype(BF16), v[:, hs])
            o_ref[gi, :, hs] = (_rms(o_h, gain[:, hs]) * g[:, hs]).astype(o_ref.dtype)

    @pl.when(c == nc - 1)
    def _():
        for gi in range(grp):
            for h in range(HG_H):
                sout_ref[gi, h] = st_scr[gi, h].T


def _hgrn_scan(hq, hk, hlf, hi, hg, s0, gain, b, l, c_len):
    bs = min(HG_BS, c_len)
    nc = l // c_len
    nb = c_len // bs
    assert nb in (1, 2) and c_len == nb * bs
    grp = SCAN_GROUP if b % SCAN_GROUP == 0 else 1
    chunk = lambda: pl.BlockSpec((grp, c_len, HG_W), lambda bi, ci: (bi, ci, 0))
    state = lambda: pl.BlockSpec((grp, HG_H, HG_DK, HG_DV), lambda bi, ci: (bi, 0, 0, 0))
    wind = jnp.tile(jnp.repeat(jnp.eye(bs, dtype=BF16), LANES, axis=0), (1, nb))
    tri = jnp.tril(jnp.ones((c_len, c_len), F32))
    seq = lambda t: t.reshape(b, l, HG_W)
    o, s_new = pl.pallas_call(
        functools.partial(_hgrn_body, c_len=c_len, bs=bs, nc=nc, grp=grp),
        grid=(b // grp, nc),
        in_specs=[chunk(), chunk(), chunk(), chunk(), chunk(), state(), _const((1, HG_W)),
                  _const((bs * LANES, c_len)), _const((c_len, c_len))],
        out_specs=[chunk(), state()],
        out_shape=[jax.ShapeDtypeStruct((b, l, HG_W), BF16),
                   jax.ShapeDtypeStruct((b, HG_H, HG_DK, HG_DV), F32)],
        scratch_shapes=[pltpu.VMEM((grp, HG_H, HG_DV, HG_DK), F32),
                        pltpu.VMEM((grp * HG_H * c_len, bs * LANES), BF16)],
        compiler_params=_params("parallel", "arbitrary"),
        name="hgrn_scan",
    )(seq(hq), seq(hk), seq(hlf), seq(hi), seq(hg), s0, gain, wind, tri)
    return o.reshape(b * l, HG_W), s_new


def _out_body(x_ref, a_ref, o_ref, w_ref, y_ref):
    half = a_ref.shape[1]
    y_ref[...] = x_ref[...] + _dot(a_ref[...], w_ref[0:half, :]) + _dot(o_ref[...], w_ref[half:2 * half, :])


def _out_proj(x, a, o, w, e):
    m = x.shape[0]
    tm = _row_tile(m, 512)
    half = a.shape[1]
    return pl.pallas_call(
        _out_body,
        grid=(m // tm,),
        in_specs=[_rows(tm, D_MODEL), _rows(tm, half), _rows(tm, half), _layer((2 * half, D_MODEL), e)],
        out_specs=_rows(tm, D_MODEL),
        out_shape=jax.ShapeDtypeStruct((m, D_MODEL), F32),
        compiler_params=_params("parallel"),
        name="out_proj",
    )(x, a, o, w)


OD_Q, OD_K, OD_V = 0, 8 * LANES, 12 * LANES
OD_XQK, OD_XV, OD_OG, OD_G = 16 * LANES, 20 * LANES, 24 * LANES, 28 * LANES
OD_COLS = 29 * LANES


def _odd_in_body(x_ref, g_ref, w_ref, gq_ref, gk_ref, gb_ref,
                 qpad_ref, kpad_ref, vpad_ref, xqk_ref, xv_ref, og_ref, gates_ref):
    hn = _rms(x_ref[...], g_ref[...]).astype(BF16)

    def headnorm(t, gain):
        ms = jnp.sum(t * t, axis=-1, keepdims=True) * (1.0 / MB_DH)
        return t * lax.rsqrt(ms + EPS) * gain

    gq, gk = gq_ref[...], gk_ref[...]
    for h in range(MB_H):
        t = _dot(hn, w_ref[:, OD_Q + h * LANES:OD_Q + (h + 1) * LANES])
        qpad_ref[:, h * LANES:(h + 1) * LANES] = (headnorm(t, gq) * MB_SCALE).astype(BF16)
    for h in range(MB_KVH):
        t = _dot(hn, w_ref[:, OD_K + h * LANES:OD_K + (h + 1) * LANES])
        kpad_ref[:, h * LANES:(h + 1) * LANES] = headnorm(t, gk)
    vpad_ref[...] = _dot(hn, w_ref[:, OD_V:OD_V + 4 * LANES])
    xqk_ref[...] = _dot(hn, w_ref[:, OD_XQK:OD_XQK + 4 * LANES])
    xv_ref[...] = _dot(hn, w_ref[:, OD_XV:OD_XV + 4 * LANES]).astype(xv_ref.dtype)
    og_ref[...] = _sigmoid(_dot(hn, w_ref[:, OD_OG:OD_OG + 4 * LANES])).astype(og_ref.dtype)
    gt = _dot(hn, w_ref[:, OD_G:OD_G + LANES]) + gb_ref[...]
    lane = lax.broadcasted_iota(I32, gt.shape, 1)
    logsig = jnp.minimum(gt, 0.0) - jnp.log(1.0 + jnp.exp(-jnp.abs(gt)))
    gates_ref[...] = jnp.where(lane < ML_H, gt, logsig)


def _odd_in(x, P, j, l):
    m = x.shape[0]
    tm = _row_tile(m)
    outs = [(8 * LANES, BF16), (4 * LANES, F32), (4 * LANES, F32), (2 * ML_QK, F32),
            (ML_H * ML_DV, BF16), (ML_H * ML_DV, BF16), (LANES, F32)]
    return pl.pallas_call(
        _odd_in_body,
        grid=(m // tm,),
        in_specs=[_rows(tm, D_MODEL), _layer((1, D_MODEL), l), _layer((D_MODEL, OD_COLS), j),
                  _layer((1, LANES), j), _layer((1, LANES), j), _layer((1, LANES), j)],
        out_specs=[_rows(tm, n) for n, _ in outs],
        out_shape=[jax.ShapeDtypeStruct((m, n), dt) for n, dt in outs],
        compiler_params=_params("parallel"),
        name="odd_in",
    )(x, P["norm_mix"], P["odd_w"], P["moba_gq"], P["moba_gk"], P["mlstm_gb"])


def _moba_p_body(q_ref, k_ref, v_ref, place_ref, a_ref, kmt_scr, qa_scr, m_scr, l_scr, acc_scr, *, nb):
    i = pl.program_id(1)
    blk = MB_BLOCK
    row = lax.broadcasted_iota(I32, (blk, blk), 0)
    col = lax.broadcasted_iota(I32, (blk, blk), 1)
    causal = col <= row
    lane = lax.broadcasted_iota(I32, (blk, LANES), 1)

    @pl.when(i == 0)
    def _():
        kmean = jnp.mean(k_ref[...].reshape(nb, blk, MB_KVH * LANES), axis=1)
        kmt_scr[...] = jnp.zeros(kmt_scr.shape, F32)
        for h in range(MB_H):
            g = h // MB_GRP
            kmt_scr[h * nb:(h + 1) * nb, h * LANES:(h + 1) * LANES] = kmean[:, g * LANES:(g + 1) * LANES]

    q_all = q_ref[...].astype(F32)
    gate = lax.dot_general(kmt_scr[...], q_all, (((1,), (1,)), ((), ())), precision=HI,
                           preferred_element_type=F32).reshape(MB_H, nb, blk)
    blk_id = lax.broadcasted_iota(I32, (1, nb, blk), 1)
    cand = jnp.where(lax.broadcasted_iota(I32, (1, nb, 1), 1) < i, 1.0, 0.0)
    rank = jnp.zeros((MB_H, nb, blk), F32)
    for c in range(nb - 1):
        gc = gate[:, c:c + 1, :]
        better = (gc > gate) | ((gc == gate) & (c < blk_id))
        rank = rank + jnp.where(better, cand[:, c:c + 1, :], 0.0)
    bias = jnp.where(rank < MB_TOPK, cand, 0.0) * (-NEG) + NEG
    qa_scr[...] = (q_all + _dot_tn(bias.reshape(MB_H * nb, blk).astype(BF16), place_ref[...])).astype(BF16)

    def step(j, diag):
        start = pl.multiple_of(j * blk, blk)
        for g in range(MB_KVH):
            gs = slice(g * LANES, (g + 1) * LANES)
            kj = k_ref[pl.ds(start, blk), gs]
            if not diag:
                kj = jnp.where(lane == MB_DH + j, 1.0, kj)
            kj = kj.astype(BF16)
            vj = v_ref[pl.ds(start, blk), gs].astype(BF16)
            for h in range(g * (MB_H // MB_KVH), (g + 1) * (MB_H // MB_KVH)):
                s = _dot_nt(qa_scr[:, h * LANES:(h + 1) * LANES], kj)
                if diag:
                    s = jnp.where(causal, s, NEG)
                    m_new = jnp.broadcast_to(jnp.max(s, axis=-1, keepdims=True), (blk, LANES))
                else:
                    m_old = m_scr[h]
                    m_new = jnp.maximum(m_old, jnp.max(s, axis=-1, keepdims=True))
                    alpha = jnp.exp(m_old - m_new)
                p = jnp.exp(s - jnp.concatenate([m_new] * (blk // LANES), axis=-1))
                l_new = jnp.broadcast_to(jnp.sum(p, axis=-1, keepdims=True), (blk, LANES))
                acc_new = _dot(p.astype(BF16), vj)
                l_scr[h] = l_new if diag else alpha * l_scr[h] + l_new
                acc_scr[h] = acc_new if diag else alpha * acc_scr[h] + acc_new
                m_scr[h] = m_new

    def body(j, c):
        step(j, False)
        return c

    step(i, True)
    lax.fori_loop(0, i, body, 0)
    for h in range(MB_H):
        a_ref[:, h * MB_DH:(h + 1) * MB_DH] = (acc_scr[h] / l_scr[h])[:, 0:MB_DH].astype(a_ref.dtype)


def _moba_prompt(qpad, kpad, vpad, b, l):
    assert l % MB_BLOCK == 0 and l // MB_BLOCK <= LANES - MB_DH
    nb = l // MB_BLOCK
    stat = pltpu.VMEM((MB_H, MB_BLOCK, LANES), F32)
    hc = jnp.arange(MB_H * nb)
    place = (jnp.arange(MB_H * LANES)[None, :] == ((hc // nb) * LANES + MB_DH + hc % nb)[:, None]).astype(BF16)
    return pl.pallas_call(
        functools.partial(_moba_p_body, nb=nb),
        grid=(b, nb),
        in_specs=[pl.BlockSpec((MB_BLOCK, 8 * LANES), lambda bi, i: (bi * nb + i, 0)),
                  pl.BlockSpec((l, 4 * LANES), lambda bi, i: (bi, 0)),
                  pl.BlockSpec((l, 4 * LANES), lambda bi, i: (bi, 0)),
                  _const((MB_H * nb, MB_H * LANES))],
        out_specs=pl.BlockSpec((MB_BLOCK, MB_H * MB_DH), lambda bi, i: (bi * nb + i, 0)),
        out_shape=jax.ShapeDtypeStruct((b * l, MB_H * MB_DH), BF16),
        scratch_shapes=[pltpu.VMEM((MB_H * nb, MB_H * LANES), F32), pltpu.VMEM((MB_BLOCK, MB_H * LANES), BF16),
                        stat, stat, stat],
        compiler_params=_params("parallel", "arbitrary"),
        name="moba_prompt",
    )(qpad, kpad, vpad, place)


def _moba_sel_body(pt_ref, qsel_ref, ck_hbm, sel_ref, buf, sem, *, n_pages, n_cand, jl):
    b = pl.program_id(0)
    slot = lax.rem(b, 2)
    ppb = MB_BLOCK // PAGE

    def page_copies(seq, slot_):
        return [pltpu.make_async_copy(ck_hbm.at[jl, pt_ref[seq * n_pages + p]], buf.at[slot_, p], sem.at[slot_])
                for p in range(n_pages)]

    @pl.when(b == 0)
    def _():
        _start_all(page_copies(0, 0))

    @pl.when(b + 1 < pl.num_programs(0))
    def _():
        _start_all(page_copies(b + 1, 1 - slot))

    _wait_all(page_copies(b, slot))
    lane = lax.broadcasted_iota(I32, (MB_KVH * MB_DH, LANES), 1)
    kmt = jnp.zeros((MB_KVH * MB_DH, LANES), F32)
    for c in range(n_cand):
        tot = buf[slot, c * ppb]
        for p in range(1, ppb):
            tot = tot + buf[slot, c * ppb + p]
        mean = jnp.sum(tot, axis=-1, keepdims=True) * (1.0 / MB_BLOCK)
        kmt = jnp.where(lane == c, mean, kmt)
    gate = jnp.dot(qsel_ref[...], kmt, precision=HI, preferred_element_type=F32)
    lane_g = lax.broadcasted_iota(I32, gate.shape, 1)
    rank = jnp.zeros(gate.shape, I32)
    for c in range(n_cand):
        gc = gate[:, c:c + 1]
        better = (gc > gate) | ((gc == gate) & (c < lane_g))
        rank = rank + jnp.where(better, 1, 0)
    out_lane = lax.broadcasted_iota(I32, sel_ref.shape, 1)
    out = jnp.zeros(sel_ref.shape, I32)
    for r in range(MB_TOPK):
        idx = jnp.sum(jnp.where((rank == r) & (lane_g < n_cand), lane_g, 0), axis=-1, keepdims=True)
        out = jnp.where(out_lane == r, idx, out)
    sel_ref[...] = out


def _moba_select(page_table, cache_kt, jl, qsel):
    bs = qsel.shape[0]
    n_pages = page_table.shape[1]
    n_cand = n_pages * PAGE // MB_BLOCK
    assert MB_TOPK <= n_cand <= LANES and n_pages * PAGE % MB_BLOCK == 0
    width = MB_KVH * MB_DH
    grid_spec = pltpu.PrefetchScalarGridSpec(
        num_scalar_prefetch=1,
        grid=(bs,),
        in_specs=[pl.BlockSpec((None, MB_H, width), lambda b, pt: (b, 0, 0)), pl.BlockSpec(memory_space=pl.ANY)],
        out_specs=pl.BlockSpec((None, MB_H, LANES), lambda b, pt: (b, 0, 0)),
        scratch_shapes=[pltpu.VMEM((2, n_pages, width, PAGE), F32), pltpu.SemaphoreType.DMA((2,))],
    )
    sel = pl.pallas_call(
        functools.partial(_moba_sel_body, n_pages=n_pages, n_cand=n_cand, jl=jl),
        grid_spec=grid_spec,
        out_shape=jax.ShapeDtypeStruct((bs, MB_H, LANES), I32),
        compiler_params=_params("arbitrary"),
        name="moba_select",
    )(page_table.reshape(-1), qsel, cache_kt)
    return sel[:, :, :MB_TOPK]


MB_GRP = MB_H // MB_KVH


def _moba_s_body(pt_ref, sel_ref, q_ref, kself_ref, vself_ref, ck_hbm, cv_hbm, o_ref, kbuf, vbuf, sem,
                 *, n_pages, jl):
    b = pl.program_id(0)
    slot = lax.rem(b, 2)
    ppb = MB_BLOCK // PAGE
    n = MB_TOPK * ppb

    def tile_copies(seq, slot_):
        out = []
        for h in range(MB_H):
            rows = pl.ds((h // MB_GRP) * MB_DH, MB_DH)
            for r in range(MB_TOPK):
                blk = sel_ref[(seq * MB_H + h) * MB_TOPK + r]
                for p in range(ppb):
                    page = pt_ref[seq * n_pages + blk * ppb + p]
                    i = h * n + r * ppb + p
                    out.append(pltpu.make_async_copy(ck_hbm.at[jl, page, rows], kbuf.at[slot_, i], sem.at[0, slot_]))
                    out.append(pltpu.make_async_copy(cv_hbm.at[jl, page, rows], vbuf.at[slot_, i], sem.at[1, slot_]))
        return out

    @pl.when(b == 0)
    def _():
        _start_all(tile_copies(0, 0))

    @pl.when(b + 1 < pl.num_programs(0))
    def _():
        _start_all(tile_copies(b + 1, 1 - slot))

    _wait_all(tile_copies(b, slot))
    for h in range(MB_H):
        g = h // MB_GRP
        q1 = q_ref[h:h + 1, :]
        kself, vself = kself_ref[g:g + 1, :], vself_ref[g:g + 1, :]
        q = jnp.broadcast_to(q1, (8, MB_DH)).astype(BF16)
        scores = [_dot(q, kbuf[slot, h * n + i].astype(BF16)) for i in range(n)]
        s_self = jnp.sum(q1 * kself, axis=-1, keepdims=True)
        m = s_self
        for s in scores:
            m = jnp.maximum(m, jnp.max(s, axis=-1, keepdims=True))
        p_self = jnp.exp(s_self - m)
        l = p_self
        acc = p_self * vself
        for i, s in enumerate(scores):
            p = jnp.exp(s - m)
            l = l + jnp.sum(p, axis=-1, keepdims=True)
            acc = acc + _dot_nt(p.astype(BF16), vbuf[slot, h * n + i].astype(BF16))
        o_ref[h:h + 1, :] = (acc / l)[0:1, :]


def _moba_sample(page_table, sel, cache_kt, cache_vt, jl, q, kself, vself):
    bs = q.shape[0]
    n_pages = page_table.shape[1]
    n_tiles = MB_H * MB_TOPK * (MB_BLOCK // PAGE)
    per_seq = lambda nh: pl.BlockSpec((None, nh, MB_DH), lambda b, pt, sl: (b, 0, 0))
    grid_spec = pltpu.PrefetchScalarGridSpec(
        num_scalar_prefetch=2,
        grid=(bs,),
        in_specs=[per_seq(MB_H), per_seq(MB_KVH), per_seq(MB_KVH),
                  pl.BlockSpec(memory_space=pl.ANY), pl.BlockSpec(memory_space=pl.ANY)],
        out_specs=per_seq(MB_H),
        scratch_shapes=[pltpu.VMEM((2, n_tiles, MB_DH, PAGE), F32), pltpu.VMEM((2, n_tiles, MB_DH, PAGE), F32),
                        pltpu.SemaphoreType.DMA((2, 2))],
    )
    return pl.pallas_call(
        functools.partial(_moba_s_body, n_pages=n_pages, jl=jl),
        grid_spec=grid_spec,
        out_shape=jax.ShapeDtypeStruct((bs, MB_H, MB_DH), F32),
        compiler_params=_params("arbitrary"),
        name="moba_sample",
    )(page_table.reshape(-1), sel.reshape(-1), q, kself, vself, cache_kt, cache_vt)


def _mlstm_body(xqk_ref, xv_ref, og_ref, gcol_ref, grow_ref, cw_ref, cb_ref, prev_ref, c0_ref, m0_ref,
                gain_ref, tri_ref, triu_ref,
                h_ref, cout_ref, mout_ref, convout_ref, full_scr, c_scr, m_scr, *, c_len, nc, l_last, grp):
    c = pl.program_id(1)
    pad = 8
    w = 2 * ML_QK

    @pl.when(c == 0)
    def _():
        full_scr[:, pad - (CONV_W - 1):pad, :] = prev_ref[...]
        c_scr[...] = c0_ref[...]
        m_scr[...] = m0_ref[...]

    row = lax.broadcasted_iota(I32, (c_len, c_len), 0)
    col = lax.broadcasted_iota(I32, (c_len, c_len), 1)
    causal = col <= row
    head_of_lane = lax.broadcasted_iota(I32, (1, ML_QK), 1) // ML_DK
    head_of_row = lax.broadcasted_iota(I32, (ML_QK, 1), 0) // ML_DK
    m_lane = lax.broadcasted_iota(I32, (1, ML_H), 1)
    one_col = jnp.where(lax.broadcasted_iota(I32, (c_len, LANES), 1) == 0, 1.0, 0.0).astype(BF16)
    gain = gain_ref[...]
    tri, triu = tri_ref[...], triu_ref[...]

    for gi in range(grp):
        full_scr[gi, pad:pad + c_len, :] = xqk_ref[gi]
        conv = cb_ref[...]
        for t in range(CONV_W):
            o = pad - (CONV_W - 1) + t
            conv = conv + full_scr[gi, o:o + c_len, :] * cw_ref[t:t + 1, :]

        @pl.when(c == nc - 1)
        def _():
            convout_ref[gi] = full_scr[gi, pad + l_last - (CONV_W - 1):pad + l_last, :]

        full_scr[gi, pad - (CONV_W - 1):pad, :] = full_scr[gi, pad + c_len - (CONV_W - 1):pad + c_len, :]

        qk = _silu(conv)
        qf = qk[:, 0:ML_QK]
        kf = qk[:, ML_QK:w] * (ML_DK ** -0.5)
        kb = kf.astype(BF16)
        gcol, grow = gcol_ref[gi], grow_ref[gi]
        f_col = jnp.dot(tri, gcol, precision=HI, preferred_element_type=F32)
        f_row = jnp.dot(grow, triu, precision=HI, preferred_element_type=F32)
        state = c_scr[gi]
        state_b = state.astype(BF16)
        m_all = m_scr[gi]
        xv = xv_ref[gi]
        og = og_ref[gi].astype(F32)
        decay_col = jnp.zeros((ML_QK, 1), F32)
        upd = jnp.zeros((ML_QK, 2 * LANES), F32)
        m_next = jnp.zeros((1, ML_H), F32)
        for h in range(ML_H):
            fc = f_col[:, ML_H + h:ML_H + h + 1]
            ic = gcol[:, h:h + 1]
            fr = f_row[ML_H + h:ML_H + h + 1, :]
            ir = grow[h:h + 1, :]
            m_prev = m_all[:, h:h + 1]
            dmat = jnp.where(causal, fc - fr + ir, NEG)
            inter = fc + m_prev
            mt = jnp.maximum(jnp.max(dmat, axis=-1, keepdims=True), inter)
            qh = jnp.where(head_of_lane == h, qf, 0.0).astype(BF16)
            wd = jnp.exp(dmat - mt) * _dot_nt(qh, kb)
            a = jnp.exp(inter - mt)
            vaug = jnp.concatenate([xv[:, h * ML_DV:(h + 1) * ML_DV], one_col], axis=-1)
            num = _dot(wd.astype(BF16), vaug) + a * _dot(qh, state_b)
            den = num[:, ML_DV:ML_DV + 1]
            hh = num[:, 0:ML_DV] / jnp.maximum(jnp.abs(den), jnp.exp(-mt))
            hs = slice(h * ML_DV, (h + 1) * ML_DV)
            h_ref[gi, :, hs] = (_rms(hh, gain[:, hs]) * og[:, hs]).astype(h_ref.dtype)
            m_new = mt[c_len - 1:c_len, :]
            f_last = fc[c_len - 1:c_len, :]
            gk = (jnp.where(head_of_lane == h, kf, 0.0) * jnp.exp(f_last - fc + ic - m_new)).astype(BF16)
            upd = upd + _dot_tn(gk, vaug)
            decay_col = decay_col + jnp.where(head_of_row == h, jnp.exp(f_last + m_prev - m_new), 0.0)
            m_next = m_next + jnp.where(m_lane == h, m_new, 0.0)
        c_scr[gi] = decay_col * state + upd
        m_scr[gi] = m_next

    @pl.when(c == nc - 1)
    def _():
        cout_ref[...] = c_scr[...]
        mout_ref[...] = m_scr[...]


def _mlstm_scan(xqk, xv, og, gcol, grow, cw, cb, prev, c0, m0, gain, j, b, l_pad, c_len, l_last):
    nc = l_pad // c_len
    w = 2 * ML_QK
    grp = SCAN_GROUP if b % SCAN_GROUP == 0 else 1
    chunk = lambda n: pl.BlockSpec((grp, c_len, n), lambda bi, ci: (bi, ci, 0))
    per_b = lambda n2, n3: pl.BlockSpec((grp, n2, n3), lambda bi, ci: (bi, 0, 0))
    tri = jnp.tril(jnp.ones((c_len, c_len), F32))
    seq = lambda t: t.reshape(b, l_pad, t.shape[-1])
    h, caug, m_new, conv_new = pl.pallas_call(
        functools.partial(_mlstm_body, c_len=c_len, nc=nc, l_last=l_last, grp=grp),
        grid=(b // grp, nc),
        in_specs=[chunk(w), chunk(ML_H * ML_DV), chunk(ML_H * ML_DV), chunk(LANES),
                  pl.BlockSpec((grp, None, 8, c_len), lambda bi, ci: (bi, ci, 0, 0)),
                  _layer((CONV_W, w), j), _layer((1, w), j), per_b(CONV_W - 1, w),
                  per_b(ML_QK, 2 * LANES), per_b(1, ML_H), _layer((1, ML_H * ML_DV), j),
                  _const((c_len, c_len)), _const((c_len, c_len))],
        out_specs=[chunk(ML_H * ML_DV), per_b(ML_QK, 2 * LANES), per_b(1, ML_H), per_b(CONV_W - 1, w)],
        out_shape=[jax.ShapeDtypeStruct((b, l_pad, ML_H * ML_DV), BF16),
                   jax.ShapeDtypeStruct((b, ML_QK, 2 * LANES), F32),
                   jax.ShapeDtypeStruct((b, 1, ML_H), F32),
                   jax.ShapeDtypeStruct((b, CONV_W - 1, w), F32)],
        scratch_shapes=[pltpu.VMEM((grp, c_len + 8, w), F32), pltpu.VMEM((grp, ML_QK, 2 * LANES), F32),
                        pltpu.VMEM((grp, 1, ML_H), F32)],
        compiler_params=_params("parallel", "arbitrary"),
        name="mlstm_scan",
    )(seq(xqk), seq(xv), seq(og), seq(gcol), grow, cw, cb, prev, c0, m0, gain, tri, tri.T)
    return h.reshape(b * l_pad, ML_H * ML_DV), caug, m_new, conv_new


def _prepare(norm_ffn1, norm_mix, norm_ffn2, ffn1_w_in, ffn1_w_out, ffn2_w_in, ffn2_w_out, even_w_in, even_w_out,
             mla_q_norm, mla_w_uq, mla_kv_norm, mla_w_uk, mla_w_uv, mla_qn_norm, mla_qr_norm, mla_kn_norm,
             mla_kr_norm, hgrn_lb_raw, hgrn_out_norm, odd_w_in, odd_w_out, moba_q_norm, moba_k_norm,
             mlstm_conv_w, mlstm_conv_b, mlstm_b_i, mlstm_b_f, mlstm_out_norm):
    ne, no = even_w_in.shape[0], odd_w_in.shape[0]
    row = lambda t: t[:, None, :].astype(F32)
    P = dict(norm_ffn1=row(norm_ffn1), norm_mix=row(norm_mix), norm_ffn2=row(norm_ffn2),
             ffn1_w_in=ffn1_w_in.astype(BF16), ffn1_w_out=ffn1_w_out.astype(BF16),
             ffn2_w_in=ffn2_w_in.astype(BF16), ffn2_w_out=ffn2_w_out.astype(BF16),
             even_w_out=even_w_out.astype(BF16), odd_w_out=odd_w_out.astype(BF16))
    w = even_w_in
    kr = jnp.zeros((ne, D_MODEL, LANES), F32).at[:, :, 64:96].set(w[:, :, 384:416])
    P["even_w"] = jnp.concatenate([w[:, :, 0:384], kr, w[:, :, 416:]], axis=-1).astype(BF16)
    pad_heads = lambda t, n: jnp.pad(t, ((0, 0), (0, 0), (0, 0), (0, LANES - n))).reshape(ne, t.shape[1], 8 * LANES)
    P["mla_wuq"] = pad_heads(mla_w_uq, MLA_DN + MLA_DR).astype(BF16)
    P["mla_wuk"] = pad_heads(mla_w_uk, MLA_DN).astype(BF16)
    P["mla_wukt"] = jnp.swapaxes(mla_w_uk.reshape(ne, MLA_R, MLA_H * MLA_DN), 1, 2).astype(BF16)
    P["mla_wuv_flat"] = mla_w_uv.reshape(ne, MLA_R, MLA_H * MLA_DV).astype(BF16)
    wuv = jnp.transpose(mla_w_uv, (0, 2, 1, 3))
    odd_head = (jnp.arange(MLA_H) % 2 == 1)[None, :, None, None]
    P["mla_wuv"] = jnp.where(odd_head, jnp.pad(wuv, ((0, 0),) * 3 + ((MLA_DV, 0),)),
                             jnp.pad(wuv, ((0, 0),) * 3 + ((0, MLA_DV),))).astype(BF16)
    z32 = jnp.zeros((ne, LANES - MLA_DN - MLA_DR), F32)
    P["mla_q_norm"], P["mla_kv_norm"] = row(mla_q_norm), row(mla_kv_norm)
    P["mla_gq"] = row(jnp.concatenate([mla_qn_norm, mla_qr_norm, z32], axis=-1))
    P["mla_gk"] = row(jnp.concatenate([mla_kn_norm, jnp.zeros((ne, LANES - MLA_DN), F32)], axis=-1))
    P["mla_gkr"] = row(jnp.concatenate([jnp.zeros((ne, MLA_DN), F32), mla_kr_norm, z32], axis=-1))
    P["mla_kn_gain"] = mla_kn_norm.astype(F32)
    seg = jnp.arange(LANES)
    seg_id = jnp.where(seg < MLA_DN, 0, jnp.where(seg < MLA_DN + MLA_DR, 1, 2 + seg))
    seg_len = jnp.where(seg < MLA_DN, MLA_DN, MLA_DR).astype(F32)
    P["bseg"] = jnp.where(seg_id[:, None] == seg_id[None, :], 1.0 / seg_len[None, :], 0.0).astype(BF16)
    p_lb = jax.nn.softmax(hgrn_lb_raw.astype(F32), axis=0)
    lb = jnp.maximum(jnp.cumsum(p_lb, axis=0) - p_lb[0], 0.0)
    P["hg_loglb"], P["hg_log1mlb"], P["hg_1mlb"] = row(jnp.log(lb)), row(jnp.log1p(-lb)), row(1.0 - lb)
    P["hg_gain"] = row(jnp.tile(hgrn_out_norm, (1, HG_H)))
    w = odd_w_in
    c0 = MB_H * MB_DH
    c1 = c0 + MB_KVH * MB_DH
    c2 = c1 + MB_KVH * MB_DH
    c3 = c2 + 2 * ML_QK
    c4 = c3 + ML_H * ML_DV
    c5 = c4 + 2 * ML_H
    pad_h = lambda t, nh: jnp.pad(t.reshape(no, D_MODEL, nh, MB_DH),
                                  ((0, 0), (0, 0), (0, 0), (0, LANES - MB_DH))).reshape(no, D_MODEL, nh * LANES)
    gates = jnp.pad(w[:, :, c4:c5], ((0, 0), (0, 0), (0, LANES - 2 * ML_H)))
    P["odd_w"] = jnp.concatenate([pad_h(w[:, :, 0:c0], MB_H), pad_h(w[:, :, c0:c1], MB_KVH),
                                  pad_h(w[:, :, c1:c2], MB_KVH), w[:, :, c2:c4], w[:, :, c5:], gates],
                                 axis=-1).astype(BF16)
    zpad = jnp.zeros((no, LANES - MB_DH), F32)
    P["moba_gq"] = row(jnp.concatenate([moba_q_norm, zpad], axis=-1))
    P["moba_gk"] = row(jnp.concatenate([moba_k_norm, zpad], axis=-1))
    P["mlstm_gb"] = row(jnp.concatenate([mlstm_b_i, mlstm_b_f, jnp.zeros((no, LANES - 2 * ML_H), F32)], axis=-1))
    P["mlstm_cw"] = mlstm_conv_w.astype(F32)
    P["mlstm_cb"] = row(mlstm_conv_b)
    P["mlstm_gain"] = row(jnp.tile(mlstm_out_norm, (1, ML_H)))
    return P


def _rope_tables(pos, n_rows):
    half = MLA_DR // 2
    freqs = ROPE_THETA ** (-jnp.arange(half, dtype=F32) / half)
    ang = pos.astype(F32)[:, None] * freqs[None, :]
    cos, sin = jnp.cos(ang), jnp.sin(ang)
    n = pos.shape[0]
    z = lambda k: jnp.zeros((n, k), F32)
    cos_t = jnp.concatenate([jnp.ones((n, MLA_DN), F32), cos, cos, z(LANES - MLA_DN - MLA_DR)], axis=-1)
    sin_a = jnp.concatenate([z(MLA_DN + half), sin, z(LANES - MLA_DN - MLA_DR)], axis=-1)
    sin_b = jnp.concatenate([z(MLA_DN), -sin, z(LANES - MLA_DN - half)], axis=-1)
    rep = n_rows // n
    return tuple(jnp.tile(t, (rep, 1)) for t in (cos_t, sin_a, sin_b))


def _pad_time(t, b, l, l_pad, value=0.0):
    if l_pad == l:
        return t
    n = t.shape[-1]
    fill = jnp.broadcast_to(jnp.asarray(value, t.dtype), (b, l_pad - l, n))
    return jnp.concatenate([t.reshape(b, l, n), fill], axis=1).reshape(b * l_pad, n)


def _even_layer(x, P, e, l, b, seq, tabs, ctx):
    qcat, kcat, rows, vlat, hq, hk, hlf, hi, hg = _even_in(x, P, e, l, tabs)
    if ctx is None:
        a = _mla_prompt(qcat, kcat, vlat, P["mla_wuv"], e, b, seq)
        s0 = jnp.zeros((b, HG_H, HG_DK, HG_DV), F32)
        c_len, l_pad = min(64, seq), seq
    else:
        q3 = qcat.reshape(b, MLA_H, LANES).astype(F32)
        eye = jnp.eye(MLA_H, dtype=F32)
        qbd = jnp.einsum("bhd,hg->bhgd", q3[:, :, :MLA_DN] * P["mla_kn_gain"][e][None, None, :], eye)
        a = _mla_sample(ctx["page_table"], ctx["cache_mla_t"], e, qbd.reshape(b, MLA_H, MLA_H * MLA_DN),
                        q3[:, :, MLA_DN:MLA_DN + MLA_DR], qcat.reshape(b, MLA_H, LANES),
                        kcat.reshape(b, MLA_H, LANES), vlat.reshape(b, 1, MLA_R),
                        P["mla_wukt"], P["mla_wuv_flat"])
        s0 = ctx["state_hgrn"][e].astype(F32)
        c_len = l_pad = 16
    pad = lambda t: _pad_time(t, b, seq, l_pad)
    o, s_new = _hgrn_scan(pad(hq), pad(hk), pad(hlf), pad(hi), pad(hg), s0, P["hg_gain"][e], b, l_pad, c_len)
    if l_pad != seq:
        o = o.reshape(b, l_pad, HG_W)[:, :seq].reshape(b * seq, HG_W)
    y = _out_proj(x, a, o, P["even_w_out"], e)
    return y, rows.reshape(b, seq, MLA_R + MLA_DR), s_new


def _odd_layer(x, P, j, l, b, seq, ctx):
    qpad, kpad, vpad, xqk, xv, og, gates = _odd_in(x, P, j, l)
    unpad = lambda t: t.reshape(b, seq, MB_KVH, LANES)[..., :MB_DH]
    k_new, v_new = unpad(kpad), unpad(vpad)
    if ctx is None:
        a = _moba_prompt(qpad, kpad, vpad, b, seq)
        prev = jnp.zeros((b, CONV_W - 1, 2 * ML_QK), F32)
        caug0 = jnp.zeros((b, ML_QK, 2 * LANES), F32)
        m0 = jnp.zeros((b, 1, ML_H), F32)
        c_len, l_pad = min(64, seq), seq
    else:
        assert seq == 1
        pt = ctx["page_table"]
        qh = qpad.reshape(b, MB_H, LANES)[:, :, :MB_DH].astype(F32)
        on_kv = jax.nn.one_hot(jnp.arange(MB_H) // MB_GRP, MB_KVH, dtype=F32)
        qsel = jnp.einsum("bhd,hg->bhgd", qh, on_kv).reshape(b, MB_H, MB_KVH * MB_DH)
        sel = _moba_select(pt, ctx["cache_moba_kt"], j, qsel)
        a = _moba_sample(pt, sel, ctx["cache_moba_kt"], ctx["cache_moba_vt"], j, qh,
                         k_new.reshape(b, MB_KVH, MB_DH), v_new.reshape(b, MB_KVH, MB_DH))
        a = a.reshape(b, MB_H * MB_DH).astype(BF16)
        prev = ctx["state_mlstm_conv"][j].astype(F32)
        caug0 = jnp.concatenate([ctx["state_mlstm_c"][j].astype(F32),
                                 ctx["state_mlstm_n"][j].astype(F32)[..., None],
                                 jnp.zeros((b, ML_H, ML_DK, LANES - 1), F32)], axis=-1).reshape(b, ML_QK, 2 * LANES)
        m0 = ctx["state_mlstm_m"][j].astype(F32).reshape(b, 1, ML_H)
        c_len = l_pad = 16
    nc = l_pad // c_len
    gate_fill = jnp.where(jnp.arange(LANES) < ML_H, NEG, 0.0)
    gcol = _pad_time(gates, b, seq, l_pad, gate_fill)
    grow = jnp.swapaxes(gcol[:, :8].reshape(b, nc, c_len, 8), 2, 3)
    pad = lambda t: _pad_time(t, b, seq, l_pad)
    l_last = seq - (nc - 1) * c_len
    h, caug, m_new, conv_new = _mlstm_scan(pad(xqk), pad(xv), pad(og), gcol, grow, P["mlstm_cw"], P["mlstm_cb"],
                                           prev, caug0, m0, P["mlstm_gain"], j, b, l_pad, c_len, l_last)
    if l_pad != seq:
        h = h.reshape(b, l_pad, ML_H * ML_DV)[:, :seq].reshape(b * seq, ML_H * ML_DV)
    y = _out_proj(x, a, h, P["odd_w_out"], j)
    caug = caug.reshape(b, ML_H, ML_DK, 2 * LANES)
    return y, k_new, v_new, caug[..., :ML_DV], caug[..., ML_DV], m_new.reshape(b, ML_H), conv_new


def _trunk(x, pos, P, ctx):
    b, seq, _ = x.shape
    m = b * seq
    x = x.reshape(m, D_MODEL).astype(F32)
    depth = P["norm_mix"].shape[0]
    tabs = _rope_tables(pos, max(_row_tile(m), seq))
    new = {k: [] for k in ("mla", "hgrn", "moba_k", "moba_v", "mlstm_c", "mlstm_n", "mlstm_m", "mlstm_conv")}
    for l in range(depth):
        x = _ffn(x, P["norm_ffn1"], P["ffn1_w_in"], P["ffn1_w_out"], l)
        if l % 2 == 0:
            x, rows, s = _even_layer(x, P, l // 2, l, b, seq, tabs, ctx)
            new["mla"].append(rows)
            new["hgrn"].append(s)
        else:
            x, k, v, c, n, mm, cv = _odd_layer(x, P, l // 2, l, b, seq, ctx)
            for name, val in zip(("moba_k", "moba_v", "mlstm_c", "mlstm_n", "mlstm_m", "mlstm_conv"),
                                 (k, v, c, n, mm, cv)):
                new[name].append(val)
        x = _ffn(x, P["norm_ffn2"], P["ffn2_w_in"], P["ffn2_w_out"], l)
    return x.reshape(b, seq, D_MODEL), {k: jnp.stack(v).astype(F32) for k, v in new.items()}


def kernel(x_prompt, x_sample, cache_mla, state_hgrn, cache_moba_k, cache_moba_v, state_mlstm_c, state_mlstm_n, state_mlstm_m, state_mlstm_conv, page_table, norm_ffn1, norm_mix, norm_ffn2, ffn1_w_in, ffn1_w_out, ffn2_w_in, ffn2_w_out, even_w_in, even_w_out, mla_q_norm, mla_w_uq, mla_kv_norm, mla_w_uk, mla_w_uv, mla_qn_norm, mla_qr_norm, mla_kn_norm, mla_kr_norm, hgrn_lb_raw, hgrn_out_norm, odd_w_in, odd_w_out, moba_q_norm, moba_k_norm, mlstm_conv_w, mlstm_conv_b, mlstm_b_i, mlstm_b_f, mlstm_out_norm):
    P = _prepare(norm_ffn1, norm_mix, norm_ffn2, ffn1_w_in, ffn1_w_out, ffn2_w_in, ffn2_w_out, even_w_in,
                 even_w_out, mla_q_norm, mla_w_uq, mla_kv_norm, mla_w_uk, mla_w_uv, mla_qn_norm, mla_qr_norm,
                 mla_kn_norm, mla_kr_norm, hgrn_lb_raw, hgrn_out_norm, odd_w_in, odd_w_out, moba_q_norm,
                 moba_k_norm, mlstm_conv_w, mlstm_conv_b, mlstm_b_i, mlstm_b_f, mlstm_out_norm)
    n_pool = cache_moba_k.shape[1]
    page_t = lambda c: jnp.transpose(c, (0, 1, 3, 4, 2)).reshape(-1, n_pool, MB_KVH * MB_DH, PAGE)
    ctx = dict(page_table=page_table.astype(I32), cache_mla_t=jnp.swapaxes(cache_mla, 2, 3), state_hgrn=state_hgrn,
               cache_moba_kt=page_t(cache_moba_k), cache_moba_vt=page_t(cache_moba_v), state_mlstm_c=state_mlstm_c,
               state_mlstm_n=state_mlstm_n, state_mlstm_m=state_mlstm_m, state_mlstm_conv=state_mlstm_conv)
    y_p, sp = _trunk(x_prompt, jnp.arange(x_prompt.shape[1]), P, None)
    past_len = page_table.shape[1] * PAGE
    y_s, ss = _trunk(x_sample, past_len + jnp.arange(x_sample.shape[1]), P, ctx)
    names = ("mla", "hgrn", "moba_k", "moba_v", "mlstm_c", "mlstm_n", "mlstm_m", "mlstm_conv")
    out = [y_p, y_s]
    for name in names:
        out += [sp[name], ss[name]]
    return tuple(out)
```

```python
import functools

import jax
import jax.numpy as jnp
from jax import lax
from jax.experimental import pallas as pl
from jax.experimental.pallas import tpu as pltpu

F32, BF16, I32 = jnp.float32, jnp.bfloat16, jnp.int32
HI = lax.Precision.HIGHEST
EPS = 1e-6
NEG = -1e30
LOG2E = 1.4426950408889634

D_MODEL = 1024
FFN_HIDDEN = 2048
PAGE = 128
MLA_H, MLA_QR, MLA_R, MLA_DN, MLA_DR, MLA_DV = 8, 256, 128, 64, 32, 64
ROPE_THETA = 10000.0
MLA_SCALE = (MLA_DN + MLA_DR) ** -0.5 * LOG2E
HG_H, HG_DK, HG_DV = 4, 128, 128
HG_W = HG_H * HG_DK
MB_H, MB_KVH, MB_DH, MB_BLOCK, MB_TOPK = 8, 4, 64, 256, 3
MB_SCALE = MB_DH ** -0.5 * LOG2E
ML_H, ML_DK, ML_DV, CONV_W = 4, 64, 128, 4
ML_QK = ML_H * ML_DK

LANES = 128
VMEM_LIMIT = 56 << 20


def _dot(a, b):
    return jnp.dot(a, b, preferred_element_type=F32)


def _dot_nt(a, b):
    return lax.dot_general(a, b, (((1,), (1,)), ((), ())), preferred_element_type=F32)


def _dot_tn(a, b):
    return lax.dot_general(a, b, (((0,), (0,)), ((), ())), preferred_element_type=F32)


def _split3(x):
    hi = x.astype(BF16)
    r = x - hi.astype(F32)
    mid = r.astype(BF16)
    return hi, mid, (r - mid.astype(F32)).astype(BF16)


def _rms(x, g):
    return x * lax.rsqrt(jnp.mean(x * x, axis=-1, keepdims=True) + EPS) * g


def _sigmoid(x):
    return 1.0 / (1.0 + jnp.exp(-x))


def _silu(x):
    return x * _sigmoid(x)


def _rows(tm, n):
    return pl.BlockSpec((tm, n), lambda i: (i, 0))


def _layer(shape, l):
    zeros = (0,) * len(shape)
    return pl.BlockSpec((None,) + tuple(shape), lambda *_: (l,) + zeros)


def _const(shape):
    zeros = (0,) * len(shape)
    return pl.BlockSpec(tuple(shape), lambda *_: zeros)


def _params(*sem):
    return pltpu.CompilerParams(dimension_semantics=sem, vmem_limit_bytes=VMEM_LIMIT)


def _row_tile(m, cap=256):
    return min(cap, m)


def _ffn_body(x_ref, g_ref, win_ref, wout_ref, o_ref, *, fc):
    x = x_ref[...]
    hn = _rms(x, g_ref[...]).astype(BF16)
    acc = jnp.zeros_like(x)
    for c in range(FFN_HIDDEN // fc):
        a = _dot(hn, win_ref[:, c * fc:(c + 1) * fc])
        b = _dot(hn, win_ref[:, FFN_HIDDEN + c * fc:FFN_HIDDEN + (c + 1) * fc])
        acc = acc + _dot((_silu(a) * b).astype(BF16), wout_ref[c * fc:(c + 1) * fc, :])
    o_ref[...] = x + 0.5 * acc


def _ffn(x, g, w_in, w_out, l):
    m = x.shape[0]
    tm = _row_tile(m, 512)
    return pl.pallas_call(
        functools.partial(_ffn_body, fc=512),
        grid=(m // tm,),
        in_specs=[_rows(tm, D_MODEL), _layer((1, D_MODEL), l),
                  _layer((D_MODEL, 2 * FFN_HIDDEN), l), _layer((FFN_HIDDEN, D_MODEL), l)],
        out_specs=_rows(tm, D_MODEL),
        out_shape=jax.ShapeDtypeStruct((m, D_MODEL), F32),
        compiler_params=_params("parallel"),
        name="ffn",
    )(x, g, w_in, w_out)


EV_COLS = 256 + 128 + 128 + 4 * HG_W


def _even_in_body(x_ref, g_ref, w_ref, qg_ref, wuq_ref, gq_ref, kvg_ref, wuk_ref, gk_ref, gkr_ref, bseg_ref,
                  cos_ref, sa_ref, sb_ref, llb_ref, l1m_ref, omlb_ref,
                  qcat_ref, kcat_ref, rows_ref, vlat_ref, hq_ref, hk_ref, hlf_ref, hi_ref, hg_ref):
    hn = _rms(x_ref[...], g_ref[...]).astype(BF16)
    cos_t, sin_a, sin_b = cos_ref[...], sa_ref[...], sb_ref[...]
    bseg = bseg_ref[...]

    def rope(t):
        return t * cos_t + pltpu.roll(t, 16, 1) * sin_a + pltpu.roll(t, LANES - 16, 1) * sin_b

    def segnorm(t, gain):
        ms = _dot((t * t).astype(BF16), bseg)
        return t * lax.rsqrt(ms + EPS) * gain

    ql = _dot(hn, w_ref[:, 0:256])
    qh = _dot(_rms(ql, qg_ref[...]).astype(BF16), wuq_ref[...])
    gq = gq_ref[...]
    for h in range(MLA_H):
        t = segnorm(qh[:, h * LANES:(h + 1) * LANES], gq)
        qcat_ref[:, h * LANES:(h + 1) * LANES] = (rope(t) * MLA_SCALE).astype(BF16)

    cn = _rms(_dot(hn, w_ref[:, 256:384]), kvg_ref[...])
    krl = _dot(hn, w_ref[:, 384:512])
    ms = jnp.sum(krl * krl, axis=-1, keepdims=True) * (1.0 / MLA_DR)
    krr = rope(krl * lax.rsqrt(ms + EPS) * gkr_ref[...])
    rows_ref[:, 0:MLA_R] = cn
    rows_ref[:, MLA_R:MLA_R + MLA_DR] = pltpu.roll(krr, 64, 1)[:, 0:MLA_DR]
    cnb = cn.astype(BF16)
    vlat_ref[...] = cnb
    kn = _dot(cnb, wuk_ref[...])
    gk = gk_ref[...]
    for h in range(MLA_H):
        t = segnorm(kn[:, h * LANES:(h + 1) * LANES], gk)
        kcat_ref[:, h * LANES:(h + 1) * LANES] = (t + krr).astype(BF16)

    o = 512
    hq_ref[...] = _silu(_dot(hn, w_ref[:, o:o + HG_W])).astype(hq_ref.dtype)
    hf = _dot(hn, w_ref[:, o + HG_W:o + 2 * HG_W])
    logsig = jnp.minimum(hf, 0.0) - jnp.log(1.0 + jnp.exp(-jnp.abs(hf)))
    aa, bb = llb_ref[...], l1m_ref[...] + logsig
    hlf_ref[...] = jnp.maximum(aa, bb) + jnp.log(1.0 + jnp.exp(-jnp.abs(aa - bb)))
    hk_ref[...] = omlb_ref[...] * _sigmoid(-hf)
    hi_ref[...] = _dot(hn, w_ref[:, o + 2 * HG_W:o + 3 * HG_W]).astype(hi_ref.dtype)
    hg_ref[...] = _silu(_dot(hn, w_ref[:, o + 3 * HG_W:o + 4 * HG_W])).astype(hg_ref.dtype)


def _even_in(x, P, e, l, tabs):
    m = x.shape[0]
    tm = _row_tile(m)
    cos_t, sin_a, sin_b = tabs
    nt = cos_t.shape[0] // tm
    tab = pl.BlockSpec((tm, LANES), lambda i: (i % nt, 0))
    outs = [(8 * LANES, BF16), (8 * LANES, BF16), (MLA_R + MLA_DR, F32), (MLA_R, BF16),
            (HG_W, BF16), (HG_W, F32), (HG_W, F32), (HG_W, BF16), (HG_W, BF16)]
    return pl.pallas_call(
        _even_in_body,
        grid=(m // tm,),
        in_specs=[_rows(tm, D_MODEL), _layer((1, D_MODEL), l), _layer((D_MODEL, EV_COLS), e),
                  _layer((1, MLA_QR), e), _layer((MLA_QR, 8 * LANES), e), _layer((1, LANES), e),
                  _layer((1, MLA_R), e), _layer((MLA_R, 8 * LANES), e), _layer((1, LANES), e),
                  _layer((1, LANES), e), _const((LANES, LANES)), tab, tab, tab,
                  _layer((1, HG_W), e), _layer((1, HG_W), e), _layer((1, HG_W), e)],
        out_specs=[_rows(tm, n) for n, _ in outs],
        out_shape=[jax.ShapeDtypeStruct((m, n), dt) for n, dt in outs],
        compiler_params=_params("parallel"),
        name="even_in",
    )(x, P["norm_mix"], P["even_w"], P["mla_q_norm"], P["mla_wuq"], P["mla_gq"], P["mla_kv_norm"],
      P["mla_wuk"], P["mla_gk"], P["mla_gkr"], P["bseg"], cos_t, sin_a, sin_b,
      P["hg_loglb"], P["hg_log1mlb"], P["hg_1mlb"])


def _mla_p_body(q_ref, k_ref, v_ref, wuv_ref, a_ref, m_scr, l_scr, acc_scr, *, tq):
    i = pl.program_id(1)
    row = lax.broadcasted_iota(I32, (tq, tq), 0)
    col = lax.broadcasted_iota(I32, (tq, tq), 1)
    causal = col <= row

    def step(j, diag):
        start = pl.multiple_of(j * tq, tq)
        v = v_ref[pl.ds(start, tq), :]
        for h in range(MLA_H):
            q = q_ref[:, h * LANES:(h + 1) * LANES]
            k = k_ref[pl.ds(start, tq), h * LANES:(h + 1) * LANES]
            s = _dot_nt(q, k)
            if diag:
                s = jnp.where(causal, s, NEG)
                m_new = jnp.broadcast_to(jnp.max(s, axis=-1, keepdims=True), (tq, LANES))
            else:
                m_old = m_scr[h]
                m_new = jnp.maximum(m_old, jnp.max(s, axis=-1, keepdims=True))
                alpha = jnp.exp2(m_old - m_new)
            p = jnp.exp2(s - jnp.concatenate([m_new] * (tq // LANES), axis=-1))
            l_new = jnp.broadcast_to(jnp.sum(p, axis=-1, keepdims=True), (tq, LANES))
            acc_new = _dot(p.astype(BF16), v)
            l_scr[h] = l_new if diag else alpha * l_scr[h] + l_new
            acc_scr[h] = acc_new if diag else alpha * acc_scr[h] + acc_new
            m_scr[h] = m_new

    def body(j, c):
        step(j, False)
        return c

    step(i, True)
    lax.fori_loop(0, i, body, 0)
    for hp in range(MLA_H // 2):
        pair = [_dot((acc_scr[h] / l_scr[h]).astype(BF16), wuv_ref[h]) for h in (2 * hp, 2 * hp + 1)]
        a_ref[:, hp * LANES:(hp + 1) * LANES] = (pair[0] + pair[1]).astype(a_ref.dtype)


def _mla_prompt(qcat, kcat, vlat, wuv, e, b, l):
    tq = min(256, l)
    assert tq % LANES == 0 and l % tq == 0
    nq = l // tq
    stat = pltpu.VMEM((MLA_H, tq, LANES), F32)
    return pl.pallas_call(
        functools.partial(_mla_p_body, tq=tq),
        grid=(b, nq),
        in_specs=[pl.BlockSpec((tq, 8 * LANES), lambda bi, i: (bi * nq + i, 0)),
                  pl.BlockSpec((l, 8 * LANES), lambda bi, i: (bi, 0)),
                  pl.BlockSpec((l, MLA_R), lambda bi, i: (bi, 0)),
                  _layer((MLA_H, MLA_R, LANES), e)],
        out_specs=pl.BlockSpec((tq, MLA_H * MLA_DV), lambda bi, i: (bi * nq + i, 0)),
        out_shape=jax.ShapeDtypeStruct((b * l, MLA_H * MLA_DV), BF16),
        scratch_shapes=[stat, stat, stat],
        compiler_params=_params("parallel", "arbitrary"),
        name="mla_prompt",
    )(qcat, kcat, vlat, wuv)


MLA_PG = 16


def _start_all(copies):
    for cp in copies:
        cp.start()


def _wait_all(copies):
    for cp in copies:
        cp.wait()


def _mla_s_body(pt_ref, qbd_ref, qr_ref, qself_ref, kself_ref, vself_ref, wukt_ref, wuv_ref, cache_hbm,
                a_ref, buf, lhs_scr, sem, *, n_pages, e):
    b = pl.program_id(0)
    slot = lax.rem(b, 2)
    hd = MLA_H * MLA_DN

    def page_copies(seq, slot_):
        return [pltpu.make_async_copy(cache_hbm.at[e, pt_ref[seq * n_pages + p]], buf.at[slot_, p], sem.at[slot_])
                for p in range(n_pages)]

    @pl.when(b == 0)
    def _():
        _start_all(page_copies(0, 0))

    @pl.when(b + 1 < pl.num_programs(0))
    def _():
        _start_all(page_copies(b + 1, 1 - slot))

    wukt = wukt_ref[...]
    lhs_scr[0:hd, :] = wukt
    qa = _dot(qbd_ref[...].astype(BF16), wukt)
    lhs_scr[hd:hd + 16, :] = jnp.concatenate([qa, jnp.zeros_like(qa)], axis=0).astype(BF16)
    lhs = lhs_scr[...]
    qrb = qr_ref[...].astype(BF16)
    _wait_all(page_copies(b, slot))

    def partial_softmax(g):
        pgs = [buf.at[slot, g * MLA_PG + p] for p in range(MLA_PG)]
        ct = jnp.concatenate([p[0:MLA_R, :] for p in pgs], axis=1).astype(BF16)
        krt = jnp.concatenate([p[MLA_R:MLA_R + MLA_DR, :] for p in pgs], axis=1).astype(BF16)
        keys = ct.shape[1]
        both = _dot(lhs, ct)
        knt = both[0:hd, :]
        ss = jnp.sum((knt * knt).reshape(MLA_H, MLA_DN, keys), axis=1)
        rinv = lax.rsqrt(ss * (1.0 / MLA_DN) + EPS)
        s = rinv * both[hd:hd + MLA_H, :] + _dot(qrb, krt)
        m = jnp.max(s, axis=-1, keepdims=True)
        p = jnp.exp2(s - m)
        return m, jnp.sum(p, axis=-1, keepdims=True), _dot_nt(p.astype(BF16), ct)

    parts = [partial_softmax(g) for g in range(n_pages // MLA_PG)]
    s_self = jnp.sum(qself_ref[...].astype(F32) * kself_ref[...].astype(F32), axis=-1, keepdims=True)
    m_new = s_self
    for m, _, _ in parts:
        m_new = jnp.maximum(m_new, m)
    p_self = jnp.exp2(s_self - m_new)
    l = p_self
    acc = p_self * vself_ref[...].astype(F32)
    for m, l_g, acc_g in parts:
        w = jnp.exp2(m - m_new)
        l = l + w * l_g
        acc = acc + w * acc_g
    r = _dot((acc / l).astype(BF16), wuv_ref[...])
    rh = lax.broadcasted_iota(I32, r.shape, 0)
    ch = lax.broadcasted_iota(I32, r.shape, 1) // MLA_DV
    a_ref[...] = jnp.sum(jnp.where(rh == ch, r, 0.0), axis=0, keepdims=True).astype(a_ref.dtype)


def _mla_sample(page_table, cache_t, e, qbd, qr, qself, kself, vself, wukt, wuv):
    bs = qbd.shape[0]
    n_pages = page_table.shape[1]
    assert n_pages % MLA_PG == 0
    width = MLA_R + MLA_DR
    hd = MLA_H * MLA_DN
    seq3 = lambda n2, n3: pl.BlockSpec((None, n2, n3), lambda b, pt: (b, 0, 0))
    grid_spec = pltpu.PrefetchScalarGridSpec(
        num_scalar_prefetch=1,
        grid=(bs,),
        in_specs=[seq3(MLA_H, hd), seq3(MLA_H, MLA_DR), seq3(MLA_H, LANES), seq3(MLA_H, LANES),
                  seq3(1, MLA_R),
                  pl.BlockSpec((None, hd, MLA_R), lambda b, pt: (e, 0, 0)),
                  pl.BlockSpec((None, MLA_R, MLA_H * MLA_DV), lambda b, pt: (e, 0, 0)),
                  pl.BlockSpec(memory_space=pl.ANY)],
        out_specs=pl.BlockSpec((None, 1, MLA_H * MLA_DV), lambda b, pt: (b, 0, 0)),
        scratch_shapes=[pltpu.VMEM((2, n_pages, width, PAGE), F32), pltpu.VMEM((hd + 16, MLA_R), BF16),
                        pltpu.SemaphoreType.DMA((2,))],
    )
    out = pl.pallas_call(
        functools.partial(_mla_s_body, n_pages=n_pages, e=e),
        grid_spec=grid_spec,
        out_shape=jax.ShapeDtypeStruct((bs, 1, MLA_H * MLA_DV), BF16),
        compiler_params=_params("arbitrary"),
        name="mla_sample",
    )(page_table.reshape(-1), qbd, qr, qself, kself, vself, wukt, wuv, cache_t)
    return out.reshape(bs, MLA_H * MLA_DV)


HG_BS = 32
SCAN_GROUP = 4


def _hgrn_body(q_ref, k_ref, lf_ref, v_ref, g_ref, s0_ref, gain_ref, wind_ref, tri_ref,
               o_ref, sout_ref, st_scr, z_scr, *, c_len, bs, nc, grp):
    c = pl.program_id(1)
    nb = c_len // bs
    hc = HG_H * c_len
    pk = 16

    @pl.when(c == 0)
    def _():
        z_scr[...] = jnp.zeros(z_scr.shape, BF16)
        for gi in range(grp):
            for h in range(HG_H):
                st_scr[gi, h] = s0_ref[gi, h].T

    tri = tri_ref[...]
    gain = gain_ref[...]
    wind = wind_ref[...]
    row = lax.broadcasted_iota(I32, (c_len, c_len), 0)
    col = lax.broadcasted_iota(I32, (c_len, c_len), 1)
    keep = (col <= row) & ((row >= bs) == (col >= bs))
    late = lax.broadcasted_iota(I32, (c_len, HG_W), 0) >= bs
    for gi in range(grp):
        b = sum(_dot(tri, t) for t in _split3(lf_ref[gi]))
        q = q_ref[gi].astype(F32)
        k = k_ref[gi]
        v = v_ref[gi]
        b_last = b[c_len - 1:c_len, :]
        qe = (q * jnp.exp(b)).astype(BF16)
        kd = (k * jnp.exp(b_last - b)).astype(BF16)
        if nb == 2:
            ref = b[bs - 1:bs, :]
            q_off = jnp.where(late, q * jnp.exp(jnp.minimum(b - ref, 0.0)), 0.0).astype(BF16)
            k_off = jnp.where(late, 0.0, k * jnp.exp(jnp.minimum(ref - b, 0.0))).astype(BF16)
        early = []
        for h in range(HG_H):
            hs = slice(h * LANES, (h + 1) * LANES)
            st = st_scr[gi, h]
            o_h = _dot_nt(qe[:, hs], st.astype(BF16))
            a_off = _dot_nt(q_off[:, hs], k_off[:, hs]) if nb == 2 else None
            st_scr[gi, h] = jnp.exp(b_last[:, hs]) * st + _dot_tn(v[:, hs], kd[:, hs])
            early.append((o_h, a_off))
        b3, q3, k3 = (t.reshape(nb, bs, HG_W) for t in (b * LOG2E, q, k))
        for s in range(bs):
            t0 = (s // pk) * pk
            z = q3[:, t0:, :] * k3[:, s:s + 1, :] * jnp.exp2(jnp.minimum(b3[:, t0:, :] - b3[:, s:s + 1, :], 0.0))
            z = z.astype(BF16)
            for ib in range(nb):
                for h in range(HG_H):
                    r0 = gi * hc + h * c_len + ib * bs
                    z_scr[r0 + t0:r0 + bs, s * LANES:(s + 1) * LANES] = z[ib, :, h * LANES:(h + 1) * LANES]
        g = g_ref[gi].astype(F32)
        for h in range(HG_H):
            hs = slice(h * LANES, (h + 1) * LANES)
            o_h, a_off = early[h]
            r0 = gi * hc + h * c_len
            a = jnp.where(keep, _dot(z_scr[r0:r0 + c_len, :], wind), 0.0)
            if nb == 2:
                a = a + a_off
            o_h = o_h + _dot(a.astype(BF16), v[:, hs])
            o_ref[gi, :, hs] = (_rms(o_h, gain[:, hs]) * g[:, hs]).astype(o_ref.dtype)

    @pl.when(c == nc - 1)
    def _():
        for gi in range(grp):
            for h in range(HG_H):
                sout_ref[gi, h] = st_scr[gi, h].T


def _hgrn_scan(hq, hk, hlf, hi, hg, s0, gain, b, l, c_len):
    bs = min(HG_BS, c_len)
    nc = l // c_len
    nb = c_len // bs
    assert nb in (1, 2) and c_len == nb * bs
    grp = SCAN_GROUP if b % SCAN_GROUP == 0 else 1
    chunk = lambda: pl.BlockSpec((grp, c_len, HG_W), lambda bi, ci: (bi, ci, 0))
    state = lambda: pl.BlockSpec((grp, HG_H, HG_DK, HG_DV), lambda bi, ci: (bi, 0, 0, 0))
    wind = jnp.tile(jnp.repeat(jnp.eye(bs, dtype=BF16), LANES, axis=0), (1, nb))
    tri = jnp.tril(jnp.ones((c_len, c_len), BF16))
    seq = lambda t: t.reshape(b, l, HG_W)
    o, s_new = pl.pallas_call(
        functools.partial(_hgrn_body, c_len=c_len, bs=bs, nc=nc, grp=grp),
        grid=(b // grp, nc),
        in_specs=[chunk(), chunk(), chunk(), chunk(), chunk(), state(), _const((1, HG_W)),
                  _const((bs * LANES, c_len)), _const((c_len, c_len))],
        out_specs=[chunk(), state()],
        out_shape=[jax.ShapeDtypeStruct((b, l, HG_W), BF16),
                   jax.ShapeDtypeStruct((b, HG_H, HG_DK, HG_DV), F32)],
        scratch_shapes=[pltpu.VMEM((grp, HG_H, HG_DV, HG_DK), F32),
                        pltpu.VMEM((grp * HG_H * c_len, bs * LANES), BF16)],
        compiler_params=_params("parallel", "arbitrary"),
        name="hgrn_scan",
    )(seq(hq), seq(hk), seq(hlf), seq(hi), seq(hg), s0, gain, wind, tri)
    return o.reshape(b * l, HG_W), s_new


def _out_body(x_ref, a_ref, o_ref, w_ref, y_ref):
    half = a_ref.shape[1]
    y_ref[...] = x_ref[...] + _dot(a_ref[...], w_ref[0:half, :]) + _dot(o_ref[...], w_ref[half:2 * half, :])


def _out_proj(x, a, o, w, e):
    m = x.shape[0]
    tm = _row_tile(m, 512)
    half = a.shape[1]
    return pl.pallas_call(
        _out_body,
        grid=(m // tm,),
        in_specs=[_rows(tm, D_MODEL), _rows(tm, half), _rows(tm, half), _layer((2 * half, D_MODEL), e)],
        out_specs=_rows(tm, D_MODEL),
        out_shape=jax.ShapeDtypeStruct((m, D_MODEL), F32),
        compiler_params=_params("parallel"),
        name="out_proj",
    )(x, a, o, w)


OD_Q, OD_K, OD_V = 0, 8 * LANES, 12 * LANES
OD_XQK, OD_XV, OD_OG, OD_G = 16 * LANES, 20 * LANES, 24 * LANES, 28 * LANES
OD_COLS = 29 * LANES


def _odd_in_body(x_ref, g_ref, w_ref, gq_ref, gk_ref, gb_ref,
                 qpad_ref, kpad_ref, vpad_ref, xqk_ref, xv_ref, og_ref, gates_ref):
    hn = _rms(x_ref[...], g_ref[...]).astype(BF16)

    def headnorm(t, gain):
        ms = jnp.sum(t * t, axis=-1, keepdims=True) * (1.0 / MB_DH)
        return t * lax.rsqrt(ms + EPS) * gain

    gq, gk = gq_ref[...], gk_ref[...]
    for h in range(MB_H):
        t = _dot(hn, w_ref[:, OD_Q + h * LANES:OD_Q + (h + 1) * LANES])
        qpad_ref[:, h * LANES:(h + 1) * LANES] = (headnorm(t, gq) * MB_SCALE).astype(BF16)
    for h in range(MB_KVH):
        t = _dot(hn, w_ref[:, OD_K + h * LANES:OD_K + (h + 1) * LANES])
        kpad_ref[:, h * LANES:(h + 1) * LANES] = headnorm(t, gk)
    vpad_ref[...] = _dot(hn, w_ref[:, OD_V:OD_V + 4 * LANES])
    xqk_ref[...] = _dot(hn, w_ref[:, OD_XQK:OD_XQK + 4 * LANES])
    xv_ref[...] = _dot(hn, w_ref[:, OD_XV:OD_XV + 4 * LANES]).astype(xv_ref.dtype)
    og_ref[...] = _sigmoid(_dot(hn, w_ref[:, OD_OG:OD_OG + 4 * LANES])).astype(og_ref.dtype)
    gt = _dot(hn, w_ref[:, OD_G:OD_G + LANES]) + gb_ref[...]
    lane = lax.broadcasted_iota(I32, gt.shape, 1)
    logsig = jnp.minimum(gt, 0.0) - jnp.log(1.0 + jnp.exp(-jnp.abs(gt)))
    gates_ref[...] = jnp.where(lane < ML_H, gt, logsig)


def _odd_in(x, P, j, l):
    m = x.shape[0]
    tm = _row_tile(m)
    outs = [(8 * LANES, BF16), (4 * LANES, F32), (4 * LANES, F32), (2 * ML_QK, F32),
            (ML_H * ML_DV, BF16), (ML_H * ML_DV, BF16), (LANES, F32)]
    return pl.pallas_call(
        _odd_in_body,
        grid=(m // tm,),
        in_specs=[_rows(tm, D_MODEL), _layer((1, D_MODEL), l), _layer((D_MODEL, OD_COLS), j),
                  _layer((1, LANES), j), _layer((1, LANES), j), _layer((1, LANES), j)],
        out_specs=[_rows(tm, n) for n, _ in outs],
        out_shape=[jax.ShapeDtypeStruct((m, n), dt) for n, dt in outs],
        compiler_params=_params("parallel"),
        name="odd_in",
    )(x, P["norm_mix"], P["odd_w"], P["moba_gq"], P["moba_gk"], P["mlstm_gb"])


def _moba_p_body(q_ref, k_ref, v_ref, place_ref, a_ref, kmt_scr, qa_scr, m_scr, l_scr, acc_scr, *, nb):
    i = pl.program_id(1)
    blk = MB_BLOCK
    row = lax.broadcasted_iota(I32, (blk, blk), 0)
    col = lax.broadcasted_iota(I32, (blk, blk), 1)
    causal = col <= row
    lane = lax.broadcasted_iota(I32, (blk, LANES), 1)

    @pl.when(i == 0)
    def _():
        kmean = jnp.mean(k_ref[...].reshape(nb, blk, MB_KVH * LANES), axis=1)
        kmt_scr[...] = jnp.zeros(kmt_scr.shape, F32)
        hi = kmean.astype(BF16).astype(F32)
        r1 = kmean - hi
        mid = r1.astype(BF16).astype(F32)
        lo = r1 - mid
        for t, part in enumerate((hi, mid, lo)):
            for h in range(MB_H):
                g = h // MB_GRP
                r0 = (t * MB_H + h) * nb
                kmt_scr[r0:r0 + nb, h * LANES:(h + 1) * LANES] = part[:, g * LANES:(g + 1) * LANES]

    q_all = q_ref[...]
    g3 = _dot_nt(kmt_scr[...].astype(BF16), q_all)
    hn = MB_H * nb
    gate = (g3[0:hn] + g3[hn:2 * hn] + g3[2 * hn:3 * hn]).reshape(MB_H, nb, blk)
    q_all = q_all.astype(F32)
    blk_id = lax.broadcasted_iota(I32, (1, nb, blk), 1)
    cand = jnp.where(lax.broadcasted_iota(I32, (1, nb, 1), 1) < i, 1.0, 0.0)
    rank = jnp.zeros((MB_H, nb, blk), F32)
    for c in range(nb - 1):
        gc = gate[:, c:c + 1, :]
        better = (gc > gate) | ((gc == gate) & (c < blk_id))
        rank = rank + jnp.where(better, cand[:, c:c + 1, :], 0.0)
    bias = jnp.where(rank < MB_TOPK, cand, 0.0) * (-NEG) + NEG
    qa_scr[...] = (q_all + _dot_tn(bias.reshape(MB_H * nb, blk).astype(BF16), place_ref[...])).astype(BF16)

    def step(j, diag):
        start = pl.multiple_of(j * blk, blk)
        for g in range(MB_KVH):
            gs = slice(g * LANES, (g + 1) * LANES)
            kj = k_ref[pl.ds(start, blk), gs]
            if not diag:
                kj = jnp.where(lane == MB_DH + j, 1.0, kj)
            kj = kj.astype(BF16)
            vj = v_ref[pl.ds(start, blk), gs].astype(BF16)
            for h in range(g * (MB_H // MB_KVH), (g + 1) * (MB_H // MB_KVH)):
                s = _dot_nt(qa_scr[:, h * LANES:(h + 1) * LANES], kj)
                if diag:
                    s = jnp.where(causal, s, NEG)
                    m_new = jnp.broadcast_to(jnp.max(s, axis=-1, keepdims=True), (blk, LANES))
                else:
                    m_old = m_scr[h]
                    m_new = jnp.maximum(m_old, jnp.max(s, axis=-1, keepdims=True))
                    alpha = jnp.exp2(m_old - m_new)
                p = jnp.exp2(s - jnp.concatenate([m_new] * (blk // LANES), axis=-1))
                l_new = jnp.broadcast_to(jnp.sum(p, axis=-1, keepdims=True), (blk, LANES))
                acc_new = _dot(p.astype(BF16), vj)
                l_scr[h] = l_new if diag else alpha * l_scr[h] + l_new
                acc_scr[h] = acc_new if diag else alpha * acc_scr[h] + acc_new
                m_scr[h] = m_new

    def body(j, c):
        step(j, False)
        return c

    step(i, True)
    lax.fori_loop(0, i, body, 0)
    for h in range(MB_H):
        a_ref[:, h * MB_DH:(h + 1) * MB_DH] = (acc_scr[h] / l_scr[h])[:, 0:MB_DH].astype(a_ref.dtype)


def _moba_prompt(qpad, kpad, vpad, b, l):
    assert l % MB_BLOCK == 0 and l // MB_BLOCK <= LANES - MB_DH
    nb = l // MB_BLOCK
    stat = pltpu.VMEM((MB_H, MB_BLOCK, LANES), F32)
    hc = jnp.arange(MB_H * nb)
    place = (jnp.arange(MB_H * LANES)[None, :] == ((hc // nb) * LANES + MB_DH + hc % nb)[:, None]).astype(BF16)
    return pl.pallas_call(
        functools.partial(_moba_p_body, nb=nb),
        grid=(b, nb),
        in_specs=[pl.BlockSpec((MB_BLOCK, 8 * LANES), lambda bi, i: (bi * nb + i, 0)),
                  pl.BlockSpec((l, 4 * LANES), lambda bi, i: (bi, 0)),
                  pl.BlockSpec((l, 4 * LANES), lambda bi, i: (bi, 0)),
                  _const((MB_H * nb, MB_H * LANES))],
        out_specs=pl.BlockSpec((MB_BLOCK, MB_H * MB_DH), lambda bi, i: (bi * nb + i, 0)),
        out_shape=jax.ShapeDtypeStruct((b * l, MB_H * MB_DH), BF16),
        scratch_shapes=[pltpu.VMEM((3 * MB_H * nb, MB_H * LANES), F32), pltpu.VMEM((MB_BLOCK, MB_H * LANES), BF16),
                        stat, stat, stat],
        compiler_params=_params("parallel", "arbitrary"),
        name="moba_prompt",
    )(qpad, kpad, vpad, place)


def _moba_sel_body(pt_ref, qsel_ref, ck_hbm, sel_ref, buf, sem, *, n_pages, n_cand, jl):
    b = pl.program_id(0)
    slot = lax.rem(b, 2)
    ppb = MB_BLOCK // PAGE

    def page_copies(seq, slot_):
        return [pltpu.make_async_copy(ck_hbm.at[jl, pt_ref[seq * n_pages + p]], buf.at[slot_, p], sem.at[slot_])
                for p in range(n_pages)]

    @pl.when(b == 0)
    def _():
        _start_all(page_copies(0, 0))

    @pl.when(b + 1 < pl.num_programs(0))
    def _():
        _start_all(page_copies(b + 1, 1 - slot))

    _wait_all(page_copies(b, slot))
    lane = lax.broadcasted_iota(I32, (MB_KVH * MB_DH, LANES), 1)
    kmt = jnp.zeros((MB_KVH * MB_DH, LANES), F32)
    for c in range(n_cand):
        tot = buf[slot, c * ppb]
        for p in range(1, ppb):
            tot = tot + buf[slot, c * ppb + p]
        mean = jnp.sum(tot, axis=-1, keepdims=True) * (1.0 / MB_BLOCK)
        kmt = jnp.where(lane == c, mean, kmt)
    gate = jnp.dot(qsel_ref[...], kmt, precision=HI, preferred_element_type=F32)
    lane_g = lax.broadcasted_iota(I32, gate.shape, 1)
    rank = jnp.zeros(gate.shape, I32)
    for c in range(n_cand):
        gc = gate[:, c:c + 1]
        better = (gc > gate) | ((gc == gate) & (c < lane_g))
        rank = rank + jnp.where(better, 1, 0)
    out_lane = lax.broadcasted_iota(I32, sel_ref.shape, 1)
    out = jnp.zeros(sel_ref.shape, I32)
    for r in range(MB_TOPK):
        idx = jnp.sum(jnp.where((rank == r) & (lane_g < n_cand), lane_g, 0), axis=-1, keepdims=True)
        out = jnp.where(out_lane == r, idx, out)
    sel_ref[...] = out


def _moba_select(page_table, cache_kt, jl, qsel):
    bs = qsel.shape[0]
    n_pages = page_table.shape[1]
    n_cand = n_pages * PAGE // MB_BLOCK
    assert MB_TOPK <= n_cand <= LANES and n_pages * PAGE % MB_BLOCK == 0
    width = MB_KVH * MB_DH
    grid_spec = pltpu.PrefetchScalarGridSpec(
        num_scalar_prefetch=1,
        grid=(bs,),
        in_specs=[pl.BlockSpec((None, MB_H, width), lambda b, pt: (b, 0, 0)), pl.BlockSpec(memory_space=pl.ANY)],
        out_specs=pl.BlockSpec((None, MB_H, LANES), lambda b, pt: (b, 0, 0)),
        scratch_shapes=[pltpu.VMEM((2, n_pages, width, PAGE), F32), pltpu.SemaphoreType.DMA((2,))],
    )
    sel = pl.pallas_call(
        functools.partial(_moba_sel_body, n_pages=n_pages, n_cand=n_cand, jl=jl),
        grid_spec=grid_spec,
        out_shape=jax.ShapeDtypeStruct((bs, MB_H, LANES), I32),
        compiler_params=_params("arbitrary"),
        name="moba_select",
    )(page_table.reshape(-1), qsel, cache_kt)
    return sel[:, :, :MB_TOPK]


MB_GRP = MB_H // MB_KVH


def _moba_s_body(pt_ref, sel_ref, q_ref, kself_ref, vself_ref, ck_hbm, cv_hbm, o_ref, kbuf, vbuf, sem,
                 *, n_pages, jl):
    b = pl.program_id(0)
    slot = lax.rem(b, 2)
    ppb = MB_BLOCK // PAGE
    n = MB_TOPK * ppb

    def tile_copies(seq, slot_):
        out = []
        for h in range(MB_H):
            rows = pl.ds((h // MB_GRP) * MB_DH, MB_DH)
            for r in range(MB_TOPK):
                blk = sel_ref[(seq * MB_H + h) * MB_TOPK + r]
                for p in range(ppb):
                    page = pt_ref[seq * n_pages + blk * ppb + p]
                    i = h * n + r * ppb + p
                    out.append(pltpu.make_async_copy(ck_hbm.at[jl, page, rows], kbuf.at[slot_, i], sem.at[0, slot_]))
                    out.append(pltpu.make_async_copy(cv_hbm.at[jl, page, rows], vbuf.at[slot_, i], sem.at[1, slot_]))
        return out

    @pl.when(b == 0)
    def _():
        _start_all(tile_copies(0, 0))

    @pl.when(b + 1 < pl.num_programs(0))
    def _():
        _start_all(tile_copies(b + 1, 1 - slot))

    _wait_all(tile_copies(b, slot))
    for h in range(MB_H):
        g = h // MB_GRP
        q1 = q_ref[h:h + 1, :]
        kself, vself = kself_ref[g:g + 1, :], vself_ref[g:g + 1, :]
        q = jnp.broadcast_to(q1, (8, MB_DH)).astype(BF16)
        scores = [_dot(q, kbuf[slot, h * n + i].astype(BF16)) for i in range(n)]
        s_self = jnp.sum(q1 * kself, axis=-1, keepdims=True)
        m = s_self
        for s in scores:
            m = jnp.maximum(m, jnp.max(s, axis=-1, keepdims=True))
        p_self = jnp.exp2(s_self - m)
        l = p_self
        acc = p_self * vself
        for i, s in enumerate(scores):
            p = jnp.exp2(s - m)
            l = l + jnp.sum(p, axis=-1, keepdims=True)
            acc = acc + _dot_nt(p.astype(BF16), vbuf[slot, h * n + i].astype(BF16))
        o_ref[h:h + 1, :] = (acc / l)[0:1, :]


def _moba_sample(page_table, sel, cache_kt, cache_vt, jl, q, kself, vself):
    bs = q.shape[0]
    n_pages = page_table.shape[1]
    n_tiles = MB_H * MB_TOPK * (MB_BLOCK // PAGE)
    per_seq = lambda nh: pl.BlockSpec((None, nh, MB_DH), lambda b, pt, sl: (b, 0, 0))
    grid_spec = pltpu.PrefetchScalarGridSpec(
        num_scalar_prefetch=2,
        grid=(bs,),
        in_specs=[per_seq(MB_H), per_seq(MB_KVH), per_seq(MB_KVH),
                  pl.BlockSpec(memory_space=pl.ANY), pl.BlockSpec(memory_space=pl.ANY)],
        out_specs=per_seq(MB_H),
        scratch_shapes=[pltpu.VMEM((2, n_tiles, MB_DH, PAGE), F32), pltpu.VMEM((2, n_tiles, MB_DH, PAGE), F32),
                        pltpu.SemaphoreType.DMA((2, 2))],
    )
    return pl.pallas_call(
        functools.partial(_moba_s_body, n_pages=n_pages, jl=jl),
        grid_spec=grid_spec,
        out_shape=jax.ShapeDtypeStruct((bs, MB_H, MB_DH), F32),
        compiler_params=_params("arbitrary"),
        name="moba_sample",
    )(page_table.reshape(-1), sel.reshape(-1), q, kself, vself, cache_kt, cache_vt)


def _mlstm_body(xqk_ref, xv_ref, og_ref, gcol_ref, grow_ref, cw_ref, cb_ref, prev_ref, c0_ref, m0_ref,
                gain_ref, tri_ref, triu_ref,
                h_ref, cout_ref, mout_ref, convout_ref, full_scr, c_scr, m_scr, *, c_len, nc, l_last, grp):
    c = pl.program_id(1)
    pad = 8
    w = 2 * ML_QK

    @pl.when(c == 0)
    def _():
        full_scr[:, pad - (CONV_W - 1):pad, :] = prev_ref[...]
        c_scr[...] = c0_ref[...]
        m_scr[...] = m0_ref[...]

    row = lax.broadcasted_iota(I32, (c_len, c_len), 0)
    col = lax.broadcasted_iota(I32, (c_len, c_len), 1)
    causal = col <= row
    head_of_lane = lax.broadcasted_iota(I32, (1, ML_QK), 1) // ML_DK
    head_of_row = lax.broadcasted_iota(I32, (ML_QK, 1), 0) // ML_DK
    m_lane = lax.broadcasted_iota(I32, (1, ML_H), 1)
    one_col = jnp.where(lax.broadcasted_iota(I32, (c_len, LANES), 1) == 0, 1.0, 0.0).astype(BF16)
    gain = gain_ref[...]
    tri, triu = tri_ref[...], triu_ref[...]

    for gi in range(grp):
        full_scr[gi, pad:pad + c_len, :] = xqk_ref[gi]
        conv = cb_ref[...]
        for t in range(CONV_W):
            o = pad - (CONV_W - 1) + t
            conv = conv + full_scr[gi, o:o + c_len, :] * cw_ref[t:t + 1, :]

        @pl.when(c == nc - 1)
        def _():
            convout_ref[gi] = full_scr[gi, pad + l_last - (CONV_W - 1):pad + l_last, :]

        full_scr[gi, pad - (CONV_W - 1):pad, :] = full_scr[gi, pad + c_len - (CONV_W - 1):pad + c_len, :]

        qk = _silu(conv)
        qf = qk[:, 0:ML_QK]
        kf = qk[:, ML_QK:w] * (ML_DK ** -0.5)
        kb = kf.astype(BF16)
        gcol, grow = gcol_ref[gi], grow_ref[gi]
        f_col = jnp.dot(tri, gcol, precision=HI, preferred_element_type=F32)
        f_row = jnp.dot(grow, triu, precision=HI, preferred_element_type=F32)
        state = c_scr[gi]
        state_b = state.astype(BF16)
        m_all = m_scr[gi]
        xv = xv_ref[gi]
        og = og_ref[gi].astype(F32)
        decay_col = jnp.zeros((ML_QK, 1), F32)
        upd = jnp.zeros((ML_QK, 2 * LANES), F32)
        m_next = jnp.zeros((1, ML_H), F32)
        for h in range(ML_H):
            fc = f_col[:, ML_H + h:ML_H + h + 1]
            ic = gcol[:, h:h + 1]
            fr = f_row[ML_H + h:ML_H + h + 1, :]
            ir = grow[h:h + 1, :]
            m_prev = m_all[:, h:h + 1]
            dmat = jnp.where(causal, fc - fr + ir, NEG)
            inter = fc + m_prev
            mt = jnp.maximum(jnp.max(dmat, axis=-1, keepdims=True), inter)
            qh = jnp.where(head_of_lane == h, qf, 0.0).astype(BF16)
            wd = jnp.exp(dmat - mt) * _dot_nt(qh, kb)
            a = jnp.exp(inter - mt)
            vaug = jnp.concatenate([xv[:, h * ML_DV:(h + 1) * ML_DV], one_col], axis=-1)
            num = _dot(wd.astype(BF16), vaug) + a * _dot(qh, state_b)
            den = num[:, ML_DV:ML_DV + 1]
            hh = num[:, 0:ML_DV] / jnp.maximum(jnp.abs(den), jnp.exp(-mt))
            hs = slice(h * ML_DV, (h + 1) * ML_DV)
            h_ref[gi, :, hs] = (_rms(hh, gain[:, hs]) * og[:, hs]).astype(h_ref.dtype)
            m_new = mt[c_len - 1:c_len, :]
            f_last = fc[c_len - 1:c_len, :]
            gk = (jnp.where(head_of_lane == h, kf, 0.0) * jnp.exp(f_last - fc + ic - m_new)).astype(BF16)
            upd = upd + _dot_tn(gk, vaug)
            decay_col = decay_col + jnp.where(head_of_row == h, jnp.exp(f_last + m_prev - m_new), 0.0)
            m_next = m_next + jnp.where(m_lane == h, m_new, 0.0)
        c_scr[gi] = decay_col * state + upd
        m_scr[gi] = m_next

    @pl.when(c == nc - 1)
    def _():
        cout_ref[...] = c_scr[...]
        mout_ref[...] = m_scr[...]


def _mlstm_scan(xqk, xv, og, gcol, grow, cw, cb, prev, c0, m0, gain, j, b, l_pad, c_len, l_last):
    nc = l_pad // c_len
    w = 2 * ML_QK
    grp = SCAN_GROUP if b % SCAN_GROUP == 0 else 1
    chunk = lambda n: pl.BlockSpec((grp, c_len, n), lambda bi, ci: (bi, ci, 0))
    per_b = lambda n2, n3: pl.BlockSpec((grp, n2, n3), lambda bi, ci: (bi, 0, 0))
    tri = jnp.tril(jnp.ones((c_len, c_len), F32))
    seq = lambda t: t.reshape(b, l_pad, t.shape[-1])
    h, caug, m_new, conv_new = pl.pallas_call(
        functools.partial(_mlstm_body, c_len=c_len, nc=nc, l_last=l_last, grp=grp),
        grid=(b // grp, nc),
        in_specs=[chunk(w), chunk(ML_H * ML_DV), chunk(ML_H * ML_DV), chunk(LANES),
                  pl.BlockSpec((grp, None, 8, c_len), lambda bi, ci: (bi, ci, 0, 0)),
                  _layer((CONV_W, w), j), _layer((1, w), j), per_b(CONV_W - 1, w),
                  per_b(ML_QK, 2 * LANES), per_b(1, ML_H), _layer((1, ML_H * ML_DV), j),
                  _const((c_len, c_len)), _const((c_len, c_len))],
        out_specs=[chunk(ML_H * ML_DV), per_b(ML_QK, 2 * LANES), per_b(1, ML_H), per_b(CONV_W - 1, w)],
        out_shape=[jax.ShapeDtypeStruct((b, l_pad, ML_H * ML_DV), BF16),
                   jax.ShapeDtypeStruct((b, ML_QK, 2 * LANES), F32),
                   jax.ShapeDtypeStruct((b, 1, ML_H), F32),
                   jax.ShapeDtypeStruct((b, CONV_W - 1, w), F32)],
        scratch_shapes=[pltpu.VMEM((grp, c_len + 8, w), F32), pltpu.VMEM((grp, ML_QK, 2 * LANES), F32),
                        pltpu.VMEM((grp, 1, ML_H), F32)],
        compiler_params=_params("parallel", "arbitrary"),
        name="mlstm_scan",
    )(seq(xqk), seq(xv), seq(og), seq(gcol), grow, cw, cb, prev, c0, m0, gain, tri, tri.T)
    return h.reshape(b * l_pad, ML_H * ML_DV), caug, m_new, conv_new


def _prepare(norm_ffn1, norm_mix, norm_ffn2, ffn1_w_in, ffn1_w_out, ffn2_w_in, ffn2_w_out, even_w_in, even_w_out,
             mla_q_norm, mla_w_uq, mla_kv_norm, mla_w_uk, mla_w_uv, mla_qn_norm, mla_qr_norm, mla_kn_norm,
             mla_kr_norm, hgrn_lb_raw, hgrn_out_norm, odd_w_in, odd_w_out, moba_q_norm, moba_k_norm,
             mlstm_conv_w, mlstm_conv_b, mlstm_b_i, mlstm_b_f, mlstm_out_norm):
    ne, no = even_w_in.shape[0], odd_w_in.shape[0]
    row = lambda t: t[:, None, :].astype(F32)
    P = dict(norm_ffn1=row(norm_ffn1), norm_mix=row(norm_mix), norm_ffn2=row(norm_ffn2),
             ffn1_w_in=ffn1_w_in.astype(BF16), ffn1_w_out=ffn1_w_out.astype(BF16),
             ffn2_w_in=ffn2_w_in.astype(BF16), ffn2_w_out=ffn2_w_out.astype(BF16),
             even_w_out=even_w_out.astype(BF16), odd_w_out=odd_w_out.astype(BF16))
    w = even_w_in
    kr = jnp.zeros((ne, D_MODEL, LANES), F32).at[:, :, 64:96].set(w[:, :, 384:416])
    P["even_w"] = jnp.concatenate([w[:, :, 0:384], kr, w[:, :, 416:]], axis=-1).astype(BF16)
    pad_heads = lambda t, n: jnp.pad(t, ((0, 0), (0, 0), (0, 0), (0, LANES - n))).reshape(ne, t.shape[1], 8 * LANES)
    P["mla_wuq"] = pad_heads(mla_w_uq, MLA_DN + MLA_DR).astype(BF16)
    P["mla_wuk"] = pad_heads(mla_w_uk, MLA_DN).astype(BF16)
    P["mla_wukt"] = jnp.swapaxes(mla_w_uk.reshape(ne, MLA_R, MLA_H * MLA_DN), 1, 2).astype(BF16)
    P["mla_wuv_flat"] = mla_w_uv.reshape(ne, MLA_R, MLA_H * MLA_DV).astype(BF16)
    wuv = jnp.transpose(mla_w_uv, (0, 2, 1, 3))
    odd_head = (jnp.arange(MLA_H) % 2 == 1)[None, :, None, None]
    P["mla_wuv"] = jnp.where(odd_head, jnp.pad(wuv, ((0, 0),) * 3 + ((MLA_DV, 0),)),
                             jnp.pad(wuv, ((0, 0),) * 3 + ((0, MLA_DV),))).astype(BF16)
    z32 = jnp.zeros((ne, LANES - MLA_DN - MLA_DR), F32)
    P["mla_q_norm"], P["mla_kv_norm"] = row(mla_q_norm), row(mla_kv_norm)
    P["mla_gq"] = row(jnp.concatenate([mla_qn_norm, mla_qr_norm, z32], axis=-1))
    P["mla_gk"] = row(jnp.concatenate([mla_kn_norm, jnp.zeros((ne, LANES - MLA_DN), F32)], axis=-1))
    P["mla_gkr"] = row(jnp.concatenate([jnp.zeros((ne, MLA_DN), F32), mla_kr_norm, z32], axis=-1))
    P["mla_kn_gain"] = mla_kn_norm.astype(F32)
    seg = jnp.arange(LANES)
    seg_id = jnp.where(seg < MLA_DN, 0, jnp.where(seg < MLA_DN + MLA_DR, 1, 2 + seg))
    seg_len = jnp.where(seg < MLA_DN, MLA_DN, MLA_DR).astype(F32)
    P["bseg"] = jnp.where(seg_id[:, None] == seg_id[None, :], 1.0 / seg_len[None, :], 0.0).astype(BF16)
    p_lb = jax.nn.softmax(hgrn_lb_raw.astype(F32), axis=0)
    lb = jnp.maximum(jnp.cumsum(p_lb, axis=0) - p_lb[0], 0.0)
    P["hg_loglb"], P["hg_log1mlb"], P["hg_1mlb"] = row(jnp.log(lb)), row(jnp.log1p(-lb)), row(1.0 - lb)
    P["hg_gain"] = row(jnp.tile(hgrn_out_norm, (1, HG_H)))
    w = odd_w_in
    c0 = MB_H * MB_DH
    c1 = c0 + MB_KVH * MB_DH
    c2 = c1 + MB_KVH * MB_DH
    c3 = c2 + 2 * ML_QK
    c4 = c3 + ML_H * ML_DV
    c5 = c4 + 2 * ML_H
    pad_h = lambda t, nh: jnp.pad(t.reshape(no, D_MODEL, nh, MB_DH),
                                  ((0, 0), (0, 0), (0, 0), (0, LANES - MB_DH))).reshape(no, D_MODEL, nh * LANES)
    gates = jnp.pad(w[:, :, c4:c5], ((0, 0), (0, 0), (0, LANES - 2 * ML_H)))
    P["odd_w"] = jnp.concatenate([pad_h(w[:, :, 0:c0], MB_H), pad_h(w[:, :, c0:c1], MB_KVH),
                                  pad_h(w[:, :, c1:c2], MB_KVH), w[:, :, c2:c4], w[:, :, c5:], gates],
                                 axis=-1).astype(BF16)
    zpad = jnp.zeros((no, LANES - MB_DH), F32)
    P["moba_gq"] = row(jnp.concatenate([moba_q_norm, zpad], axis=-1))
    P["moba_gk"] = row(jnp.concatenate([moba_k_norm, zpad], axis=-1))
    P["mlstm_gb"] = row(jnp.concatenate([mlstm_b_i, mlstm_b_f, jnp.zeros((no, LANES - 2 * ML_H), F32)], axis=-1))
    P["mlstm_cw"] = mlstm_conv_w.astype(F32)
    P["mlstm_cb"] = row(mlstm_conv_b)
    P["mlstm_gain"] = row(jnp.tile(mlstm_out_norm, (1, ML_H)))
    return P


def _rope_tables(pos, n_rows):
    half = MLA_DR // 2
    freqs = ROPE_THETA ** (-jnp.arange(half, dtype=F32) / half)
    ang = pos.astype(F32)[:, None] * freqs[None, :]
    cos, sin = jnp.cos(ang), jnp.sin(ang)
    n = pos.shape[0]
    z = lambda k: jnp.zeros((n, k), F32)
    cos_t = jnp.concatenate([jnp.ones((n, MLA_DN), F32), cos, cos, z(LANES - MLA_DN - MLA_DR)], axis=-1)
    sin_a = jnp.concatenate([z(MLA_DN + half), sin, z(LANES - MLA_DN - MLA_DR)], axis=-1)
    sin_b = jnp.concatenate([z(MLA_DN), -sin, z(LANES - MLA_DN - half)], axis=-1)
    rep = n_rows // n
    return tuple(jnp.tile(t, (rep, 1)) for t in (cos_t, sin_a, sin_b))


def _pad_time(t, b, l, l_pad, value=0.0):
    if l_pad == l:
        return t
    n = t.shape[-1]
    fill = jnp.broadcast_to(jnp.asarray(value, t.dtype), (b, l_pad - l, n))
    return jnp.concatenate([t.reshape(b, l, n), fill], axis=1).reshape(b * l_pad, n)


def _even_layer(x, P, e, l, b, seq, tabs, ctx):
    qcat, kcat, rows, vlat, hq, hk, hlf, hi, hg = _even_in(x, P, e, l, tabs)
    if ctx is None:
        a = _mla_prompt(qcat, kcat, vlat, P["mla_wuv"], e, b, seq)
        s0 = jnp.zeros((b, HG_H, HG_DK, HG_DV), F32)
        c_len, l_pad = min(64, seq), seq
    else:
        q3 = qcat.reshape(b, MLA_H, LANES).astype(F32)
        eye = jnp.eye(MLA_H, dtype=F32)
        qbd = jnp.einsum("bhd,hg->bhgd", q3[:, :, :MLA_DN] * P["mla_kn_gain"][e][None, None, :], eye)
        a = _mla_sample(ctx["page_table"], ctx["cache_mla_t"], e, qbd.reshape(b, MLA_H, MLA_H * MLA_DN),
                        q3[:, :, MLA_DN:MLA_DN + MLA_DR], qcat.reshape(b, MLA_H, LANES),
                        kcat.reshape(b, MLA_H, LANES), vlat.reshape(b, 1, MLA_R),
                        P["mla_wukt"], P["mla_wuv_flat"])
        s0 = ctx["state_hgrn"][e].astype(F32)
        c_len = l_pad = 16
    pad = lambda t: _pad_time(t, b, seq, l_pad)
    o, s_new = _hgrn_scan(pad(hq), pad(hk), pad(hlf), pad(hi), pad(hg), s0, P["hg_gain"][e], b, l_pad, c_len)
    if l_pad != seq:
        o = o.reshape(b, l_pad, HG_W)[:, :seq].reshape(b * seq, HG_W)
    y = _out_proj(x, a, o, P["even_w_out"], e)
    return y, rows.reshape(b, seq, MLA_R + MLA_DR), s_new


def _odd_layer(x, P, j, l, b, seq, ctx):
    qpad, kpad, vpad, xqk, xv, og, gates = _odd_in(x, P, j, l)
    unpad = lambda t: t.reshape(b, seq, MB_KVH, LANES)[..., :MB_DH]
    k_new, v_new = unpad(kpad), unpad(vpad)
    if ctx is None:
        a = _moba_prompt(qpad, kpad, vpad, b, seq)
        prev = jnp.zeros((b, CONV_W - 1, 2 * ML_QK), F32)
        caug0 = jnp.zeros((b, ML_QK, 2 * LANES), F32)
        m0 = jnp.zeros((b, 1, ML_H), F32)
        c_len, l_pad = min(64, seq), seq
    else:
        assert seq == 1
        pt = ctx["page_table"]
        qh = qpad.reshape(b, MB_H, LANES)[:, :, :MB_DH].astype(F32)
        on_kv = jax.nn.one_hot(jnp.arange(MB_H) // MB_GRP, MB_KVH, dtype=F32)
        qsel = jnp.einsum("bhd,hg->bhgd", qh, on_kv).reshape(b, MB_H, MB_KVH * MB_DH)
        sel = _moba_select(pt, ctx["cache_moba_kt"], j, qsel)
        a = _moba_sample(pt, sel, ctx["cache_moba_kt"], ctx["cache_moba_vt"], j, qh,
                         k_new.reshape(b, MB_KVH, MB_DH), v_new.reshape(b, MB_KVH, MB_DH))
        a = a.reshape(b, MB_H * MB_DH).astype(BF16)
        prev = ctx["state_mlstm_conv"][j].astype(F32)
        caug0 = jnp.concatenate([ctx["state_mlstm_c"][j].astype(F32),
                                 ctx["state_mlstm_n"][j].astype(F32)[..., None],
                                 jnp.zeros((b, ML_H, ML_DK, LANES - 1), F32)], axis=-1).reshape(b, ML_QK, 2 * LANES)
        m0 = ctx["state_mlstm_m"][j].astype(F32).reshape(b, 1, ML_H)
        c_len = l_pad = 16
    nc = l_pad // c_len
    gate_fill = jnp.where(jnp.arange(LANES) < ML_H, NEG, 0.0)
    gcol = _pad_time(gates, b, seq, l_pad, gate_fill)
    grow = jnp.swapaxes(gcol[:, :8].reshape(b, nc, c_len, 8), 2, 3)
    pad = lambda t: _pad_time(t, b, seq, l_pad)
    l_last = seq - (nc - 1) * c_len
    h, caug, m_new, conv_new = _mlstm_scan(pad(xqk), pad(xv), pad(og), gcol, grow, P["mlstm_cw"], P["mlstm_cb"],
                                           prev, caug0, m0, P["mlstm_gain"], j, b, l_pad, c_len, l_last)
    if l_pad != seq:
        h = h.reshape(b, l_pad, ML_H * ML_DV)[:, :seq].reshape(b * seq, ML_H * ML_DV)
    y = _out_proj(x, a, h, P["odd_w_out"], j)
    caug = caug.reshape(b, ML_H, ML_DK, 2 * LANES)
    return y, k_new, v_new, caug[..., :ML_DV], caug[..., ML_DV], m_new.reshape(b, ML_H), conv_new


def _trunk(x, pos, P, ctx):
    b, seq, _ = x.shape
    m = b * seq
    x = x.reshape(m, D_MODEL).astype(F32)
    depth = P["norm_mix"].shape[0]
    tabs = _rope_tables(pos, max(_row_tile(m), seq))
    new = {k: [] for k in ("mla", "hgrn", "moba_k", "moba_v", "mlstm_c", "mlstm_n", "mlstm_m", "mlstm_conv")}
    for l in range(depth):
        x = _ffn(x, P["norm_ffn1"], P["ffn1_w_in"], P["ffn1_w_out"], l)
        if l % 2 == 0:
            x, rows, s = _even_layer(x, P, l // 2, l, b, seq, tabs, ctx)
            new["mla"].append(rows)
            new["hgrn"].append(s)
        else:
            x, k, v, c, n, mm, cv = _odd_layer(x, P, l // 2, l, b, seq, ctx)
            for name, val in zip(("moba_k", "moba_v", "mlstm_c", "mlstm_n", "mlstm_m", "mlstm_conv"),
                                 (k, v, c, n, mm, cv)):
                new[name].append(val)
        x = _ffn(x, P["norm_ffn2"], P["ffn2_w_in"], P["ffn2_w_out"], l)
    return x.reshape(b, seq, D_MODEL), {k: jnp.stack(v).astype(F32) for k, v in new.items()}


def kernel(x_prompt, x_sample, cache_mla, state_hgrn, cache_moba_k, cache_moba_v, state_mlstm_c, state_mlstm_n, state_mlstm_m, state_mlstm_conv, page_table, norm_ffn1, norm_mix, norm_ffn2, ffn1_w_in, ffn1_w_out, ffn2_w_in, ffn2_w_out, even_w_in, even_w_out, mla_q_norm, mla_w_uq, mla_kv_norm, mla_w_uk, mla_w_uv, mla_qn_norm, mla_qr_norm, mla_kn_norm, mla_kr_norm, hgrn_lb_raw, hgrn_out_norm, odd_w_in, odd_w_out, moba_q_norm, moba_k_norm, mlstm_conv_w, mlstm_conv_b, mlstm_b_i, mlstm_b_f, mlstm_out_norm):
    P = _prepare(norm_ffn1, norm_mix, norm_ffn2, ffn1_w_in, ffn1_w_out, ffn2_w_in, ffn2_w_out, even_w_in,
                 even_w_out, mla_q_norm, mla_w_uq, mla_kv_norm, mla_w_uk, mla_w_uv, mla_qn_norm, mla_qr_norm,
                 mla_kn_norm, mla_kr_norm, hgrn_lb_raw, hgrn_out_norm, odd_w_in, odd_w_out, moba_q_norm,
                 moba_k_norm, mlstm_conv_w, mlstm_conv_b, mlstm_b_i, mlstm_b_f, mlstm_out_norm)
    n_pool = cache_moba_k.shape[1]
    page_t = lambda c: jnp.transpose(c, (0, 1, 3, 4, 2)).reshape(-1, n_pool, MB_KVH * MB_DH, PAGE)
    ctx = dict(page_table=page_table.astype(I32), cache_mla_t=jnp.swapaxes(cache_mla, 2, 3), state_hgrn=state_hgrn,
               cache_moba_kt=page_t(cache_moba_k), cache_moba_vt=page_t(cache_moba_v), state_mlstm_c=state_mlstm_c,
               state_mlstm_n=state_mlstm_n, state_mlstm_m=state_mlstm_m, state_mlstm_conv=state_mlstm_conv)
    y_p, sp = _trunk(x_prompt, jnp.arange(x_prompt.shape[1]), P, None)
    past_len = page_table.shape[1] * PAGE
    y_s, ss = _trunk(x_sample, past_len + jnp.arange(x_sample.shape[1]), P, ctx)
    names = ("mla", "hgrn", "moba_k", "moba_v", "mlstm_c", "mlstm_n", "mlstm_m", "mlstm_conv")
    out = [y_p, y_s]
    for name in names:
        out += [sp[name], ss[name]]
    return tuple(out)
```

```python
import functools

import jax
import jax.numpy as jnp
from jax import lax
from jax.experimental import pallas as pl
from jax.experimental.pallas import tpu as pltpu

F32, BF16, I32 = jnp.float32, jnp.bfloat16, jnp.int32
HI = lax.Precision.HIGHEST
EPS = 1e-6
NEG = -1e30
LOG2E = 1.4426950408889634

D_MODEL = 1024
FFN_HIDDEN = 2048
PAGE = 128
MLA_H, MLA_QR, MLA_R, MLA_DN, MLA_DR, MLA_DV = 8, 256, 128, 64, 32, 64
ROPE_THETA = 10000.0
MLA_SCALE = (MLA_DN + MLA_DR) ** -0.5 * LOG2E
HG_H, HG_DK, HG_DV = 4, 128, 128
HG_W = HG_H * HG_DK
MB_H, MB_KVH, MB_DH, MB_BLOCK, MB_TOPK = 8, 4, 64, 256, 3
MB_SCALE = MB_DH ** -0.5 * LOG2E
ML_H, ML_DK, ML_DV, CONV_W = 4, 64, 128, 4
ML_QK = ML_H * ML_DK

LANES = 128
VMEM_LIMIT = 56 << 20


def _dot(a, b):
    return jnp.dot(a, b, preferred_element_type=F32)


def _dot_nt(a, b):
    return lax.dot_general(a, b, (((1,), (1,)), ((), ())), preferred_element_type=F32)


def _dot_tn(a, b):
    return lax.dot_general(a, b, (((0,), (0,)), ((), ())), preferred_element_type=F32)


def _split3(x):
    hi = x.astype(BF16)
    r = x - hi.astype(F32)
    mid = r.astype(BF16)
    return hi, mid, (r - mid.astype(F32)).astype(BF16)


def _rms(x, g):
    return x * lax.rsqrt(jnp.mean(x * x, axis=-1, keepdims=True) + EPS) * g


def _sigmoid(x):
    return 1.0 / (1.0 + jnp.exp(-x))


def _silu(x):
    return x * _sigmoid(x)


def _rows(tm, n):
    return pl.BlockSpec((tm, n), lambda i: (i, 0))


def _layer(shape, l):
    zeros = (0,) * len(shape)
    return pl.BlockSpec((None,) + tuple(shape), lambda *_: (l,) + zeros)


def _const(shape):
    zeros = (0,) * len(shape)
    return pl.BlockSpec(tuple(shape), lambda *_: zeros)


def _params(*sem):
    return pltpu.CompilerParams(dimension_semantics=sem, vmem_limit_bytes=VMEM_LIMIT)


def _row_tile(m, cap=256):
    return min(cap, m)


def _ffn_body(x_ref, g_ref, win_ref, wout_ref, o_ref, *, fc):
    x = x_ref[...]
    hn = _rms(x, g_ref[...]).astype(BF16)
    acc = jnp.zeros_like(x)
    for c in range(FFN_HIDDEN // fc):
        a = _dot(hn, win_ref[:, c * fc:(c + 1) * fc])
        b = _dot(hn, win_ref[:, FFN_HIDDEN + c * fc:FFN_HIDDEN + (c + 1) * fc])
        acc = acc + _dot((_silu(a) * b).astype(BF16), wout_ref[c * fc:(c + 1) * fc, :])
    o_ref[...] = x + 0.5 * acc


def _ffn(x, g, w_in, w_out, l):
    m = x.shape[0]
    tm = _row_tile(m, 512)
    return pl.pallas_call(
        functools.partial(_ffn_body, fc=512),
        grid=(m // tm,),
        in_specs=[_rows(tm, D_MODEL), _layer((1, D_MODEL), l),
                  _layer((D_MODEL, 2 * FFN_HIDDEN), l), _layer((FFN_HIDDEN, D_MODEL), l)],
        out_specs=_rows(tm, D_MODEL),
        out_shape=jax.ShapeDtypeStruct((m, D_MODEL), F32),
        compiler_params=_params("parallel"),
        name="ffn",
    )(x, g, w_in, w_out)


EV_COLS = 256 + 128 + 128 + 4 * HG_W


def _even_in_body(x_ref, g_ref, w_ref, qg_ref, wuq_ref, gq_ref, kvg_ref, wuk_ref, gk_ref, gkr_ref, bseg_ref,
                  cos_ref, sa_ref, sb_ref, llb_ref, l1m_ref, omlb_ref,
                  qcat_ref, kcat_ref, rows_ref, vlat_ref, hq_ref, hk_ref, hlf_ref, hi_ref, hg_ref):
    hn = _rms(x_ref[...], g_ref[...]).astype(BF16)
    cos_t, sin_a, sin_b = cos_ref[...], sa_ref[...], sb_ref[...]
    bseg = bseg_ref[...]

    def rope(t):
        return t * cos_t + pltpu.roll(t, 16, 1) * sin_a + pltpu.roll(t, LANES - 16, 1) * sin_b

    def segnorm(t, gain):
        ms = _dot((t * t).astype(BF16), bseg)
        return t * lax.rsqrt(ms + EPS) * gain

    ql = _dot(hn, w_ref[:, 0:256])
    qh = _dot(_rms(ql, qg_ref[...]).astype(BF16), wuq_ref[...])
    gq = gq_ref[...]
    for h in range(MLA_H):
        t = segnorm(qh[:, h * LANES:(h + 1) * LANES], gq)
        qcat_ref[:, h * LANES:(h + 1) * LANES] = (rope(t) * MLA_SCALE).astype(BF16)

    cn = _rms(_dot(hn, w_ref[:, 256:384]), kvg_ref[...])
    krl = _dot(hn, w_ref[:, 384:512])
    ms = jnp.sum(krl * krl, axis=-1, keepdims=True) * (1.0 / MLA_DR)
    krr = rope(krl * lax.rsqrt(ms + EPS) * gkr_ref[...])
    rows_ref[:, 0:MLA_R] = cn
    rows_ref[:, MLA_R:MLA_R + MLA_DR] = pltpu.roll(krr, 64, 1)[:, 0:MLA_DR]
    cnb = cn.astype(BF16)
    vlat_ref[...] = cnb
    kn = _dot(cnb, wuk_ref[...])
    gk = gk_ref[...]
    for h in range(MLA_H):
        t = segnorm(kn[:, h * LANES:(h + 1) * LANES], gk)
        kcat_ref[:, h * LANES:(h + 1) * LANES] = (t + krr).astype(BF16)

    o = 512
    hq_ref[...] = _silu(_dot(hn, w_ref[:, o:o + HG_W])).astype(hq_ref.dtype)
    hf = _dot(hn, w_ref[:, o + HG_W:o + 2 * HG_W])
    logsig = jnp.minimum(hf, 0.0) - jnp.log(1.0 + jnp.exp(-jnp.abs(hf)))
    aa, bb = llb_ref[...], l1m_ref[...] + logsig
    hlf_ref[...] = jnp.maximum(aa, bb) + jnp.log(1.0 + jnp.exp(-jnp.abs(aa - bb)))
    hk_ref[...] = omlb_ref[...] * _sigmoid(-hf)
    hi_ref[...] = _dot(hn, w_ref[:, o + 2 * HG_W:o + 3 * HG_W]).astype(hi_ref.dtype)
    hg_ref[...] = _silu(_dot(hn, w_ref[:, o + 3 * HG_W:o + 4 * HG_W])).astype(hg_ref.dtype)


def _even_in(x, P, e, l, tabs):
    m = x.shape[0]
    tm = _row_tile(m)
    cos_t, sin_a, sin_b = tabs
    nt = cos_t.shape[0] // tm
    tab = pl.BlockSpec((tm, LANES), lambda i: (i % nt, 0))
    outs = [(8 * LANES, BF16), (8 * LANES, BF16), (MLA_R + MLA_DR, F32), (MLA_R, BF16),
            (HG_W, BF16), (HG_W, F32), (HG_W, F32), (HG_W, BF16), (HG_W, BF16)]
    return pl.pallas_call(
        _even_in_body,
        grid=(m // tm,),
        in_specs=[_rows(tm, D_MODEL), _layer((1, D_MODEL), l), _layer((D_MODEL, EV_COLS), e),
                  _layer((1, MLA_QR), e), _layer((MLA_QR, 8 * LANES), e), _layer((1, LANES), e),
                  _layer((1, MLA_R), e), _layer((MLA_R, 8 * LANES), e), _layer((1, LANES), e),
                  _layer((1, LANES), e), _const((LANES, LANES)), tab, tab, tab,
                  _layer((1, HG_W), e), _layer((1, HG_W), e), _layer((1, HG_W), e)],
        out_specs=[_rows(tm, n) for n, _ in outs],
        out_shape=[jax.ShapeDtypeStruct((m, n), dt) for n, dt in outs],
        compiler_params=_params("parallel"),
        name="even_in",
    )(x, P["norm_mix"], P["even_w"], P["mla_q_norm"], P["mla_wuq"], P["mla_gq"], P["mla_kv_norm"],
      P["mla_wuk"], P["mla_gk"], P["mla_gkr"], P["bseg"], cos_t, sin_a, sin_b,
      P["hg_loglb"], P["hg_log1mlb"], P["hg_1mlb"])


def _mla_p_body(q_ref, k_ref, v_ref, wuv_ref, a_ref, m_scr, l_scr, acc_scr, *, tq):
    i = pl.program_id(1)
    row = lax.broadcasted_iota(I32, (tq, tq), 0)
    col = lax.broadcasted_iota(I32, (tq, tq), 1)
    causal = col <= row

    def step(j, diag):
        start = pl.multiple_of(j * tq, tq)
        v = v_ref[pl.ds(start, tq), :]
        for h in range(MLA_H):
            q = q_ref[:, h * LANES:(h + 1) * LANES]
            k = k_ref[pl.ds(start, tq), h * LANES:(h + 1) * LANES]
            s = _dot_nt(q, k)
            if diag:
                s = jnp.where(causal, s, NEG)
                m_new = jnp.broadcast_to(jnp.max(s, axis=-1, keepdims=True), (tq, LANES))
            else:
                m_old = m_scr[h]
                m_new = jnp.maximum(m_old, jnp.max(s, axis=-1, keepdims=True))
                alpha = jnp.exp2(m_old - m_new)
            p = jnp.exp2(s - jnp.concatenate([m_new] * (tq // LANES), axis=-1))
            l_new = jnp.broadcast_to(jnp.sum(p, axis=-1, keepdims=True), (tq, LANES))
            acc_new = _dot(p.astype(BF16), v)
            l_scr[h] = l_new if diag else alpha * l_scr[h] + l_new
            acc_scr[h] = acc_new if diag else alpha * acc_scr[h] + acc_new
            m_scr[h] = m_new

    def body(j, c):
        step(j, False)
        return c

    step(i, True)
    lax.fori_loop(0, i, body, 0)
    for hp in range(MLA_H // 2):
        pair = [_dot((acc_scr[h] / l_scr[h]).astype(BF16), wuv_ref[h]) for h in (2 * hp, 2 * hp + 1)]
        a_ref[:, hp * LANES:(hp + 1) * LANES] = (pair[0] + pair[1]).astype(a_ref.dtype)


def _mla_prompt(qcat, kcat, vlat, wuv, e, b, l):
    tq = min(256, l)
    assert tq % LANES == 0 and l % tq == 0
    nq = l // tq
    stat = pltpu.VMEM((MLA_H, tq, LANES), F32)
    return pl.pallas_call(
        functools.partial(_mla_p_body, tq=tq),
        grid=(b, nq),
        in_specs=[pl.BlockSpec((tq, 8 * LANES), lambda bi, i: (bi * nq + i, 0)),
                  pl.BlockSpec((l, 8 * LANES), lambda bi, i: (bi, 0)),
                  pl.BlockSpec((l, MLA_R), lambda bi, i: (bi, 0)),
                  _layer((MLA_H, MLA_R, LANES), e)],
        out_specs=pl.BlockSpec((tq, MLA_H * MLA_DV), lambda bi, i: (bi * nq + i, 0)),
        out_shape=jax.ShapeDtypeStruct((b * l, MLA_H * MLA_DV), BF16),
        scratch_shapes=[stat, stat, stat],
        compiler_params=_params("parallel", "arbitrary"),
        name="mla_prompt",
    )(qcat, kcat, vlat, wuv)


MLA_PG = 16


def _start_all(copies):
    for cp in copies:
        cp.start()


def _wait_all(copies):
    for cp in copies:
        cp.wait()


def _mla_s_body(pt_ref, qbd_ref, qr_ref, qself_ref, kself_ref, vself_ref, wukt_ref, wuv_ref, cache_hbm,
                a_ref, buf, lhs_scr, sem, *, n_pages, e):
    b = pl.program_id(0)
    slot = lax.rem(b, 2)
    hd = MLA_H * MLA_DN

    def page_copies(seq, slot_):
        return [pltpu.make_async_copy(cache_hbm.at[e, pt_ref[seq * n_pages + p]], buf.at[slot_, p], sem.at[slot_])
                for p in range(n_pages)]

    @pl.when(b == 0)
    def _():
        _start_all(page_copies(0, 0))

    @pl.when(b + 1 < pl.num_programs(0))
    def _():
        _start_all(page_copies(b + 1, 1 - slot))

    wukt = wukt_ref[...]
    lhs_scr[0:hd, :] = wukt
    qa = _dot(qbd_ref[...].astype(BF16), wukt)
    lhs_scr[hd:hd + 16, :] = jnp.concatenate([qa, jnp.zeros_like(qa)], axis=0).astype(BF16)
    lhs = lhs_scr[...]
    qrb = qr_ref[...].astype(BF16)
    _wait_all(page_copies(b, slot))

    def partial_softmax(g):
        pgs = [buf.at[slot, g * MLA_PG + p] for p in range(MLA_PG)]
        ct = jnp.concatenate([p[0:MLA_R, :] for p in pgs], axis=1).astype(BF16)
        krt = jnp.concatenate([p[MLA_R:MLA_R + MLA_DR, :] for p in pgs], axis=1).astype(BF16)
        keys = ct.shape[1]
        both = _dot(lhs, ct)
        knt = both[0:hd, :]
        ss = jnp.sum((knt * knt).reshape(MLA_H, MLA_DN, keys), axis=1)
        rinv = lax.rsqrt(ss * (1.0 / MLA_DN) + EPS)
        s = rinv * both[hd:hd + MLA_H, :] + _dot(qrb, krt)
        m = jnp.max(s, axis=-1, keepdims=True)
        p = jnp.exp2(s - m)
        return m, jnp.sum(p, axis=-1, keepdims=True), _dot_nt(p.astype(BF16), ct)

    parts = [partial_softmax(g) for g in range(n_pages // MLA_PG)]
    s_self = jnp.sum(qself_ref[...].astype(F32) * kself_ref[...].astype(F32), axis=-1, keepdims=True)
    m_new = s_self
    for m, _, _ in parts:
        m_new = jnp.maximum(m_new, m)
    p_self = jnp.exp2(s_self - m_new)
    l = p_self
    acc = p_self * vself_ref[...].astype(F32)
    for m, l_g, acc_g in parts:
        w = jnp.exp2(m - m_new)
        l = l + w * l_g
        acc = acc + w * acc_g
    r = _dot((acc / l).astype(BF16), wuv_ref[...])
    rh = lax.broadcasted_iota(I32, r.shape, 0)
    ch = lax.broadcasted_iota(I32, r.shape, 1) // MLA_DV
    a_ref[...] = jnp.sum(jnp.where(rh == ch, r, 0.0), axis=0, keepdims=True).astype(a_ref.dtype)


def _mla_sample(page_table, cache_t, e, qbd, qr, qself, kself, vself, wukt, wuv):
    bs = qbd.shape[0]
    n_pages = page_table.shape[1]
    assert n_pages % MLA_PG == 0
    width = MLA_R + MLA_DR
    hd = MLA_H * MLA_DN
    seq3 = lambda n2, n3: pl.BlockSpec((None, n2, n3), lambda b, pt: (b, 0, 0))
    grid_spec = pltpu.PrefetchScalarGridSpec(
        num_scalar_prefetch=1,
        grid=(bs,),
        in_specs=[seq3(MLA_H, hd), seq3(MLA_H, MLA_DR), seq3(MLA_H, LANES), seq3(MLA_H, LANES),
                  seq3(1, MLA_R),
                  pl.BlockSpec((None, hd, MLA_R), lambda b, pt: (e, 0, 0)),
                  pl.BlockSpec((None, MLA_R, MLA_H * MLA_DV), lambda b, pt: (e, 0, 0)),
                  pl.BlockSpec(memory_space=pl.ANY)],
        out_specs=pl.BlockSpec((None, 1, MLA_H * MLA_DV), lambda b, pt: (b, 0, 0)),
        scratch_shapes=[pltpu.VMEM((2, n_pages, width, PAGE), F32), pltpu.VMEM((hd + 16, MLA_R), BF16),
                        pltpu.SemaphoreType.DMA((2,))],
    )
    out = pl.pallas_call(
        functools.partial(_mla_s_body, n_pages=n_pages, e=e),
        grid_spec=grid_spec,
        out_shape=jax.ShapeDtypeStruct((bs, 1, MLA_H * MLA_DV), BF16),
        compiler_params=_params("arbitrary"),
        name="mla_sample",
    )(page_table.reshape(-1), qbd, qr, qself, kself, vself, wukt, wuv, cache_t)
    return out.reshape(bs, MLA_H * MLA_DV)


HG_BS = 32
SCAN_GROUP = 8


def _hgrn_body(q_ref, k_ref, lf_ref, v_ref, g_ref, s0_ref, gain_ref, wind_ref, tri_ref,
               o_ref, sout_ref, st_scr, z_scr, *, c_len, bs, nc, grp):
    c = pl.program_id(1)
    nb = c_len // bs
    hc = HG_H * c_len
    pk = 16

    @pl.when(c == 0)
    def _():
        z_scr[...] = jnp.zeros(z_scr.shape, BF16)
        for gi in range(grp):
            for h in range(HG_H):
                st_scr[gi, h] = s0_ref[gi, h].T

    tri = tri_ref[...]
    gain = gain_ref[...]
    wind = wind_ref[...]
    row = lax.broadcasted_iota(I32, (c_len, c_len), 0)
    col = lax.broadcasted_iota(I32, (c_len, c_len), 1)
    keep = (col <= row) & ((row >= bs) == (col >= bs))
    late = lax.broadcasted_iota(I32, (c_len, HG_W), 0) >= bs
    for gi in range(grp):
        b = sum(_dot(tri, t) for t in _split3(lf_ref[gi]))
        q = q_ref[gi].astype(F32)
        k = k_ref[gi]
        v = v_ref[gi]
        b_last = b[c_len - 1:c_len, :]
        qe = (q * jnp.exp(b)).astype(BF16)
        kd = (k * jnp.exp(b_last - b)).astype(BF16)
        if nb == 2:
            ref = b[bs - 1:bs, :]
            q_off = jnp.where(late, q * jnp.exp(jnp.minimum(b - ref, 0.0)), 0.0).astype(BF16)
            k_off = jnp.where(late, 0.0, k * jnp.exp(jnp.minimum(ref - b, 0.0))).astype(BF16)
        early = []
        for h in range(HG_H):
            hs = slice(h * LANES, (h + 1) * LANES)
            st = st_scr[gi, h]
            o_h = _dot_nt(qe[:, hs], st.astype(BF16))
            a_off = _dot_nt(q_off[:, hs], k_off[:, hs]) if nb == 2 else None
            st_scr[gi, h] = jnp.exp(b_last[:, hs]) * st + _dot_tn(v[:, hs], kd[:, hs])
            early.append((o_h, a_off))
        b3, q3, k3 = (t.reshape(nb, bs, HG_W) for t in (b * LOG2E, q, k))
        for s in range(bs):
            t0 = (s // pk) * pk
            z = q3[:, t0:, :] * k3[:, s:s + 1, :] * jnp.exp2(jnp.minimum(b3[:, t0:, :] - b3[:, s:s + 1, :], 0.0))
            z = z.astype(BF16)
            for ib in range(nb):
                for h in range(HG_H):
                    r0 = gi * hc + h * c_len + ib * bs
                    z_scr[r0 + t0:r0 + bs, s * LANES:(s + 1) * LANES] = z[ib, :, h * LANES:(h + 1) * LANES]
        g = g_ref[gi].astype(F32)
        for h in range(HG_H):
            hs = slice(h * LANES, (h + 1) * LANES)
            o_h, a_off = early[h]
            r0 = gi * hc + h * c_len
            a = jnp.where(keep, _dot(z_scr[r0:r0 + c_len, :], wind), 0.0)
            if nb == 2:
                a = a + a_off
            o_h = o_h + _dot(a.astype(BF16), v[:, hs])
            o_ref[gi, :, hs] = (_rms(o_h, gain[:, hs]) * g[:, hs]).astype(o_ref.dtype)

    @pl.when(c == nc - 1)
    def _():
        for gi in range(grp):
            for h in range(HG_H):
                sout_ref[gi, h] = st_scr[gi, h].T


def _hgrn_scan(hq, hk, hlf, hi, hg, s0, gain, b, l, c_len):
    bs = min(HG_BS, c_len)
    nc = l // c_len
    nb = c_len // bs
    assert nb in (1, 2) and c_len == nb * bs
    grp = SCAN_GROUP if b % SCAN_GROUP == 0 else 1
    chunk = lambda: pl.BlockSpec((grp, c_len, HG_W), lambda bi, ci: (bi, ci, 0))
    state = lambda: pl.BlockSpec((grp, HG_H, HG_DK, HG_DV), lambda bi, ci: (bi, 0, 0, 0))
    wind = jnp.tile(jnp.repeat(jnp.eye(bs, dtype=BF16), LANES, axis=0), (1, nb))
    tri = jnp.tril(jnp.ones((c_len, c_len), BF16))
    seq = lambda t: t.reshape(b, l, HG_W)
    o, s_new = pl.pallas_call(
        functools.partial(_hgrn_body, c_len=c_len, bs=bs, nc=nc, grp=grp),
        grid=(b // grp, nc),
        in_specs=[chunk(), chunk(), chunk(), chunk(), chunk(), state(), _const((1, HG_W)),
                  _const((bs * LANES, c_len)), _const((c_len, c_len))],
        out_specs=[chunk(), state()],
        out_shape=[jax.ShapeDtypeStruct((b, l, HG_W), BF16),
                   jax.ShapeDtypeStruct((b, HG_H, HG_DK, HG_DV), F32)],
        scratch_shapes=[pltpu.VMEM((grp, HG_H, HG_DV, HG_DK), F32),
                        pltpu.VMEM((grp * HG_H * c_len, bs * LANES), BF16)],
        compiler_params=_params("parallel", "arbitrary"),
        name="hgrn_scan",
    )(seq(hq), seq(hk), seq(hlf), seq(hi), seq(hg), s0, gain, wind, tri)
    return o.reshape(b * l, HG_W), s_new


def _out_body(x_ref, a_ref, o_ref, w_ref, y_ref):
    half = a_ref.shape[1]
    y_ref[...] = x_ref[...] + _dot(a_ref[...], w_ref[0:half, :]) + _dot(o_ref[...], w_ref[half:2 * half, :])


def _out_proj(x, a, o, w, e):
    m = x.shape[0]
    tm = _row_tile(m, 512)
    half = a.shape[1]
    return pl.pallas_call(
        _out_body,
        grid=(m // tm,),
        in_specs=[_rows(tm, D_MODEL), _rows(tm, half), _rows(tm, half), _layer((2 * half, D_MODEL), e)],
        out_specs=_rows(tm, D_MODEL),
        out_shape=jax.ShapeDtypeStruct((m, D_MODEL), F32),
        compiler_params=_params("parallel"),
        name="out_proj",
    )(x, a, o, w)


OD_Q, OD_K, OD_V = 0, 8 * LANES, 12 * LANES
OD_XQK, OD_XV, OD_OG, OD_G = 16 * LANES, 20 * LANES, 24 * LANES, 28 * LANES
OD_COLS = 29 * LANES


def _odd_in_body(x_ref, g_ref, w_ref, gq_ref, gk_ref, gb_ref,
                 qpad_ref, kpad_ref, vpad_ref, xqk_ref, xv_ref, og_ref, gates_ref, *state_t):
    hn = _rms(x_ref[...], g_ref[...]).astype(BF16)

    def headnorm(t, gain):
        ms = jnp.sum(t * t, axis=-1, keepdims=True) * (1.0 / MB_DH)
        return t * lax.rsqrt(ms + EPS) * gain

    gq, gk = gq_ref[...], gk_ref[...]
    for h in range(MB_H):
        t = _dot(hn, w_ref[:, OD_Q + h * LANES:OD_Q + (h + 1) * LANES])
        qpad_ref[:, h * LANES:(h + 1) * LANES] = (headnorm(t, gq) * MB_SCALE).astype(BF16)
    for h in range(MB_KVH):
        t = _dot(hn, w_ref[:, OD_K + h * LANES:OD_K + (h + 1) * LANES])
        kn = headnorm(t, gk)
        kpad_ref[:, h * LANES:(h + 1) * LANES] = kn
        if state_t:
            state_t[0][h] = kn.T[0:MB_DH, :]
    v = _dot(hn, w_ref[:, OD_V:OD_V + 4 * LANES])
    vpad_ref[...] = v
    if state_t:
        for h in range(MB_KVH):
            state_t[1][h] = v[:, h * LANES:(h + 1) * LANES].T[0:MB_DH, :]
    xqk_ref[...] = _dot(hn, w_ref[:, OD_XQK:OD_XQK + 4 * LANES])
    xv_ref[...] = _dot(hn, w_ref[:, OD_XV:OD_XV + 4 * LANES]).astype(xv_ref.dtype)
    og_ref[...] = _sigmoid(_dot(hn, w_ref[:, OD_OG:OD_OG + 4 * LANES])).astype(og_ref.dtype)
    gt = _dot(hn, w_ref[:, OD_G:OD_G + LANES]) + gb_ref[...]
    lane = lax.broadcasted_iota(I32, gt.shape, 1)
    logsig = jnp.minimum(gt, 0.0) - jnp.log(1.0 + jnp.exp(-jnp.abs(gt)))
    gates_ref[...] = jnp.where(lane < ML_H, gt, logsig)


def _odd_in(x, P, j, l, seq=None):
    m = x.shape[0]
    tm = _row_tile(m)
    outs = [(8 * LANES, BF16), (4 * LANES, F32), (4 * LANES, F32), (2 * ML_QK, F32),
            (ML_H * ML_DV, BF16), (ML_H * ML_DV, BF16), (LANES, F32)]
    out_specs = [_rows(tm, n) for n, _ in outs]
    out_shape = [jax.ShapeDtypeStruct((m, n), dt) for n, dt in outs]
    if seq is not None:
        assert seq % tm == 0 and tm % LANES == 0
        nt = seq // tm
        for _ in range(2):
            out_specs.append(pl.BlockSpec((None, MB_KVH, MB_DH, tm), lambda i: (i // nt, 0, 0, i % nt)))
            out_shape.append(jax.ShapeDtypeStruct((m // seq, MB_KVH, MB_DH, seq), F32))
    return pl.pallas_call(
        _odd_in_body,
        grid=(m // tm,),
        in_specs=[_rows(tm, D_MODEL), _layer((1, D_MODEL), l), _layer((D_MODEL, OD_COLS), j),
                  _layer((1, LANES), j), _layer((1, LANES), j), _layer((1, LANES), j)],
        out_specs=out_specs,
        out_shape=out_shape,
        compiler_params=_params("parallel"),
        name="odd_in",
    )(x, P["norm_mix"], P["odd_w"], P["moba_gq"], P["moba_gk"], P["mlstm_gb"])


def _moba_p_body(q_ref, k_ref, v_ref, place_ref, a_ref, kmt_scr, qa_scr, m_scr, l_scr, acc_scr, *, nb):
    i = pl.program_id(1)
    blk = MB_BLOCK
    row = lax.broadcasted_iota(I32, (blk, blk), 0)
    col = lax.broadcasted_iota(I32, (blk, blk), 1)
    causal = col <= row
    lane = lax.broadcasted_iota(I32, (blk, LANES), 1)

    @pl.when(i == 0)
    def _():
        kmean = jnp.mean(k_ref[...].reshape(nb, blk, MB_KVH * LANES), axis=1)
        kmt_scr[...] = jnp.zeros(kmt_scr.shape, F32)
        hi = kmean.astype(BF16).astype(F32)
        r1 = kmean - hi
        mid = r1.astype(BF16).astype(F32)
        lo = r1 - mid
        for t, part in enumerate((hi, mid, lo)):
            for h in range(MB_H):
                g = h // MB_GRP
                r0 = (t * MB_H + h) * nb
                kmt_scr[r0:r0 + nb, h * LANES:(h + 1) * LANES] = part[:, g * LANES:(g + 1) * LANES]

    q_all = q_ref[...]
    g3 = _dot_nt(kmt_scr[...].astype(BF16), q_all)
    hn = MB_H * nb
    gate = (g3[0:hn] + g3[hn:2 * hn] + g3[2 * hn:3 * hn]).reshape(MB_H, nb, blk)
    q_all = q_all.astype(F32)
    blk_id = lax.broadcasted_iota(I32, (1, nb, blk), 1)
    cand = jnp.where(lax.broadcasted_iota(I32, (1, nb, 1), 1) < i, 1.0, 0.0)
    rank = jnp.zeros((MB_H, nb, blk), F32)
    for c in range(nb - 1):
        gc = gate[:, c:c + 1, :]
        better = (gc > gate) | ((gc == gate) & (c < blk_id))
        rank = rank + jnp.where(better, cand[:, c:c + 1, :], 0.0)
    bias = jnp.where(rank < MB_TOPK, cand, 0.0) * (-NEG) + NEG
    qa_scr[...] = (q_all + _dot_tn(bias.reshape(MB_H * nb, blk).astype(BF16), place_ref[...])).astype(BF16)

    def step(j, diag):
        start = pl.multiple_of(j * blk, blk)
        for g in range(MB_KVH):
            gs = slice(g * LANES, (g + 1) * LANES)
            kj = k_ref[pl.ds(start, blk), gs]
            if not diag:
                kj = jnp.where(lane == MB_DH + j, 1.0, kj)
            kj = kj.astype(BF16)
            vj = v_ref[pl.ds(start, blk), gs].astype(BF16)
            for h in range(g * (MB_H // MB_KVH), (g + 1) * (MB_H // MB_KVH)):
                s = _dot_nt(qa_scr[:, h * LANES:(h + 1) * LANES], kj)
                if diag:
                    s = jnp.where(causal, s, NEG)
                    m_new = jnp.broadcast_to(jnp.max(s, axis=-1, keepdims=True), (blk, LANES))
                else:
                    m_old = m_scr[h]
                    m_new = jnp.maximum(m_old, jnp.max(s, axis=-1, keepdims=True))
                    alpha = jnp.exp2(m_old - m_new)
                p = jnp.exp2(s - jnp.concatenate([m_new] * (blk // LANES), axis=-1))
                l_new = jnp.broadcast_to(jnp.sum(p, axis=-1, keepdims=True), (blk, LANES))
                acc_new = _dot(p.astype(BF16), vj)
                l_scr[h] = l_new if diag else alpha * l_scr[h] + l_new
                acc_scr[h] = acc_new if diag else alpha * acc_scr[h] + acc_new
                m_scr[h] = m_new

    def body(j, c):
        step(j, False)
        return c

    step(i, True)
    lax.fori_loop(0, i, body, 0)
    for h in range(MB_H):
        a_ref[:, h * MB_DH:(h + 1) * MB_DH] = (acc_scr[h] / l_scr[h])[:, 0:MB_DH].astype(a_ref.dtype)


def _moba_prompt(qpad, kpad, vpad, b, l):
    assert l % MB_BLOCK == 0 and l // MB_BLOCK <= LANES - MB_DH
    nb = l // MB_BLOCK
    stat = pltpu.VMEM((MB_H, MB_BLOCK, LANES), F32)
    hc = jnp.arange(MB_H * nb)
    place = (jnp.arange(MB_H * LANES)[None, :] == ((hc // nb) * LANES + MB_DH + hc % nb)[:, None]).astype(BF16)
    return pl.pallas_call(
        functools.partial(_moba_p_body, nb=nb),
        grid=(b, nb),
        in_specs=[pl.BlockSpec((MB_BLOCK, 8 * LANES), lambda bi, i: (bi * nb + i, 0)),
                  pl.BlockSpec((l, 4 * LANES), lambda bi, i: (bi, 0)),
                  pl.BlockSpec((l, 4 * LANES), lambda bi, i: (bi, 0)),
                  _const((MB_H * nb, MB_H * LANES))],
        out_specs=pl.BlockSpec((MB_BLOCK, MB_H * MB_DH), lambda bi, i: (bi * nb + i, 0)),
        out_shape=jax.ShapeDtypeStruct((b * l, MB_H * MB_DH), BF16),
        scratch_shapes=[pltpu.VMEM((3 * MB_H * nb, MB_H * LANES), F32), pltpu.VMEM((MB_BLOCK, MB_H * LANES), BF16),
                        stat, stat, stat],
        compiler_params=_params("parallel", "arbitrary"),
        name="moba_prompt",
    )(qpad, kpad, vpad, place)


def _moba_sel_body(pt_ref, qsel_ref, ck_hbm, sel_ref, buf, sem, *, n_pages, n_cand, jl):
    b = pl.program_id(0)
    slot = lax.rem(b, 2)
    ppb = MB_BLOCK // PAGE

    def page_copies(seq, slot_):
        return [pltpu.make_async_copy(ck_hbm.at[jl, pt_ref[seq * n_pages + p]], buf.at[slot_, p], sem.at[slot_])
                for p in range(n_pages)]

    @pl.when(b == 0)
    def _():
        _start_all(page_copies(0, 0))

    @pl.when(b + 1 < pl.num_programs(0))
    def _():
        _start_all(page_copies(b + 1, 1 - slot))

    _wait_all(page_copies(b, slot))
    lane = lax.broadcasted_iota(I32, (MB_KVH * MB_DH, LANES), 1)
    kmt = jnp.zeros((MB_KVH * MB_DH, LANES), F32)
    for c in range(n_cand):
        tot = buf[slot, c * ppb]
        for p in range(1, ppb):
            tot = tot + buf[slot, c * ppb + p]
        mean = jnp.sum(tot, axis=-1, keepdims=True) * (1.0 / MB_BLOCK)
        kmt = jnp.where(lane == c, mean, kmt)
    gate = jnp.dot(qsel_ref[...], kmt, precision=HI, preferred_element_type=F32)
    lane_g = lax.broadcasted_iota(I32, gate.shape, 1)
    rank = jnp.zeros(gate.shape, I32)
    for c in range(n_cand):
        gc = gate[:, c:c + 1]
        better = (gc > gate) | ((gc == gate) & (c < lane_g))
        rank = rank + jnp.where(better, 1, 0)
    out_lane = lax.broadcasted_iota(I32, sel_ref.shape, 1)
    out = jnp.zeros(sel_ref.shape, I32)
    for r in range(MB_TOPK):
        idx = jnp.sum(jnp.where((rank == r) & (lane_g < n_cand), lane_g, 0), axis=-1, keepdims=True)
        out = jnp.where(out_lane == r, idx, out)
    sel_ref[...] = out


def _moba_select(page_table, cache_kt, jl, qsel):
    bs = qsel.shape[0]
    n_pages = page_table.shape[1]
    n_cand = n_pages * PAGE // MB_BLOCK
    assert MB_TOPK <= n_cand <= LANES and n_pages * PAGE % MB_BLOCK == 0
    width = MB_KVH * MB_DH
    grid_spec = pltpu.PrefetchScalarGridSpec(
        num_scalar_prefetch=1,
        grid=(bs,),
        in_specs=[pl.BlockSpec((None, MB_H, width), lambda b, pt: (b, 0, 0)), pl.BlockSpec(memory_space=pl.ANY)],
        out_specs=pl.BlockSpec((None, MB_H, LANES), lambda b, pt: (b, 0, 0)),
        scratch_shapes=[pltpu.VMEM((2, n_pages, width, PAGE), F32), pltpu.SemaphoreType.DMA((2,))],
    )
    sel = pl.pallas_call(
        functools.partial(_moba_sel_body, n_pages=n_pages, n_cand=n_cand, jl=jl),
        grid_spec=grid_spec,
        out_shape=jax.ShapeDtypeStruct((bs, MB_H, LANES), I32),
        compiler_params=_params("arbitrary"),
        name="moba_select",
    )(page_table.reshape(-1), qsel, cache_kt)
    return sel[:, :, :MB_TOPK]


MB_GRP = MB_H // MB_KVH


def _moba_s_body(pt_ref, sel_ref, q_ref, kself_ref, vself_ref, ck_hbm, cv_hbm, o_ref, kbuf, vbuf, sem,
                 *, n_pages, jl):
    b = pl.program_id(0)
    slot = lax.rem(b, 2)
    ppb = MB_BLOCK // PAGE
    n = MB_TOPK * ppb

    def tile_copies(seq, slot_):
        out = []
        for h in range(MB_H):
            rows = pl.ds((h // MB_GRP) * MB_DH, MB_DH)
            for r in range(MB_TOPK):
                blk = sel_ref[(seq * MB_H + h) * MB_TOPK + r]
                for p in range(ppb):
                    page = pt_ref[seq * n_pages + blk * ppb + p]
                    i = h * n + r * ppb + p
                    out.append(pltpu.make_async_copy(ck_hbm.at[jl, page, rows], kbuf.at[slot_, i], sem.at[0, slot_]))
                    out.append(pltpu.make_async_copy(cv_hbm.at[jl, page, rows], vbuf.at[slot_, i], sem.at[1, slot_]))
        return out

    @pl.when(b == 0)
    def _():
        _start_all(tile_copies(0, 0))

    @pl.when(b + 1 < pl.num_programs(0))
    def _():
        _start_all(tile_copies(b + 1, 1 - slot))

    _wait_all(tile_copies(b, slot))
    for h in range(MB_H):
        g = h // MB_GRP
        q1 = q_ref[h:h + 1, :]
        kself, vself = kself_ref[g:g + 1, :], vself_ref[g:g + 1, :]
        q = jnp.broadcast_to(q1, (8, MB_DH)).astype(BF16)
        scores = [_dot(q, kbuf[slot, h * n + i].astype(BF16)) for i in range(n)]
        s_self = jnp.sum(q1 * kself, axis=-1, keepdims=True)
        m = s_self
        for s in scores:
            m = jnp.maximum(m, jnp.max(s, axis=-1, keepdims=True))
        p_self = jnp.exp2(s_self - m)
        l = p_self
        acc = p_self * vself
        for i, s in enumerate(scores):
            p = jnp.exp2(s - m)
            l = l + jnp.sum(p, axis=-1, keepdims=True)
            acc = acc + _dot_nt(p.astype(BF16), vbuf[slot, h * n + i].astype(BF16))
        o_ref[h:h + 1, :] = (acc / l)[0:1, :]


def _moba_sample(page_table, sel, cache_kt, cache_vt, jl, q, kself, vself):
    bs = q.shape[0]
    n_pages = page_table.shape[1]
    n_tiles = MB_H * MB_TOPK * (MB_BLOCK // PAGE)
    per_seq = lambda nh: pl.BlockSpec((None, nh, MB_DH), lambda b, pt, sl: (b, 0, 0))
    grid_spec = pltpu.PrefetchScalarGridSpec(
        num_scalar_prefetch=2,
        grid=(bs,),
        in_specs=[per_seq(MB_H), per_seq(MB_KVH), per_seq(MB_KVH),
                  pl.BlockSpec(memory_space=pl.ANY), pl.BlockSpec(memory_space=pl.ANY)],
        out_specs=per_seq(MB_H),
        scratch_shapes=[pltpu.VMEM((2, n_tiles, MB_DH, PAGE), F32), pltpu.VMEM((2, n_tiles, MB_DH, PAGE), F32),
                        pltpu.SemaphoreType.DMA((2, 2))],
    )
    return pl.pallas_call(
        functools.partial(_moba_s_body, n_pages=n_pages, jl=jl),
        grid_spec=grid_spec,
        out_shape=jax.ShapeDtypeStruct((bs, MB_H, MB_DH), F32),
        compiler_params=_params("arbitrary"),
        name="moba_sample",
    )(page_table.reshape(-1), sel.reshape(-1), q, kself, vself, cache_kt, cache_vt)


def _mlstm_body(xqk_ref, xv_ref, og_ref, gcol_ref, grow_ref, cw_ref, cb_ref, prev_ref, c0_ref, m0_ref,
                gain_ref, tri_ref, triu_ref,
                h_ref, cout_ref, mout_ref, convout_ref, full_scr, c_scr, m_scr, *, c_len, nc, l_last, grp):
    c = pl.program_id(1)
    pad = 8
    w = 2 * ML_QK

    @pl.when(c == 0)
    def _():
        full_scr[:, pad - (CONV_W - 1):pad, :] = prev_ref[...]
        c_scr[...] = c0_ref[...]
        m_scr[...] = m0_ref[...]

    row = lax.broadcasted_iota(I32, (c_len, c_len), 0)
    col = lax.broadcasted_iota(I32, (c_len, c_len), 1)
    causal = col <= row
    head_of_lane = lax.broadcasted_iota(I32, (1, ML_QK), 1) // ML_DK
    head_of_row = lax.broadcasted_iota(I32, (ML_QK, 1), 0) // ML_DK
    m_lane = lax.broadcasted_iota(I32, (1, ML_H), 1)
    one_col = jnp.where(lax.broadcasted_iota(I32, (c_len, LANES), 1) == 0, 1.0, 0.0).astype(BF16)
    gain = gain_ref[...]
    tri, triu = tri_ref[...], triu_ref[...]

    for gi in range(grp):
        full_scr[gi, pad:pad + c_len, :] = xqk_ref[gi]
        conv = cb_ref[...]
        for t in range(CONV_W):
            o = pad - (CONV_W - 1) + t
            conv = conv + full_scr[gi, o:o + c_len, :] * cw_ref[t:t + 1, :]

        @pl.when(c == nc - 1)
        def _():
            convout_ref[gi] = full_scr[gi, pad + l_last - (CONV_W - 1):pad + l_last, :]

        full_scr[gi, pad - (CONV_W - 1):pad, :] = full_scr[gi, pad + c_len - (CONV_W - 1):pad + c_len, :]

        qk = _silu(conv)
        qf = qk[:, 0:ML_QK]
        kf = qk[:, ML_QK:w] * (ML_DK ** -0.5)
        kb = kf.astype(BF16)
        gcol, grow = gcol_ref[gi], grow_ref[gi]
        f_col = jnp.dot(tri, gcol, precision=HI, preferred_element_type=F32)
        f_row = jnp.dot(grow, triu, precision=HI, preferred_element_type=F32)
        state = c_scr[gi]
        state_b = state.astype(BF16)
        m_all = m_scr[gi]
        xv = xv_ref[gi]
        og = og_ref[gi].astype(F32)
        decay_col = jnp.zeros((ML_QK, 1), F32)
        upd = jnp.zeros((ML_QK, 2 * LANES), F32)
        m_next = jnp.zeros((1, ML_H), F32)
        for h in range(ML_H):
            fc = f_col[:, ML_H + h:ML_H + h + 1]
            ic = gcol[:, h:h + 1]
            fr = f_row[ML_H + h:ML_H + h + 1, :]
            ir = grow[h:h + 1, :]
            m_prev = m_all[:, h:h + 1]
            dmat = jnp.where(causal, fc - fr + ir, NEG)
            inter = fc + m_prev
            mt = jnp.maximum(jnp.max(dmat, axis=-1, keepdims=True), inter)
            qh = jnp.where(head_of_lane == h, qf, 0.0).astype(BF16)
            wd = jnp.exp(dmat - mt) * _dot_nt(qh, kb)
            a = jnp.exp(inter - mt)
            vaug = jnp.concatenate([xv[:, h * ML_DV:(h + 1) * ML_DV], one_col], axis=-1)
            num = _dot(wd.astype(BF16), vaug) + a * _dot(qh, state_b)
            den = num[:, ML_DV:ML_DV + 1]
            hh = num[:, 0:ML_DV] / jnp.maximum(jnp.abs(den), jnp.exp(-mt))
            hs = slice(h * ML_DV, (h + 1) * ML_DV)
            h_ref[gi, :, hs] = (_rms(hh, gain[:, hs]) * og[:, hs]).astype(h_ref.dtype)
            m_new = mt[c_len - 1:c_len, :]
            f_last = fc[c_len - 1:c_len, :]
            gk = (jnp.where(head_of_lane == h, kf, 0.0) * jnp.exp(f_last - fc + ic - m_new)).astype(BF16)
            upd = upd + _dot_tn(gk, vaug)
            decay_col = decay_col + jnp.where(head_of_row == h, jnp.exp(f_last + m_prev - m_new), 0.0)
            m_next = m_next + jnp.where(m_lane == h, m_new, 0.0)
        c_scr[gi] = decay_col * state + upd
        m_scr[gi] = m_next

    @pl.when(c == nc - 1)
    def _():
        cout_ref[...] = c_scr[...]
        mout_ref[...] = m_scr[...]


def _mlstm_scan(xqk, xv, og, gcol, grow, cw, cb, prev, c0, m0, gain, j, b, l_pad, c_len, l_last):
    nc = l_pad // c_len
    w = 2 * ML_QK
    grp = SCAN_GROUP if b % SCAN_GROUP == 0 else 1
    chunk = lambda n: pl.BlockSpec((grp, c_len, n), lambda bi, ci: (bi, ci, 0))
    per_b = lambda n2, n3: pl.BlockSpec((grp, n2, n3), lambda bi, ci: (bi, 0, 0))
    tri = jnp.tril(jnp.ones((c_len, c_len), F32))
    seq = lambda t: t.reshape(b, l_pad, t.shape[-1])
    h, caug, m_new, conv_new = pl.pallas_call(
        functools.partial(_mlstm_body, c_len=c_len, nc=nc, l_last=l_last, grp=grp),
        grid=(b // grp, nc),
        in_specs=[chunk(w), chunk(ML_H * ML_DV), chunk(ML_H * ML_DV), chunk(LANES),
                  pl.BlockSpec((grp, None, 8, c_len), lambda bi, ci: (bi, ci, 0, 0)),
                  _layer((CONV_W, w), j), _layer((1, w), j), per_b(CONV_W - 1, w),
                  per_b(ML_QK, 2 * LANES), per_b(1, ML_H), _layer((1, ML_H * ML_DV), j),
                  _const((c_len, c_len)), _const((c_len, c_len))],
        out_specs=[chunk(ML_H * ML_DV), per_b(ML_QK, 2 * LANES), per_b(1, ML_H), per_b(CONV_W - 1, w)],
        out_shape=[jax.ShapeDtypeStruct((b, l_pad, ML_H * ML_DV), BF16),
                   jax.ShapeDtypeStruct((b, ML_QK, 2 * LANES), F32),
                   jax.ShapeDtypeStruct((b, 1, ML_H), F32),
                   jax.ShapeDtypeStruct((b, CONV_W - 1, w), F32)],
        scratch_shapes=[pltpu.VMEM((grp, c_len + 8, w), F32), pltpu.VMEM((grp, ML_QK, 2 * LANES), F32),
                        pltpu.VMEM((grp, 1, ML_H), F32)],
        compiler_params=_params("parallel", "arbitrary"),
        name="mlstm_scan",
    )(seq(xqk), seq(xv), seq(og), seq(gcol), grow, cw, cb, prev, c0, m0, gain, tri, tri.T)
    return h.reshape(b * l_pad, ML_H * ML_DV), caug, m_new, conv_new


def _prepare(norm_ffn1, norm_mix, norm_ffn2, ffn1_w_in, ffn1_w_out, ffn2_w_in, ffn2_w_out, even_w_in, even_w_out,
             mla_q_norm, mla_w_uq, mla_kv_norm, mla_w_uk, mla_w_uv, mla_qn_norm, mla_qr_norm, mla_kn_norm,
             mla_kr_norm, hgrn_lb_raw, hgrn_out_norm, odd_w_in, odd_w_out, moba_q_norm, moba_k_norm,
             mlstm_conv_w, mlstm_conv_b, mlstm_b_i, mlstm_b_f, mlstm_out_norm):
    ne, no = even_w_in.shape[0], odd_w_in.shape[0]
    row = lambda t: t[:, None, :].astype(F32)
    P = dict(norm_ffn1=row(norm_ffn1), norm_mix=row(norm_mix), norm_ffn2=row(norm_ffn2),
             ffn1_w_in=ffn1_w_in.astype(BF16), ffn1_w_out=ffn1_w_out.astype(BF16),
             ffn2_w_in=ffn2_w_in.astype(BF16), ffn2_w_out=ffn2_w_out.astype(BF16),
             even_w_out=even_w_out.astype(BF16), odd_w_out=odd_w_out.astype(BF16))
    w = even_w_in
    kr = jnp.zeros((ne, D_MODEL, LANES), F32).at[:, :, 64:96].set(w[:, :, 384:416])
    P["even_w"] = jnp.concatenate([w[:, :, 0:384], kr, w[:, :, 416:]], axis=-1).astype(BF16)
    pad_heads = lambda t, n: jnp.pad(t, ((0, 0), (0, 0), (0, 0), (0, LANES - n))).reshape(ne, t.shape[1], 8 * LANES)
    P["mla_wuq"] = pad_heads(mla_w_uq, MLA_DN + MLA_DR).astype(BF16)
    P["mla_wuk"] = pad_heads(mla_w_uk, MLA_DN).astype(BF16)
    P["mla_wukt"] = jnp.swapaxes(mla_w_uk.reshape(ne, MLA_R, MLA_H * MLA_DN), 1, 2).astype(BF16)
    P["mla_wuv_flat"] = mla_w_uv.reshape(ne, MLA_R, MLA_H * MLA_DV).astype(BF16)
    wuv = jnp.transpose(mla_w_uv, (0, 2, 1, 3))
    odd_head = (jnp.arange(MLA_H) % 2 == 1)[None, :, None, None]
    P["mla_wuv"] = jnp.where(odd_head, jnp.pad(wuv, ((0, 0),) * 3 + ((MLA_DV, 0),)),
                             jnp.pad(wuv, ((0, 0),) * 3 + ((0, MLA_DV),))).astype(BF16)
    z32 = jnp.zeros((ne, LANES - MLA_DN - MLA_DR), F32)
    P["mla_q_norm"], P["mla_kv_norm"] = row(mla_q_norm), row(mla_kv_norm)
    P["mla_gq"] = row(jnp.concatenate([mla_qn_norm, mla_qr_norm, z32], axis=-1))
    P["mla_gk"] = row(jnp.concatenate([mla_kn_norm, jnp.zeros((ne, LANES - MLA_DN), F32)], axis=-1))
    P["mla_gkr"] = row(jnp.concatenate([jnp.zeros((ne, MLA_DN), F32), mla_kr_norm, z32], axis=-1))
    P["mla_kn_gain"] = mla_kn_norm.astype(F32)
    seg = jnp.arange(LANES)
    seg_id = jnp.where(seg < MLA_DN, 0, jnp.where(seg < MLA_DN + MLA_DR, 1, 2 + seg))
    seg_len = jnp.where(seg < MLA_DN, MLA_DN, MLA_DR).astype(F32)
    P["bseg"] = jnp.where(seg_id[:, None] == seg_id[None, :], 1.0 / seg_len[None, :], 0.0).astype(BF16)
    p_lb = jax.nn.softmax(hgrn_lb_raw.astype(F32), axis=0)
    lb = jnp.maximum(jnp.cumsum(p_lb, axis=0) - p_lb[0], 0.0)
    P["hg_loglb"], P["hg_log1mlb"], P["hg_1mlb"] = row(jnp.log(lb)), row(jnp.log1p(-lb)), row(1.0 - lb)
    P["hg_gain"] = row(jnp.tile(hgrn_out_norm, (1, HG_H)))
    w = odd_w_in
    c0 = MB_H * MB_DH
    c1 = c0 + MB_KVH * MB_DH
    c2 = c1 + MB_KVH * MB_DH
    c3 = c2 + 2 * ML_QK
    c4 = c3 + ML_H * ML_DV
    c5 = c4 + 2 * ML_H
    pad_h = lambda t, nh: jnp.pad(t.reshape(no, D_MODEL, nh, MB_DH),
                                  ((0, 0), (0, 0), (0, 0), (0, LANES - MB_DH))).reshape(no, D_MODEL, nh * LANES)
    gates = jnp.pad(w[:, :, c4:c5], ((0, 0), (0, 0), (0, LANES - 2 * ML_H)))
    P["odd_w"] = jnp.concatenate([pad_h(w[:, :, 0:c0], MB_H), pad_h(w[:, :, c0:c1], MB_KVH),
                                  pad_h(w[:, :, c1:c2], MB_KVH), w[:, :, c2:c4], w[:, :, c5:], gates],
                                 axis=-1).astype(BF16)
    zpad = jnp.zeros((no, LANES - MB_DH), F32)
    P["moba_gq"] = row(jnp.concatenate([moba_q_norm, zpad], axis=-1))
    P["moba_gk"] = row(jnp.concatenate([moba_k_norm, zpad], axis=-1))
    P["mlstm_gb"] = row(jnp.concatenate([mlstm_b_i, mlstm_b_f, jnp.zeros((no, LANES - 2 * ML_H), F32)], axis=-1))
    P["mlstm_cw"] = mlstm_conv_w.astype(F32)
    P["mlstm_cb"] = row(mlstm_conv_b)
    P["mlstm_gain"] = row(jnp.tile(mlstm_out_norm, (1, ML_H)))
    return P


def _rope_tables(pos, n_rows):
    half = MLA_DR // 2
    freqs = ROPE_THETA ** (-jnp.arange(half, dtype=F32) / half)
    ang = pos.astype(F32)[:, None] * freqs[None, :]
    cos, sin = jnp.cos(ang), jnp.sin(ang)
    n = pos.shape[0]
    z = lambda k: jnp.zeros((n, k), F32)
    cos_t = jnp.concatenate([jnp.ones((n, MLA_DN), F32), cos, cos, z(LANES - MLA_DN - MLA_DR)], axis=-1)
    sin_a = jnp.concatenate([z(MLA_DN + half), sin, z(LANES - MLA_DN - MLA_DR)], axis=-1)
    sin_b = jnp.concatenate([z(MLA_DN), -sin, z(LANES - MLA_DN - half)], axis=-1)
    rep = n_rows // n
    return tuple(jnp.tile(t, (rep, 1)) for t in (cos_t, sin_a, sin_b))


def _pad_time(t, b, l, l_pad, value=0.0):
    if l_pad == l:
        return t
    n = t.shape[-1]
    fill = jnp.broadcast_to(jnp.asarray(value, t.dtype), (b, l_pad - l, n))
    return jnp.concatenate([t.reshape(b, l, n), fill], axis=1).reshape(b * l_pad, n)


def _even_layer(x, P, e, l, b, seq, tabs, ctx):
    qcat, kcat, rows, vlat, hq, hk, hlf, hi, hg = _even_in(x, P, e, l, tabs)
    if ctx is None:
        a = _mla_prompt(qcat, kcat, vlat, P["mla_wuv"], e, b, seq)
        s0 = jnp.zeros((b, HG_H, HG_DK, HG_DV), F32)
        c_len, l_pad = min(64, seq), seq
    else:
        q3 = qcat.reshape(b, MLA_H, LANES).astype(F32)
        eye = jnp.eye(MLA_H, dtype=F32)
        qbd = jnp.einsum("bhd,hg->bhgd", q3[:, :, :MLA_DN] * P["mla_kn_gain"][e][None, None, :], eye)
        a = _mla_sample(ctx["page_table"], ctx["cache_mla_t"], e, qbd.reshape(b, MLA_H, MLA_H * MLA_DN),
                        q3[:, :, MLA_DN:MLA_DN + MLA_DR], qcat.reshape(b, MLA_H, LANES),
                        kcat.reshape(b, MLA_H, LANES), vlat.reshape(b, 1, MLA_R),
                        P["mla_wukt"], P["mla_wuv_flat"])
        s0 = ctx["state_hgrn"][e].astype(F32)
        c_len = l_pad = 16
    pad = lambda t: _pad_time(t, b, seq, l_pad)
    o, s_new = _hgrn_scan(pad(hq), pad(hk), pad(hlf), pad(hi), pad(hg), s0, P["hg_gain"][e], b, l_pad, c_len)
    if l_pad != seq:
        o = o.reshape(b, l_pad, HG_W)[:, :seq].reshape(b * seq, HG_W)
    y = _out_proj(x, a, o, P["even_w_out"], e)
    return y, rows.reshape(b, seq, MLA_R + MLA_DR), s_new


def _odd_layer(x, P, j, l, b, seq, ctx):
    if ctx is None and seq % _row_tile(b * seq) == 0 and _row_tile(b * seq) % LANES == 0:
        qpad, kpad, vpad, xqk, xv, og, gates, kt, vt = _odd_in(x, P, j, l, seq)
        k_new, v_new = (jnp.transpose(t, (0, 3, 1, 2)) for t in (kt, vt))
    else:
        qpad, kpad, vpad, xqk, xv, og, gates = _odd_in(x, P, j, l)
        unpad = lambda t: t.reshape(b, seq, MB_KVH, LANES)[..., :MB_DH]
        k_new, v_new = unpad(kpad), unpad(vpad)
    if ctx is None:
        a = _moba_prompt(qpad, kpad, vpad, b, seq)
        prev = jnp.zeros((b, CONV_W - 1, 2 * ML_QK), F32)
        caug0 = jnp.zeros((b, ML_QK, 2 * LANES), F32)
        m0 = jnp.zeros((b, 1, ML_H), F32)
        c_len, l_pad = min(64, seq), seq
    else:
        assert seq == 1
        pt = ctx["page_table"]
        qh = qpad.reshape(b, MB_H, LANES)[:, :, :MB_DH].astype(F32)
        on_kv = jax.nn.one_hot(jnp.arange(MB_H) // MB_GRP, MB_KVH, dtype=F32)
        qsel = jnp.einsum("bhd,hg->bhgd", qh, on_kv).reshape(b, MB_H, MB_KVH * MB_DH)
        sel = _moba_select(pt, ctx["cache_moba_kt"], j, qsel)
        a = _moba_sample(pt, sel, ctx["cache_moba_kt"], ctx["cache_moba_vt"], j, qh,
                         k_new.reshape(b, MB_KVH, MB_DH), v_new.reshape(b, MB_KVH, MB_DH))
        a = a.reshape(b, MB_H * MB_DH).astype(BF16)
        prev = ctx["state_mlstm_conv"][j].astype(F32)
        caug0 = jnp.concatenate([ctx["state_mlstm_c"][j].astype(F32),
                                 ctx["state_mlstm_n"][j].astype(F32)[..., None],
                                 jnp.zeros((b, ML_H, ML_DK, LANES - 1), F32)], axis=-1).reshape(b, ML_QK, 2 * LANES)
        m0 = ctx["state_mlstm_m"][j].astype(F32).reshape(b, 1, ML_H)
        c_len = l_pad = 16
    nc = l_pad // c_len
    gate_fill = jnp.where(jnp.arange(LANES) < ML_H, NEG, 0.0)
    gcol = _pad_time(gates, b, seq, l_pad, gate_fill)
    grow = jnp.swapaxes(gcol[:, :8].reshape(b, nc, c_len, 8), 2, 3)
    pad = lambda t: _pad_time(t, b, seq, l_pad)
    l_last = seq - (nc - 1) * c_len
    h, caug, m_new, conv_new = _mlstm_scan(pad(xqk), pad(xv), pad(og), gcol, grow, P["mlstm_cw"], P["mlstm_cb"],
                                           prev, caug0, m0, P["mlstm_gain"], j, b, l_pad, c_len, l_last)
    if l_pad != seq:
        h = h.reshape(b, l_pad, ML_H * ML_DV)[:, :seq].reshape(b * seq, ML_H * ML_DV)
    y = _out_proj(x, a, h, P["odd_w_out"], j)
    caug = caug.reshape(b, ML_H, ML_DK, 2 * LANES)
    return y, k_new, v_new, caug[..., :ML_DV], caug[..., ML_DV], m_new.reshape(b, ML_H), conv_new


def _trunk(x, pos, P, ctx):
    b, seq, _ = x.shape
    m = b * seq
    x = x.reshape(m, D_MODEL).astype(F32)
    depth = P["norm_mix"].shape[0]
    tabs = _rope_tables(pos, max(_row_tile(m), seq))
    new = {k: [] for k in ("mla", "hgrn", "moba_k", "moba_v", "mlstm_c", "mlstm_n", "mlstm_m", "mlstm_conv")}
    for l in range(depth):
        x = _ffn(x, P["norm_ffn1"], P["ffn1_w_in"], P["ffn1_w_out"], l)
        if l % 2 == 0:
            x, rows, s = _even_layer(x, P, l // 2, l, b, seq, tabs, ctx)
            new["mla"].append(rows)
            new["hgrn"].append(s)
        else:
            x, k, v, c, n, mm, cv = _odd_layer(x, P, l // 2, l, b, seq, ctx)
            for name, val in zip(("moba_k", "moba_v", "mlstm_c", "mlstm_n", "mlstm_m", "mlstm_conv"),
                                 (k, v, c, n, mm, cv)):
                new[name].append(val)
        x = _ffn(x, P["norm_ffn2"], P["ffn2_w_in"], P["ffn2_w_out"], l)
    return x.reshape(b, seq, D_MODEL), {k: jnp.stack(v).astype(F32) for k, v in new.items()}


def kernel(x_prompt, x_sample, cache_mla, state_hgrn, cache_moba_k, cache_moba_v, state_mlstm_c, state_mlstm_n, state_mlstm_m, state_mlstm_conv, page_table, norm_ffn1, norm_mix, norm_ffn2, ffn1_w_in, ffn1_w_out, ffn2_w_in, ffn2_w_out, even_w_in, even_w_out, mla_q_norm, mla_w_uq, mla_kv_norm, mla_w_uk, mla_w_uv, mla_qn_norm, mla_qr_norm, mla_kn_norm, mla_kr_norm, hgrn_lb_raw, hgrn_out_norm, odd_w_in, odd_w_out, moba_q_norm, moba_k_norm, mlstm_conv_w, mlstm_conv_b, mlstm_b_i, mlstm_b_f, mlstm_out_norm):
    P = _prepare(norm_ffn1, norm_mix, norm_ffn2, ffn1_w_in, ffn1_w_out, ffn2_w_in, ffn2_w_out, even_w_in,
                 even_w_out, mla_q_norm, mla_w_uq, mla_kv_norm, mla_w_uk, mla_w_uv, mla_qn_norm, mla_qr_norm,
                 mla_kn_norm, mla_kr_norm, hgrn_lb_raw, hgrn_out_norm, odd_w_in, odd_w_out, moba_q_norm,
                 moba_k_norm, mlstm_conv_w, mlstm_conv_b, mlstm_b_i, mlstm_b_f, mlstm_out_norm)
    n_pool = cache_moba_k.shape[1]
    page_t = lambda c: jnp.transpose(c, (0, 1, 3, 4, 2)).reshape(-1, n_pool, MB_KVH * MB_DH, PAGE)
    ctx = dict(page_table=page_table.astype(I32), cache_mla_t=jnp.swapaxes(cache_mla, 2, 3), state_hgrn=state_hgrn,
               cache_moba_kt=page_t(cache_moba_k), cache_moba_vt=page_t(cache_moba_v), state_mlstm_c=state_mlstm_c,
               state_mlstm_n=state_mlstm_n, state_mlstm_m=state_mlstm_m, state_mlstm_conv=state_mlstm_conv)
    y_p, sp = _trunk(x_prompt, jnp.arange(x_prompt.shape[1]), P, None)
    past_len = page_table.shape[1] * PAGE
    y_s, ss = _trunk(x_sample, past_len + jnp.arange(x_sample.shape[1]), P, ctx)
    names = ("mla", "hgrn", "moba_k", "moba_v", "mlstm_c", "mlstm_n", "mlstm_m", "mlstm_conv")
    out = [y_p, y_s]
    for name in names:
        out += [sp[name], ss[name]]
    return tuple(out)
```

```python
import functools

import jax
import jax.numpy as jnp
from jax import lax
from jax.experimental import pallas as pl
from jax.experimental.pallas import tpu as pltpu

F32, BF16, I32 = jnp.float32, jnp.bfloat16, jnp.int32
HI = lax.Precision.HIGHEST
EPS = 1e-6
NEG = -1e30
LOG2E = 1.4426950408889634

D_MODEL = 1024
FFN_HIDDEN = 2048
PAGE = 128
MLA_H, MLA_QR, MLA_R, MLA_DN, MLA_DR, MLA_DV = 8, 256, 128, 64, 32, 64
ROPE_THETA = 10000.0
MLA_SCALE = (MLA_DN + MLA_DR) ** -0.5 * LOG2E
HG_H, HG_DK, HG_DV = 4, 128, 128
HG_W = HG_H * HG_DK
MB_H, MB_KVH, MB_DH, MB_BLOCK, MB_TOPK = 8, 4, 64, 256, 3
MB_SCALE = MB_DH ** -0.5 * LOG2E
ML_H, ML_DK, ML_DV, CONV_W = 4, 64, 128, 4
ML_QK = ML_H * ML_DK

LANES = 128
VMEM_LIMIT = 56 << 20


def _dot(a, b):
    return jnp.dot(a, b, preferred_element_type=F32)


def _dot_nt(a, b):
    return lax.dot_general(a, b, (((1,), (1,)), ((), ())), preferred_element_type=F32)


def _dot_tn(a, b):
    return lax.dot_general(a, b, (((0,), (0,)), ((), ())), preferred_element_type=F32)


def _split3(x):
    hi = x.astype(BF16)
    r = x - hi.astype(F32)
    mid = r.astype(BF16)
    return hi, mid, (r - mid.astype(F32)).astype(BF16)


def _rms(x, g):
    return x * lax.rsqrt(jnp.mean(x * x, axis=-1, keepdims=True) + EPS) * g


def _sigmoid(x):
    return 1.0 / (1.0 + jnp.exp(-x))


def _silu(x):
    return x * _sigmoid(x)


def _rows(tm, n):
    return pl.BlockSpec((tm, n), lambda i: (i, 0))


def _layer(shape, l):
    zeros = (0,) * len(shape)
    return pl.BlockSpec((None,) + tuple(shape), lambda *_: (l,) + zeros)


def _const(shape):
    zeros = (0,) * len(shape)
    return pl.BlockSpec(tuple(shape), lambda *_: zeros)


def _params(*sem):
    return pltpu.CompilerParams(dimension_semantics=sem, vmem_limit_bytes=VMEM_LIMIT)


def _row_tile(m, cap=256):
    return min(cap, m)


def _ffn_body(x_ref, g_ref, win_ref, wout_ref, o_ref, *, fc):
    x = x_ref[...]
    hn = _rms(x, g_ref[...]).astype(BF16)
    acc = jnp.zeros_like(x)
    for c in range(FFN_HIDDEN // fc):
        a = _dot(hn, win_ref[:, c * fc:(c + 1) * fc])
        b = _dot(hn, win_ref[:, FFN_HIDDEN + c * fc:FFN_HIDDEN + (c + 1) * fc])
        acc = acc + _dot((_silu(a) * b).astype(BF16), wout_ref[c * fc:(c + 1) * fc, :])
    o_ref[...] = x + 0.5 * acc


def _ffn(x, g, w_in, w_out, l):
    m = x.shape[0]
    tm = _row_tile(m, 512)
    return pl.pallas_call(
        functools.partial(_ffn_body, fc=512),
        grid=(m // tm,),
        in_specs=[_rows(tm, D_MODEL), _layer((1, D_MODEL), l),
                  _layer((D_MODEL, 2 * FFN_HIDDEN), l), _layer((FFN_HIDDEN, D_MODEL), l)],
        out_specs=_rows(tm, D_MODEL),
        out_shape=jax.ShapeDtypeStruct((m, D_MODEL), F32),
        compiler_params=_params("parallel"),
        name="ffn",
    )(x, g, w_in, w_out)


EV_COLS = 256 + 128 + 128 + 4 * HG_W


def _even_in_body(x_ref, g_ref, w_ref, qg_ref, wuq_ref, gq_ref, kvg_ref, wuk_ref, gk_ref, gkr_ref, bseg_ref,
                  cos_ref, sa_ref, sb_ref, llb_ref, l1m_ref, omlb_ref,
                  qcat_ref, kcat_ref, rows_ref, vlat_ref, hq_ref, hk_ref, hlf_ref, hi_ref, hg_ref):
    hn = _rms(x_ref[...], g_ref[...]).astype(BF16)
    cos_t, sin_a, sin_b = cos_ref[...], sa_ref[...], sb_ref[...]
    bseg = bseg_ref[...]

    def rope(t):
        return t * cos_t + pltpu.roll(t, 16, 1) * sin_a + pltpu.roll(t, LANES - 16, 1) * sin_b

    def segnorm(t, gain):
        ms = _dot((t * t).astype(BF16), bseg)
        return t * lax.rsqrt(ms + EPS) * gain

    ql = _dot(hn, w_ref[:, 0:256])
    qh = _dot(_rms(ql, qg_ref[...]).astype(BF16), wuq_ref[...])
    gq = gq_ref[...]
    for h in range(MLA_H):
        t = segnorm(qh[:, h * LANES:(h + 1) * LANES], gq)
        qcat_ref[:, h * LANES:(h + 1) * LANES] = (rope(t) * MLA_SCALE).astype(BF16)

    cn = _rms(_dot(hn, w_ref[:, 256:384]), kvg_ref[...])
    krl = _dot(hn, w_ref[:, 384:512])
    ms = jnp.sum(krl * krl, axis=-1, keepdims=True) * (1.0 / MLA_DR)
    krr = rope(krl * lax.rsqrt(ms + EPS) * gkr_ref[...])
    rows_ref[:, 0:MLA_R] = cn
    rows_ref[:, MLA_R:MLA_R + MLA_DR] = pltpu.roll(krr, 64, 1)[:, 0:MLA_DR]
    cnb = cn.astype(BF16)
    vlat_ref[...] = cnb
    kn = _dot(cnb, wuk_ref[...])
    gk = gk_ref[...]
    for h in range(MLA_H):
        t = segnorm(kn[:, h * LANES:(h + 1) * LANES], gk)
        kcat_ref[:, h * LANES:(h + 1) * LANES] = (t + krr).astype(BF16)

    o = 512
    hq_ref[...] = _silu(_dot(hn, w_ref[:, o:o + HG_W])).astype(hq_ref.dtype)
    hf = _dot(hn, w_ref[:, o + HG_W:o + 2 * HG_W])
    logsig = jnp.minimum(hf, 0.0) - jnp.log(1.0 + jnp.exp(-jnp.abs(hf)))
    aa, bb = llb_ref[...], l1m_ref[...] + logsig
    hlf_ref[...] = jnp.maximum(aa, bb) + jnp.log(1.0 + jnp.exp(-jnp.abs(aa - bb)))
    hk_ref[...] = omlb_ref[...] * _sigmoid(-hf)
    hi_ref[...] = _dot(hn, w_ref[:, o + 2 * HG_W:o + 3 * HG_W]).astype(hi_ref.dtype)
    hg_ref[...] = _silu(_dot(hn, w_ref[:, o + 3 * HG_W:o + 4 * HG_W])).astype(hg_ref.dtype)


def _even_in(x, P, e, l, tabs):
    m = x.shape[0]
    tm = _row_tile(m)
    cos_t, sin_a, sin_b = tabs
    nt = cos_t.shape[0] // tm
    tab = pl.BlockSpec((tm, LANES), lambda i: (i % nt, 0))
    outs = [(8 * LANES, BF16), (8 * LANES, BF16), (MLA_R + MLA_DR, F32), (MLA_R, BF16),
            (HG_W, BF16), (HG_W, F32), (HG_W, F32), (HG_W, BF16), (HG_W, BF16)]
    return pl.pallas_call(
        _even_in_body,
        grid=(m // tm,),
        in_specs=[_rows(tm, D_MODEL), _layer((1, D_MODEL), l), _layer((D_MODEL, EV_COLS), e),
                  _layer((1, MLA_QR), e), _layer((MLA_QR, 8 * LANES), e), _layer((1, LANES), e),
                  _layer((1, MLA_R), e), _layer((MLA_R, 8 * LANES), e), _layer((1, LANES), e),
                  _layer((1, LANES), e), _const((LANES, LANES)), tab, tab, tab,
                  _layer((1, HG_W), e), _layer((1, HG_W), e), _layer((1, HG_W), e)],
        out_specs=[_rows(tm, n) for n, _ in outs],
        out_shape=[jax.ShapeDtypeStruct((m, n), dt) for n, dt in outs],
        compiler_params=_params("parallel"),
        name="even_in",
    )(x, P["norm_mix"], P["even_w"], P["mla_q_norm"], P["mla_wuq"], P["mla_gq"], P["mla_kv_norm"],
      P["mla_wuk"], P["mla_gk"], P["mla_gkr"], P["bseg"], cos_t, sin_a, sin_b,
      P["hg_loglb"], P["hg_log1mlb"], P["hg_1mlb"])


def _mla_p_body(q_ref, k_ref, v_ref, wuv_ref, a_ref, m_scr, l_scr, acc_scr, *, tq):
    i = pl.program_id(1)
    row = lax.broadcasted_iota(I32, (tq, tq), 0)
    col = lax.broadcasted_iota(I32, (tq, tq), 1)
    causal = col <= row

    def step(j, diag):
        start = pl.multiple_of(j * tq, tq)
        v = v_ref[pl.ds(start, tq), :]
        for h in range(MLA_H):
            q = q_ref[:, h * LANES:(h + 1) * LANES]
            k = k_ref[pl.ds(start, tq), h * LANES:(h + 1) * LANES]
            s = _dot_nt(q, k)
            if diag:
                s = jnp.where(causal, s, NEG)
                m_new = jnp.broadcast_to(jnp.max(s, axis=-1, keepdims=True), (tq, LANES))
            else:
                m_old = m_scr[h]
                m_new = jnp.maximum(m_old, jnp.max(s, axis=-1, keepdims=True))
                alpha = jnp.exp2(m_old - m_new)
            p = jnp.exp2(s - jnp.concatenate([m_new] * (tq // LANES), axis=-1))
            l_new = jnp.broadcast_to(jnp.sum(p, axis=-1, keepdims=True), (tq, LANES))
            acc_new = _dot(p.astype(BF16), v)
            l_scr[h] = l_new if diag else alpha * l_scr[h] + l_new
            acc_scr[h] = acc_new if diag else alpha * acc_scr[h] + acc_new
            m_scr[h] = m_new

    def body(j, c):
        step(j, False)
        return c

    step(i, True)
    lax.fori_loop(0, i, body, 0)
    for hp in range(MLA_H // 2):
        pair = [_dot((acc_scr[h] / l_scr[h]).astype(BF16), wuv_ref[h]) for h in (2 * hp, 2 * hp + 1)]
        a_ref[:, hp * LANES:(hp + 1) * LANES] = (pair[0] + pair[1]).astype(a_ref.dtype)


def _mla_prompt(qcat, kcat, vlat, wuv, e, b, l):
    tq = min(512, l)
    assert tq % LANES == 0 and l % tq == 0
    nq = l // tq
    stat = pltpu.VMEM((MLA_H, tq, LANES), F32)
    return pl.pallas_call(
        functools.partial(_mla_p_body, tq=tq),
        grid=(b, nq),
        in_specs=[pl.BlockSpec((tq, 8 * LANES), lambda bi, i: (bi * nq + i, 0)),
                  pl.BlockSpec((l, 8 * LANES), lambda bi, i: (bi, 0)),
                  pl.BlockSpec((l, MLA_R), lambda bi, i: (bi, 0)),
                  _layer((MLA_H, MLA_R, LANES), e)],
        out_specs=pl.BlockSpec((tq, MLA_H * MLA_DV), lambda bi, i: (bi * nq + i, 0)),
        out_shape=jax.ShapeDtypeStruct((b * l, MLA_H * MLA_DV), BF16),
        scratch_shapes=[stat, stat, stat],
        compiler_params=_params("parallel", "arbitrary"),
        name="mla_prompt",
    )(qcat, kcat, vlat, wuv)


MLA_PG = 16


def _start_all(copies):
    for cp in copies:
        cp.start()


def _wait_all(copies):
    for cp in copies:
        cp.wait()


def _mla_s_body(pt_ref, qbd_ref, qr_ref, qself_ref, kself_ref, vself_ref, wukt_ref, wuv_ref, cache_hbm,
                a_ref, buf, lhs_scr, sem, *, n_pages, e):
    b = pl.program_id(0)
    slot = lax.rem(b, 2)
    hd = MLA_H * MLA_DN

    def page_copies(seq, slot_):
        return [pltpu.make_async_copy(cache_hbm.at[e, pt_ref[seq * n_pages + p]], buf.at[slot_, p], sem.at[slot_])
                for p in range(n_pages)]

    @pl.when(b == 0)
    def _():
        _start_all(page_copies(0, 0))

    @pl.when(b + 1 < pl.num_programs(0))
    def _():
        _start_all(page_copies(b + 1, 1 - slot))

    wukt = wukt_ref[...]
    lhs_scr[0:hd, :] = wukt
    qa = _dot(qbd_ref[...].astype(BF16), wukt)
    lhs_scr[hd:hd + 16, :] = jnp.concatenate([qa, jnp.zeros_like(qa)], axis=0).astype(BF16)
    lhs = lhs_scr[...]
    qrb = qr_ref[...].astype(BF16)
    _wait_all(page_copies(b, slot))

    def partial_softmax(g):
        pgs = [buf.at[slot, g * MLA_PG + p] for p in range(MLA_PG)]
        ct = jnp.concatenate([p[0:MLA_R, :] for p in pgs], axis=1).astype(BF16)
        krt = jnp.concatenate([p[MLA_R:MLA_R + MLA_DR, :] for p in pgs], axis=1).astype(BF16)
        keys = ct.shape[1]
        both = _dot(lhs, ct)
        knt = both[0:hd, :]
        ss = jnp.sum((knt * knt).reshape(MLA_H, MLA_DN, keys), axis=1)
        rinv = lax.rsqrt(ss * (1.0 / MLA_DN) + EPS)
        s = rinv * both[hd:hd + MLA_H, :] + _dot(qrb, krt)
        m = jnp.max(s, axis=-1, keepdims=True)
        p = jnp.exp2(s - m)
        return m, jnp.sum(p, axis=-1, keepdims=True), _dot_nt(p.astype(BF16), ct)

    parts = [partial_softmax(g) for g in range(n_pages // MLA_PG)]
    s_self = jnp.sum(qself_ref[...].astype(F32) * kself_ref[...].astype(F32), axis=-1, keepdims=True)
    m_new = s_self
    for m, _, _ in parts:
        m_new = jnp.maximum(m_new, m)
    p_self = jnp.exp2(s_self - m_new)
    l = p_self
    acc = p_self * vself_ref[...].astype(F32)
    for m, l_g, acc_g in parts:
        w = jnp.exp2(m - m_new)
        l = l + w * l_g
        acc = acc + w * acc_g
    r = _dot((acc / l).astype(BF16), wuv_ref[...])
    rh = lax.broadcasted_iota(I32, r.shape, 0)
    ch = lax.broadcasted_iota(I32, r.shape, 1) // MLA_DV
    a_ref[...] = jnp.sum(jnp.where(rh == ch, r, 0.0), axis=0, keepdims=True).astype(a_ref.dtype)


def _mla_sample(page_table, cache_t, e, qbd, qr, qself, kself, vself, wukt, wuv):
    bs = qbd.shape[0]
    n_pages = page_table.shape[1]
    assert n_pages % MLA_PG == 0
    width = MLA_R + MLA_DR
    hd = MLA_H * MLA_DN
    seq3 = lambda n2, n3: pl.BlockSpec((None, n2, n3), lambda b, pt: (b, 0, 0))
    grid_spec = pltpu.PrefetchScalarGridSpec(
        num_scalar_prefetch=1,
        grid=(bs,),
        in_specs=[seq3(MLA_H, hd), seq3(MLA_H, MLA_DR), seq3(MLA_H, LANES), seq3(MLA_H, LANES),
                  seq3(1, MLA_R),
                  pl.BlockSpec((None, hd, MLA_R), lambda b, pt: (e, 0, 0)),
                  pl.BlockSpec((None, MLA_R, MLA_H * MLA_DV), lambda b, pt: (e, 0, 0)),
                  pl.BlockSpec(memory_space=pl.ANY)],
        out_specs=pl.BlockSpec((None, 1, MLA_H * MLA_DV), lambda b, pt: (b, 0, 0)),
        scratch_shapes=[pltpu.VMEM((2, n_pages, width, PAGE), F32), pltpu.VMEM((hd + 16, MLA_R), BF16),
                        pltpu.SemaphoreType.DMA((2,))],
    )
    out = pl.pallas_call(
        functools.partial(_mla_s_body, n_pages=n_pages, e=e),
        grid_spec=grid_spec,
        out_shape=jax.ShapeDtypeStruct((bs, 1, MLA_H * MLA_DV), BF16),
        compiler_params=_params("arbitrary"),
        name="mla_sample",
    )(page_table.reshape(-1), qbd, qr, qself, kself, vself, wukt, wuv, cache_t)
    return out.reshape(bs, MLA_H * MLA_DV)


HG_BS = 32
SCAN_GROUP = 8


def _hgrn_body(q_ref, k_ref, lf_ref, v_ref, g_ref, s0_ref, gain_ref, wind_ref, tri_ref,
               o_ref, sout_ref, st_scr, z_scr, *, c_len, bs, nc, grp):
    c = pl.program_id(1)
    nb = c_len // bs
    hc = HG_H * c_len
    pk = 16

    @pl.when(c == 0)
    def _():
        z_scr[...] = jnp.zeros(z_scr.shape, BF16)
        for gi in range(grp):
            for h in range(HG_H):
                st_scr[gi, h] = s0_ref[gi, h].T

    tri = tri_ref[...]
    gain = gain_ref[...]
    wind = wind_ref[...]
    row = lax.broadcasted_iota(I32, (c_len, c_len), 0)
    col = lax.broadcasted_iota(I32, (c_len, c_len), 1)
    keep = (col <= row) & ((row >= bs) == (col >= bs))
    late = lax.broadcasted_iota(I32, (c_len, HG_W), 0) >= bs
    for gi in range(grp):
        b = sum(_dot(tri, t) for t in _split3(lf_ref[gi]))
        q = q_ref[gi].astype(F32)
        k = k_ref[gi]
        v = v_ref[gi]
        b_last = b[c_len - 1:c_len, :]
        qe = (q * jnp.exp(b)).astype(BF16)
        kd = (k * jnp.exp(b_last - b)).astype(BF16)
        if nb == 2:
            ref = b[bs - 1:bs, :]
            q_off = jnp.where(late, q * jnp.exp(jnp.minimum(b - ref, 0.0)), 0.0).astype(BF16)
            k_off = jnp.where(late, 0.0, k * jnp.exp(jnp.minimum(ref - b, 0.0))).astype(BF16)
        early = []
        for h in range(HG_H):
            hs = slice(h * LANES, (h + 1) * LANES)
            st = st_scr[gi, h]
            o_h = _dot_nt(qe[:, hs], st.astype(BF16))
            a_off = _dot_nt(q_off[:, hs], k_off[:, hs]) if nb == 2 else None
            st_scr[gi, h] = jnp.exp(b_last[:, hs]) * st + _dot_tn(v[:, hs], kd[:, hs])
            early.append((o_h, a_off))
        b3, q3, k3 = (t.reshape(nb, bs, HG_W) for t in (b * LOG2E, q, k))
        for s in range(bs):
            t0 = (s // pk) * pk
            z = q3[:, t0:, :] * k3[:, s:s + 1, :] * jnp.exp2(jnp.minimum(b3[:, t0:, :] - b3[:, s:s + 1, :], 0.0))
            z = z.astype(BF16)
            for ib in range(nb):
                for h in range(HG_H):
                    r0 = gi * hc + h * c_len + ib * bs
                    z_scr[r0 + t0:r0 + bs, s * LANES:(s + 1) * LANES] = z[ib, :, h * LANES:(h + 1) * LANES]
        g = g_ref[gi].astype(F32)
        for h in range(HG_H):
            hs = slice(h * LANES, (h + 1) * LANES)
            o_h, a_off = early[h]
            r0 = gi * hc + h * c_len
            a = jnp.where(keep, _dot(z_scr[r0:r0 + c_len, :], wind), 0.0)
            if nb == 2:
                a = a + a_off
            o_h = o_h + _dot(a.astype(BF16), v[:, hs])
            o_ref[gi, :, hs] = (_rms(o_h, gain[:, hs]) * g[:, hs]).astype(o_ref.dtype)

    @pl.when(c == nc - 1)
    def _():
        for gi in range(grp):
            for h in range(HG_H):
                sout_ref[gi, h] = st_scr[gi, h].T


def _hgrn_scan(hq, hk, hlf, hi, hg, s0, gain, b, l, c_len):
    bs = min(HG_BS, c_len)
    nc = l // c_len
    nb = c_len // bs
    assert nb in (1, 2) and c_len == nb * bs
    grp = SCAN_GROUP if b % SCAN_GROUP == 0 else 1
    chunk = lambda: pl.BlockSpec((grp, c_len, HG_W), lambda bi, ci: (bi, ci, 0))
    state = lambda: pl.BlockSpec((grp, HG_H, HG_DK, HG_DV), lambda bi, ci: (bi, 0, 0, 0))
    wind = jnp.tile(jnp.repeat(jnp.eye(bs, dtype=BF16), LANES, axis=0), (1, nb))
    tri = jnp.tril(jnp.ones((c_len, c_len), BF16))
    seq = lambda t: t.reshape(b, l, HG_W)
    o, s_new = pl.pallas_call(
        functools.partial(_hgrn_body, c_len=c_len, bs=bs, nc=nc, grp=grp),
        grid=(b // grp, nc),
        in_specs=[chunk(), chunk(), chunk(), chunk(), chunk(), state(), _const((1, HG_W)),
                  _const((bs * LANES, c_len)), _const((c_len, c_len))],
        out_specs=[chunk(), state()],
        out_shape=[jax.ShapeDtypeStruct((b, l, HG_W), BF16),
                   jax.ShapeDtypeStruct((b, HG_H, HG_DK, HG_DV), F32)],
        scratch_shapes=[pltpu.VMEM((grp, HG_H, HG_DV, HG_DK), F32),
                        pltpu.VMEM((grp * HG_H * c_len, bs * LANES), BF16)],
        compiler_params=_params("parallel", "arbitrary"),
        name="hgrn_scan",
    )(seq(hq), seq(hk), seq(hlf), seq(hi), seq(hg), s0, gain, wind, tri)
    return o.reshape(b * l, HG_W), s_new


def _out_body(x_ref, a_ref, o_ref, w_ref, y_ref):
    half = a_ref.shape[1]
    y_ref[...] = x_ref[...] + _dot(a_ref[...], w_ref[0:half, :]) + _dot(o_ref[...], w_ref[half:2 * half, :])


def _out_proj(x, a, o, w, e):
    m = x.shape[0]
    tm = _row_tile(m, 512)
    half = a.shape[1]
    return pl.pallas_call(
        _out_body,
        grid=(m // tm,),
        in_specs=[_rows(tm, D_MODEL), _rows(tm, half), _rows(tm, half), _layer((2 * half, D_MODEL), e)],
        out_specs=_rows(tm, D_MODEL),
        out_shape=jax.ShapeDtypeStruct((m, D_MODEL), F32),
        compiler_params=_params("parallel"),
        name="out_proj",
    )(x, a, o, w)


OD_Q, OD_K, OD_V = 0, 8 * LANES, 12 * LANES
OD_XQK, OD_XV, OD_OG, OD_G = 16 * LANES, 20 * LANES, 24 * LANES, 28 * LANES
OD_COLS = 29 * LANES


def _odd_in_body(x_ref, g_ref, w_ref, gq_ref, gk_ref, gb_ref,
                 qpad_ref, kpad_ref, vpad_ref, xqk_ref, xv_ref, og_ref, gates_ref, *state_t):
    hn = _rms(x_ref[...], g_ref[...]).astype(BF16)

    def headnorm(t, gain):
        ms = jnp.sum(t * t, axis=-1, keepdims=True) * (1.0 / MB_DH)
        return t * lax.rsqrt(ms + EPS) * gain

    gq, gk = gq_ref[...], gk_ref[...]
    for h in range(MB_H):
        t = _dot(hn, w_ref[:, OD_Q + h * LANES:OD_Q + (h + 1) * LANES])
        qpad_ref[:, h * LANES:(h + 1) * LANES] = (headnorm(t, gq) * MB_SCALE).astype(BF16)
    for h in range(MB_KVH):
        t = _dot(hn, w_ref[:, OD_K + h * LANES:OD_K + (h + 1) * LANES])
        kn = headnorm(t, gk)
        kpad_ref[:, h * LANES:(h + 1) * LANES] = kn
        if state_t:
            state_t[0][h] = kn.T[0:MB_DH, :]
    v = _dot(hn, w_ref[:, OD_V:OD_V + 4 * LANES])
    vpad_ref[...] = v
    if state_t:
        for h in range(MB_KVH):
            state_t[1][h] = v[:, h * LANES:(h + 1) * LANES].T[0:MB_DH, :]
    xqk_ref[...] = _dot(hn, w_ref[:, OD_XQK:OD_XQK + 4 * LANES])
    xv_ref[...] = _dot(hn, w_ref[:, OD_XV:OD_XV + 4 * LANES]).astype(xv_ref.dtype)
    og_ref[...] = _sigmoid(_dot(hn, w_ref[:, OD_OG:OD_OG + 4 * LANES])).astype(og_ref.dtype)
    gt = _dot(hn, w_ref[:, OD_G:OD_G + LANES]) + gb_ref[...]
    lane = lax.broadcasted_iota(I32, gt.shape, 1)
    logsig = jnp.minimum(gt, 0.0) - jnp.log(1.0 + jnp.exp(-jnp.abs(gt)))
    gates_ref[...] = jnp.where(lane < ML_H, gt, logsig)


def _odd_in(x, P, j, l, seq=None):
    m = x.shape[0]
    tm = _row_tile(m)
    outs = [(8 * LANES, BF16), (4 * LANES, F32), (4 * LANES, F32), (2 * ML_QK, F32),
            (ML_H * ML_DV, BF16), (ML_H * ML_DV, BF16), (LANES, F32)]
    out_specs = [_rows(tm, n) for n, _ in outs]
    out_shape = [jax.ShapeDtypeStruct((m, n), dt) for n, dt in outs]
    if seq is not None:
        assert seq % tm == 0 and tm % LANES == 0
        nt = seq // tm
        for _ in range(2):
            out_specs.append(pl.BlockSpec((None, MB_KVH, MB_DH, tm), lambda i: (i // nt, 0, 0, i % nt)))
            out_shape.append(jax.ShapeDtypeStruct((m // seq, MB_KVH, MB_DH, seq), F32))
    return pl.pallas_call(
        _odd_in_body,
        grid=(m // tm,),
        in_specs=[_rows(tm, D_MODEL), _layer((1, D_MODEL), l), _layer((D_MODEL, OD_COLS), j),
                  _layer((1, LANES), j), _layer((1, LANES), j), _layer((1, LANES), j)],
        out_specs=out_specs,
        out_shape=out_shape,
        compiler_params=_params("parallel"),
        name="odd_in",
    )(x, P["norm_mix"], P["odd_w"], P["moba_gq"], P["moba_gk"], P["mlstm_gb"])


def _moba_p_body(q_ref, k_ref, v_ref, place_ref, a_ref, kmt_scr, qa_scr, m_scr, l_scr, acc_scr, *, nb):
    i = pl.program_id(1)
    blk = MB_BLOCK
    row = lax.broadcasted_iota(I32, (blk, blk), 0)
    col = lax.broadcasted_iota(I32, (blk, blk), 1)
    causal = col <= row
    lane = lax.broadcasted_iota(I32, (blk, LANES), 1)

    @pl.when(i == 0)
    def _():
        kmean = jnp.mean(k_ref[...].reshape(nb, blk, MB_KVH * LANES), axis=1)
        kmt_scr[...] = jnp.zeros(kmt_scr.shape, F32)
        hi = kmean.astype(BF16).astype(F32)
        r1 = kmean - hi
        mid = r1.astype(BF16).astype(F32)
        lo = r1 - mid
        for t, part in enumerate((hi, mid, lo)):
            for h in range(MB_H):
                g = h // MB_GRP
                r0 = (t * MB_H + h) * nb
                kmt_scr[r0:r0 + nb, h * LANES:(h + 1) * LANES] = part[:, g * LANES:(g + 1) * LANES]

    q_all = q_ref[...]
    g3 = _dot_nt(kmt_scr[...].astype(BF16), q_all)
    hn = MB_H * nb
    gate = (g3[0:hn] + g3[hn:2 * hn] + g3[2 * hn:3 * hn]).reshape(MB_H, nb, blk)
    q_all = q_all.astype(F32)
    blk_id = lax.broadcasted_iota(I32, (1, nb, blk), 1)
    cand = jnp.where(lax.broadcasted_iota(I32, (1, nb, 1), 1) < i, 1.0, 0.0)
    rank = jnp.zeros((MB_H, nb, blk), F32)
    for c in range(nb - 1):
        gc = gate[:, c:c + 1, :]
        better = (gc > gate) | ((gc == gate) & (c < blk_id))
        rank = rank + jnp.where(better, cand[:, c:c + 1, :], 0.0)
    bias = jnp.where(rank < MB_TOPK, cand, 0.0) * (-NEG) + NEG
    qa_scr[...] = (q_all + _dot_tn(bias.reshape(MB_H * nb, blk).astype(BF16), place_ref[...])).astype(BF16)

    def step(j, diag):
        start = pl.multiple_of(j * blk, blk)
        for g in range(MB_KVH):
            gs = slice(g * LANES, (g + 1) * LANES)
            kj = k_ref[pl.ds(start, blk), gs]
            if not diag:
                kj = jnp.where(lane == MB_DH + j, 1.0, kj)
            kj = kj.astype(BF16)
            vj = v_ref[pl.ds(start, blk), gs].astype(BF16)
            for h in range(g * (MB_H // MB_KVH), (g + 1) * (MB_H // MB_KVH)):
                s = _dot_nt(qa_scr[:, h * LANES:(h + 1) * LANES], kj)
                if diag:
                    s = jnp.where(causal, s, NEG)
                    m_new = jnp.broadcast_to(jnp.max(s, axis=-1, keepdims=True), (blk, LANES))
                else:
                    m_old = m_scr[h]
                    m_new = jnp.maximum(m_old, jnp.max(s, axis=-1, keepdims=True))
                    alpha = jnp.exp2(m_old - m_new)
                p = jnp.exp2(s - jnp.concatenate([m_new] * (blk // LANES), axis=-1))
                l_new = jnp.broadcast_to(jnp.sum(p, axis=-1, keepdims=True), (blk, LANES))
                acc_new = _dot(p.astype(BF16), vj)
                l_scr[h] = l_new if diag else alpha * l_scr[h] + l_new
                acc_scr[h] = acc_new if diag else alpha * acc_scr[h] + acc_new
                m_scr[h] = m_new

    def body(j, c):
        step(j, False)
        return c

    step(i, True)
    lax.fori_loop(0, i, body, 0)
    for h in range(MB_H):
        a_ref[:, h * MB_DH:(h + 1) * MB_DH] = (acc_scr[h] / l_scr[h])[:, 0:MB_DH].astype(a_ref.dtype)


def _moba_prompt(qpad, kpad, vpad, b, l):
    assert l % MB_BLOCK == 0 and l // MB_BLOCK <= LANES - MB_DH
    nb = l // MB_BLOCK
    stat = pltpu.VMEM((MB_H, MB_BLOCK, LANES), F32)
    hc = jnp.arange(MB_H * nb)
    place = (jnp.arange(MB_H * LANES)[None, :] == ((hc // nb) * LANES + MB_DH + hc % nb)[:, None]).astype(BF16)
    return pl.pallas_call(
        functools.partial(_moba_p_body, nb=nb),
        grid=(b, nb),
        in_specs=[pl.BlockSpec((MB_BLOCK, 8 * LANES), lambda bi, i: (bi * nb + i, 0)),
                  pl.BlockSpec((l, 4 * LANES), lambda bi, i: (bi, 0)),
                  pl.BlockSpec((l, 4 * LANES), lambda bi, i: (bi, 0)),
                  _const((MB_H * nb, MB_H * LANES))],
        out_specs=pl.BlockSpec((MB_BLOCK, MB_H * MB_DH), lambda bi, i: (bi * nb + i, 0)),
        out_shape=jax.ShapeDtypeStruct((b * l, MB_H * MB_DH), BF16),
        scratch_shapes=[pltpu.VMEM((3 * MB_H * nb, MB_H * LANES), F32), pltpu.VMEM((MB_BLOCK, MB_H * LANES), BF16),
                        stat, stat, stat],
        compiler_params=_params("parallel", "arbitrary"),
        name="moba_prompt",
    )(qpad, kpad, vpad, place)


def _moba_sel_body(pt_ref, qsel_ref, ck_hbm, sel_ref, buf, sem, *, n_pages, n_cand, jl):
    b = pl.program_id(0)
    slot = lax.rem(b, 2)
    ppb = MB_BLOCK // PAGE

    def page_copies(seq, slot_):
        return [pltpu.make_async_copy(ck_hbm.at[jl, pt_ref[seq * n_pages + p]], buf.at[slot_, p], sem.at[slot_])
                for p in range(n_pages)]

    @pl.when(b == 0)
    def _():
        _start_all(page_copies(0, 0))

    @pl.when(b + 1 < pl.num_programs(0))
    def _():
        _start_all(page_copies(b + 1, 1 - slot))

    _wait_all(page_copies(b, slot))
    lane = lax.broadcasted_iota(I32, (MB_KVH * MB_DH, LANES), 1)
    kmt = jnp.zeros((MB_KVH * MB_DH, LANES), F32)
    for c in range(n_cand):
        tot = buf[slot, c * ppb]
        for p in range(1, ppb):
            tot = tot + buf[slot, c * ppb + p]
        mean = jnp.sum(tot, axis=-1, keepdims=True) * (1.0 / MB_BLOCK)
        kmt = jnp.where(lane == c, mean, kmt)
    gate = jnp.dot(qsel_ref[...], kmt, precision=HI, preferred_element_type=F32)
    lane_g = lax.broadcasted_iota(I32, gate.shape, 1)
    rank = jnp.zeros(gate.shape, I32)
    for c in range(n_cand):
        gc = gate[:, c:c + 1]
        better = (gc > gate) | ((gc == gate) & (c < lane_g))
        rank = rank + jnp.where(better, 1, 0)
    out_lane = lax.broadcasted_iota(I32, sel_ref.shape, 1)
    out = jnp.zeros(sel_ref.shape, I32)
    for r in range(MB_TOPK):
        idx = jnp.sum(jnp.where((rank == r) & (lane_g < n_cand), lane_g, 0), axis=-1, keepdims=True)
        out = jnp.where(out_lane == r, idx, out)
    sel_ref[...] = out


def _moba_select(page_table, cache_kt, jl, qsel):
    bs = qsel.shape[0]
    n_pages = page_table.shape[1]
    n_cand = n_pages * PAGE // MB_BLOCK
    assert MB_TOPK <= n_cand <= LANES and n_pages * PAGE % MB_BLOCK == 0
    width = MB_KVH * MB_DH
    grid_spec = pltpu.PrefetchScalarGridSpec(
        num_scalar_prefetch=1,
        grid=(bs,),
        in_specs=[pl.BlockSpec((None, MB_H, width), lambda b, pt: (b, 0, 0)), pl.BlockSpec(memory_space=pl.ANY)],
        out_specs=pl.BlockSpec((None, MB_H, LANES), lambda b, pt: (b, 0, 0)),
        scratch_shapes=[pltpu.VMEM((2, n_pages, width, PAGE), F32), pltpu.SemaphoreType.DMA((2,))],
    )
    sel = pl.pallas_call(
        functools.partial(_moba_sel_body, n_pages=n_pages, n_cand=n_cand, jl=jl),
        grid_spec=grid_spec,
        out_shape=jax.ShapeDtypeStruct((bs, MB_H, LANES), I32),
        compiler_params=_params("arbitrary"),
        name="moba_select",
    )(page_table.reshape(-1), qsel, cache_kt)
    return sel[:, :, :MB_TOPK]


MB_GRP = MB_H // MB_KVH


def _moba_s_body(pt_ref, sel_ref, q_ref, kself_ref, vself_ref, ck_hbm, cv_hbm, o_ref, kbuf, vbuf, sem,
                 *, n_pages, jl):
    b = pl.program_id(0)
    slot = lax.rem(b, 2)
    ppb = MB_BLOCK // PAGE
    n = MB_TOPK * ppb

    def tile_copies(seq, slot_):
        out = []
        for h in range(MB_H):
            rows = pl.ds((h // MB_GRP) * MB_DH, MB_DH)
            for r in range(MB_TOPK):
                blk = sel_ref[(seq * MB_H + h) * MB_TOPK + r]
                for p in range(ppb):
                    page = pt_ref[seq * n_pages + blk * ppb + p]
                    i = h * n + r * ppb + p
                    out.append(pltpu.make_async_copy(ck_hbm.at[jl, page, rows], kbuf.at[slot_, i], sem.at[0, slot_]))
                    out.append(pltpu.make_async_copy(cv_hbm.at[jl, page, rows], vbuf.at[slot_, i], sem.at[1, slot_]))
        return out

    @pl.when(b == 0)
    def _():
        _start_all(tile_copies(0, 0))

    @pl.when(b + 1 < pl.num_programs(0))
    def _():
        _start_all(tile_copies(b + 1, 1 - slot))

    _wait_all(tile_copies(b, slot))
    for h in range(MB_H):
        g = h // MB_GRP
        q1 = q_ref[h:h + 1, :]
        kself, vself = kself_ref[g:g + 1, :], vself_ref[g:g + 1, :]
        q = jnp.broadcast_to(q1, (8, MB_DH)).astype(BF16)
        scores = [_dot(q, kbuf[slot, h * n + i].astype(BF16)) for i in range(n)]
        s_self = jnp.sum(q1 * kself, axis=-1, keepdims=True)
        m = s_self
        for s in scores:
            m = jnp.maximum(m, jnp.max(s, axis=-1, keepdims=True))
        p_self = jnp.exp2(s_self - m)
        l = p_self
        acc = p_self * vself
        for i, s in enumerate(scores):
            p = jnp.exp2(s - m)
            l = l + jnp.sum(p, axis=-1, keepdims=True)
            acc = acc + _dot_nt(p.astype(BF16), vbuf[slot, h * n + i].astype(BF16))
        o_ref[h:h + 1, :] = (acc / l)[0:1, :]


def _moba_sample(page_table, sel, cache_kt, cache_vt, jl, q, kself, vself):
    bs = q.shape[0]
    n_pages = page_table.shape[1]
    n_tiles = MB_H * MB_TOPK * (MB_BLOCK // PAGE)
    per_seq = lambda nh: pl.BlockSpec((None, nh, MB_DH), lambda b, pt, sl: (b, 0, 0))
    grid_spec = pltpu.PrefetchScalarGridSpec(
        num_scalar_prefetch=2,
        grid=(bs,),
        in_specs=[per_seq(MB_H), per_seq(MB_KVH), per_seq(MB_KVH),
                  pl.BlockSpec(memory_space=pl.ANY), pl.BlockSpec(memory_space=pl.ANY)],
        out_specs=per_seq(MB_H),
        scratch_shapes=[pltpu.VMEM((2, n_tiles, MB_DH, PAGE), F32), pltpu.VMEM((2, n_tiles, MB_DH, PAGE), F32),
                        pltpu.SemaphoreType.DMA((2, 2))],
    )
    return pl.pallas_call(
        functools.partial(_moba_s_body, n_pages=n_pages, jl=jl),
        grid_spec=grid_spec,
        out_shape=jax.ShapeDtypeStruct((bs, MB_H, MB_DH), F32),
        compiler_params=_params("arbitrary"),
        name="moba_sample",
    )(page_table.reshape(-1), sel.reshape(-1), q, kself, vself, cache_kt, cache_vt)


def _mlstm_body(xqk_ref, xv_ref, og_ref, gcol_ref, grow_ref, cw_ref, cb_ref, prev_ref, c0_ref, m0_ref,
                gain_ref, tri_ref, triu_ref,
                h_ref, cout_ref, mout_ref, convout_ref, full_scr, c_scr, m_scr, *, c_len, nc, l_last, grp):
    c = pl.program_id(1)
    pad = 8
    w = 2 * ML_QK

    @pl.when(c == 0)
    def _():
        full_scr[:, pad - (CONV_W - 1):pad, :] = prev_ref[...]
        c_scr[...] = c0_ref[...]
        m_scr[...] = m0_ref[...]

    row = lax.broadcasted_iota(I32, (c_len, c_len), 0)
    col = lax.broadcasted_iota(I32, (c_len, c_len), 1)
    causal = col <= row
    head_of_lane = lax.broadcasted_iota(I32, (1, ML_QK), 1) // ML_DK
    head_of_row = lax.broadcasted_iota(I32, (ML_QK, 1), 0) // ML_DK
    m_lane = lax.broadcasted_iota(I32, (1, ML_H), 1)
    one_col = jnp.where(lax.broadcasted_iota(I32, (c_len, LANES), 1) == 0, 1.0, 0.0).astype(BF16)
    gain = gain_ref[...]
    tri, triu = tri_ref[...], triu_ref[...]

    for gi in range(grp):
        full_scr[gi, pad:pad + c_len, :] = xqk_ref[gi]
        conv = cb_ref[...]
        for t in range(CONV_W):
            o = pad - (CONV_W - 1) + t
            conv = conv + full_scr[gi, o:o + c_len, :] * cw_ref[t:t + 1, :]

        @pl.when(c == nc - 1)
        def _():
            convout_ref[gi] = full_scr[gi, pad + l_last - (CONV_W - 1):pad + l_last, :]

        full_scr[gi, pad - (CONV_W - 1):pad, :] = full_scr[gi, pad + c_len - (CONV_W - 1):pad + c_len, :]

        qk = _silu(conv)
        qf = qk[:, 0:ML_QK]
        kf = qk[:, ML_QK:w] * (ML_DK ** -0.5)
        kb = kf.astype(BF16)
        gcol, grow = gcol_ref[gi], grow_ref[gi]
        f_col = jnp.dot(tri, gcol, precision=HI, preferred_element_type=F32)
        f_row = jnp.dot(grow, triu, precision=HI, preferred_element_type=F32)
        state = c_scr[gi]
        state_b = state.astype(BF16)
        m_all = m_scr[gi]
        xv = xv_ref[gi]
        og = og_ref[gi].astype(F32)
        decay_col = jnp.zeros((ML_QK, 1), F32)
        upd = jnp.zeros((ML_QK, 2 * LANES), F32)
        m_next = jnp.zeros((1, ML_H), F32)
        for h in range(ML_H):
            fc = f_col[:, ML_H + h:ML_H + h + 1]
            ic = gcol[:, h:h + 1]
            fr = f_row[ML_H + h:ML_H + h + 1, :]
            ir = grow[h:h + 1, :]
            m_prev = m_all[:, h:h + 1]
            dmat = jnp.where(causal, fc - fr + ir, NEG)
            inter = fc + m_prev
            mt = jnp.maximum(jnp.max(dmat, axis=-1, keepdims=True), inter)
            qh = jnp.where(head_of_lane == h, qf, 0.0).astype(BF16)
            wd = jnp.exp(dmat - mt) * _dot_nt(qh, kb)
            a = jnp.exp(inter - mt)
            vaug = jnp.concatenate([xv[:, h * ML_DV:(h + 1) * ML_DV], one_col], axis=-1)
            num = _dot(wd.astype(BF16), vaug) + a * _dot(qh, state_b)
            den = num[:, ML_DV:ML_DV + 1]
            hh = num[:, 0:ML_DV] / jnp.maximum(jnp.abs(den), jnp.exp(-mt))
            hs = slice(h * ML_DV, (h + 1) * ML_DV)
            h_ref[gi, :, hs] = (_rms(hh, gain[:, hs]) * og[:, hs]).astype(h_ref.dtype)
            m_new = mt[c_len - 1:c_len, :]
            f_last = fc[c_len - 1:c_len, :]
            gk = (jnp.where(head_of_lane == h, kf, 0.0) * jnp.exp(f_last - fc + ic - m_new)).astype(BF16)
            upd = upd + _dot_tn(gk, vaug)
            decay_col = decay_col + jnp.where(head_of_row == h, jnp.exp(f_last + m_prev - m_new), 0.0)
            m_next = m_next + jnp.where(m_lane == h, m_new, 0.0)
        c_scr[gi] = decay_col * state + upd
        m_scr[gi] = m_next

    @pl.when(c == nc - 1)
    def _():
        cout_ref[...] = c_scr[...]
        mout_ref[...] = m_scr[...]


def _mlstm_scan(xqk, xv, og, gcol, grow, cw, cb, prev, c0, m0, gain, j, b, l_pad, c_len, l_last):
    nc = l_pad // c_len
    w = 2 * ML_QK
    grp = SCAN_GROUP if b % SCAN_GROUP == 0 else 1
    chunk = lambda n: pl.BlockSpec((grp, c_len, n), lambda bi, ci: (bi, ci, 0))
    per_b = lambda n2, n3: pl.BlockSpec((grp, n2, n3), lambda bi, ci: (bi, 0, 0))
    tri = jnp.tril(jnp.ones((c_len, c_len), F32))
    seq = lambda t: t.reshape(b, l_pad, t.shape[-1])
    h, caug, m_new, conv_new = pl.pallas_call(
        functools.partial(_mlstm_body, c_len=c_len, nc=nc, l_last=l_last, grp=grp),
        grid=(b // grp, nc),
        in_specs=[chunk(w), chunk(ML_H * ML_DV), chunk(ML_H * ML_DV), chunk(LANES),
                  pl.BlockSpec((grp, None, 8, c_len), lambda bi, ci: (bi, ci, 0, 0)),
                  _layer((CONV_W, w), j), _layer((1, w), j), per_b(CONV_W - 1, w),
                  per_b(ML_QK, 2 * LANES), per_b(1, ML_H), _layer((1, ML_H * ML_DV), j),
                  _const((c_len, c_len)), _const((c_len, c_len))],
        out_specs=[chunk(ML_H * ML_DV), per_b(ML_QK, 2 * LANES), per_b(1, ML_H), per_b(CONV_W - 1, w)],
        out_shape=[jax.ShapeDtypeStruct((b, l_pad, ML_H * ML_DV), BF16),
                   jax.ShapeDtypeStruct((b, ML_QK, 2 * LANES), F32),
                   jax.ShapeDtypeStruct((b, 1, ML_H), F32),
                   jax.ShapeDtypeStruct((b, CONV_W - 1, w), F32)],
        scratch_shapes=[pltpu.VMEM((grp, c_len + 8, w), F32), pltpu.VMEM((grp, ML_QK, 2 * LANES), F32),
                        pltpu.VMEM((grp, 1, ML_H), F32)],
        compiler_params=_params("parallel", "arbitrary"),
        name="mlstm_scan",
    )(seq(xqk), seq(xv), seq(og), seq(gcol), grow, cw, cb, prev, c0, m0, gain, tri, tri.T)
    return h.reshape(b * l_pad, ML_H * ML_DV), caug, m_new, conv_new


def _prepare(norm_ffn1, norm_mix, norm_ffn2, ffn1_w_in, ffn1_w_out, ffn2_w_in, ffn2_w_out, even_w_in, even_w_out,
             mla_q_norm, mla_w_uq, mla_kv_norm, mla_w_uk, mla_w_uv, mla_qn_norm, mla_qr_norm, mla_kn_norm,
             mla_kr_norm, hgrn_lb_raw, hgrn_out_norm, odd_w_in, odd_w_out, moba_q_norm, moba_k_norm,
             mlstm_conv_w, mlstm_conv_b, mlstm_b_i, mlstm_b_f, mlstm_out_norm):
    ne, no = even_w_in.shape[0], odd_w_in.shape[0]
    row = lambda t: t[:, None, :].astype(F32)
    P = dict(norm_ffn1=row(norm_ffn1), norm_mix=row(norm_mix), norm_ffn2=row(norm_ffn2),
             ffn1_w_in=ffn1_w_in.astype(BF16), ffn1_w_out=ffn1_w_out.astype(BF16),
             ffn2_w_in=ffn2_w_in.astype(BF16), ffn2_w_out=ffn2_w_out.astype(BF16),
             even_w_out=even_w_out.astype(BF16), odd_w_out=odd_w_out.astype(BF16))
    w = even_w_in
    kr = jnp.zeros((ne, D_MODEL, LANES), F32).at[:, :, 64:96].set(w[:, :, 384:416])
    P["even_w"] = jnp.concatenate([w[:, :, 0:384], kr, w[:, :, 416:]], axis=-1).astype(BF16)
    pad_heads = lambda t, n: jnp.pad(t, ((0, 0), (0, 0), (0, 0), (0, LANES - n))).reshape(ne, t.shape[1], 8 * LANES)
    P["mla_wuq"] = pad_heads(mla_w_uq, MLA_DN + MLA_DR).astype(BF16)
    P["mla_wuk"] = pad_heads(mla_w_uk, MLA_DN).astype(BF16)
    P["mla_wukt"] = jnp.swapaxes(mla_w_uk.reshape(ne, MLA_R, MLA_H * MLA_DN), 1, 2).astype(BF16)
    P["mla_wuv_flat"] = mla_w_uv.reshape(ne, MLA_R, MLA_H * MLA_DV).astype(BF16)
    wuv = jnp.transpose(mla_w_uv, (0, 2, 1, 3))
    odd_head = (jnp.arange(MLA_H) % 2 == 1)[None, :, None, None]
    P["mla_wuv"] = jnp.where(odd_head, jnp.pad(wuv, ((0, 0),) * 3 + ((MLA_DV, 0),)),
                             jnp.pad(wuv, ((0, 0),) * 3 + ((0, MLA_DV),))).astype(BF16)
    z32 = jnp.zeros((ne, LANES - MLA_DN - MLA_DR), F32)
    P["mla_q_norm"], P["mla_kv_norm"] = row(mla_q_norm), row(mla_kv_norm)
    P["mla_gq"] = row(jnp.concatenate([mla_qn_norm, mla_qr_norm, z32], axis=-1))
    P["mla_gk"] = row(jnp.concatenate([mla_kn_norm, jnp.zeros((ne, LANES - MLA_DN), F32)], axis=-1))
    P["mla_gkr"] = row(jnp.concatenate([jnp.zeros((ne, MLA_DN), F32), mla_kr_norm, z32], axis=-1))
    P["mla_kn_gain"] = mla_kn_norm.astype(F32)
    seg = jnp.arange(LANES)
    seg_id = jnp.where(seg < MLA_DN, 0, jnp.where(seg < MLA_DN + MLA_DR, 1, 2 + seg))
    seg_len = jnp.where(seg < MLA_DN, MLA_DN, MLA_DR).astype(F32)
    P["bseg"] = jnp.where(seg_id[:, None] == seg_id[None, :], 1.0 / seg_len[None, :], 0.0).astype(BF16)
    p_lb = jax.nn.softmax(hgrn_lb_raw.astype(F32), axis=0)
    lb = jnp.maximum(jnp.cumsum(p_lb, axis=0) - p_lb[0], 0.0)
    P["hg_loglb"], P["hg_log1mlb"], P["hg_1mlb"] = row(jnp.log(lb)), row(jnp.log1p(-lb)), row(1.0 - lb)
    P["hg_gain"] = row(jnp.tile(hgrn_out_norm, (1, HG_H)))
    w = odd_w_in
    c0 = MB_H * MB_DH
    c1 = c0 + MB_KVH * MB_DH
    c2 = c1 + MB_KVH * MB_DH
    c3 = c2 + 2 * ML_QK
    c4 = c3 + ML_H * ML_DV
    c5 = c4 + 2 * ML_H
    pad_h = lambda t, nh: jnp.pad(t.reshape(no, D_MODEL, nh, MB_DH),
                                  ((0, 0), (0, 0), (0, 0), (0, LANES - MB_DH))).reshape(no, D_MODEL, nh * LANES)
    gates = jnp.pad(w[:, :, c4:c5], ((0, 0), (0, 0), (0, LANES - 2 * ML_H)))
    P["odd_w"] = jnp.concatenate([pad_h(w[:, :, 0:c0], MB_H), pad_h(w[:, :, c0:c1], MB_KVH),
                                  pad_h(w[:, :, c1:c2], MB_KVH), w[:, :, c2:c4], w[:, :, c5:], gates],
                                 axis=-1).astype(BF16)
    zpad = jnp.zeros((no, LANES - MB_DH), F32)
    P["moba_gq"] = row(jnp.concatenate([moba_q_norm, zpad], axis=-1))
    P["moba_gk"] = row(jnp.concatenate([moba_k_norm, zpad], axis=-1))
    P["mlstm_gb"] = row(jnp.concatenate([mlstm_b_i, mlstm_b_f, jnp.zeros((no, LANES - 2 * ML_H), F32)], axis=-1))
    P["mlstm_cw"] = mlstm_conv_w.astype(F32)
    P["mlstm_cb"] = row(mlstm_conv_b)
    P["mlstm_gain"] = row(jnp.tile(mlstm_out_norm, (1, ML_H)))
    return P


def _rope_tables(pos, n_rows):
    half = MLA_DR // 2
    freqs = ROPE_THETA ** (-jnp.arange(half, dtype=F32) / half)
    ang = pos.astype(F32)[:, None] * freqs[None, :]
    cos, sin = jnp.cos(ang), jnp.sin(ang)
    n = pos.shape[0]
    z = lambda k: jnp.zeros((n, k), F32)
    cos_t = jnp.concatenate([jnp.ones((n, MLA_DN), F32), cos, cos, z(LANES - MLA_DN - MLA_DR)], axis=-1)
    sin_a = jnp.concatenate([z(MLA_DN + half), sin, z(LANES - MLA_DN - MLA_DR)], axis=-1)
    sin_b = jnp.concatenate([z(MLA_DN), -sin, z(LANES - MLA_DN - half)], axis=-1)
    rep = n_rows // n
    return tuple(jnp.tile(t, (rep, 1)) for t in (cos_t, sin_a, sin_b))


def _pad_time(t, b, l, l_pad, value=0.0):
    if l_pad == l:
        return t
    n = t.shape[-1]
    fill = jnp.broadcast_to(jnp.asarray(value, t.dtype), (b, l_pad - l, n))
    return jnp.concatenate([t.reshape(b, l, n), fill], axis=1).reshape(b * l_pad, n)


def _even_layer(x, P, e, l, b, seq, tabs, ctx):
    qcat, kcat, rows, vlat, hq, hk, hlf, hi, hg = _even_in(x, P, e, l, tabs)
    if ctx is None:
        a = _mla_prompt(qcat, kcat, vlat, P["mla_wuv"], e, b, seq)
        s0 = jnp.zeros((b, HG_H, HG_DK, HG_DV), F32)
        c_len, l_pad = min(64, seq), seq
    else:
        q3 = qcat.reshape(b, MLA_H, LANES).astype(F32)
        eye = jnp.eye(MLA_H, dtype=F32)
        qbd = jnp.einsum("bhd,hg->bhgd", q3[:, :, :MLA_DN] * P["mla_kn_gain"][e][None, None, :], eye)
        a = _mla_sample(ctx["page_table"], ctx["cache_mla_t"], e, qbd.reshape(b, MLA_H, MLA_H * MLA_DN),
                        q3[:, :, MLA_DN:MLA_DN + MLA_DR], qcat.reshape(b, MLA_H, LANES),
                        kcat.reshape(b, MLA_H, LANES), vlat.reshape(b, 1, MLA_R),
                        P["mla_wukt"], P["mla_wuv_flat"])
        s0 = ctx["state_hgrn"][e].astype(F32)
        c_len = l_pad = 16
    pad = lambda t: _pad_time(t, b, seq, l_pad)
    o, s_new = _hgrn_scan(pad(hq), pad(hk), pad(hlf), pad(hi), pad(hg), s0, P["hg_gain"][e], b, l_pad, c_len)
    if l_pad != seq:
        o = o.reshape(b, l_pad, HG_W)[:, :seq].reshape(b * seq, HG_W)
    y = _out_proj(x, a, o, P["even_w_out"], e)
    return y, rows.reshape(b, seq, MLA_R + MLA_DR), s_new


def _odd_layer(x, P, j, l, b, seq, ctx):
    if ctx is None and seq % _row_tile(b * seq) == 0 and _row_tile(b * seq) % LANES == 0:
        qpad, kpad, vpad, xqk, xv, og, gates, kt, vt = _odd_in(x, P, j, l, seq)
        k_new, v_new = (jnp.transpose(t, (0, 3, 1, 2)) for t in (kt, vt))
    else:
        qpad, kpad, vpad, xqk, xv, og, gates = _odd_in(x, P, j, l)
        unpad = lambda t: t.reshape(b, seq, MB_KVH, LANES)[..., :MB_DH]
        k_new, v_new = unpad(kpad), unpad(vpad)
    if ctx is None:
        a = _moba_prompt(qpad, kpad, vpad, b, seq)
        prev = jnp.zeros((b, CONV_W - 1, 2 * ML_QK), F32)
        caug0 = jnp.zeros((b, ML_QK, 2 * LANES), F32)
        m0 = jnp.zeros((b, 1, ML_H), F32)
        c_len, l_pad = min(64, seq), seq
    else:
        assert seq == 1
        pt = ctx["page_table"]
        qh = qpad.reshape(b, MB_H, LANES)[:, :, :MB_DH].astype(F32)
        on_kv = jax.nn.one_hot(jnp.arange(MB_H) // MB_GRP, MB_KVH, dtype=F32)
        qsel = jnp.einsum("bhd,hg->bhgd", qh, on_kv).reshape(b, MB_H, MB_KVH * MB_DH)
        sel = _moba_select(pt, ctx["cache_moba_kt"], j, qsel)
        a = _moba_sample(pt, sel, ctx["cache_moba_kt"], ctx["cache_moba_vt"], j, qh,
                         k_new.reshape(b, MB_KVH, MB_DH), v_new.reshape(b, MB_KVH, MB_DH))
        a = a.reshape(b, MB_H * MB_DH).astype(BF16)
        prev = ctx["state_mlstm_conv"][j].astype(F32)
        caug0 = jnp.concatenate([ctx["state_mlstm_c"][j].astype(F32),
                                 ctx["state_mlstm_n"][j].astype(F32)[..., None],
                                 jnp.zeros((b, ML_H, ML_DK, LANES - 1), F32)], axis=-1).reshape(b, ML_QK, 2 * LANES)
        m0 = ctx["state_mlstm_m"][j].astype(F32).reshape(b, 1, ML_H)
        c_len = l_pad = 16
    nc = l_pad // c_len
    gate_fill = jnp.where(jnp.arange(LANES) < ML_H, NEG, 0.0)
    gcol = _pad_time(gates, b, seq, l_pad, gate_fill)
    grow = jnp.swapaxes(gcol[:, :8].reshape(b, nc, c_len, 8), 2, 3)
    pad = lambda t: _pad_time(t, b, seq, l_pad)
    l_last = seq - (nc - 1) * c_len
    h, caug, m_new, conv_new = _mlstm_scan(pad(xqk), pad(xv), pad(og), gcol, grow, P["mlstm_cw"], P["mlstm_cb"],
                                           prev, caug0, m0, P["mlstm_gain"], j, b, l_pad, c_len, l_last)
    if l_pad != seq:
        h = h.reshape(b, l_pad, ML_H * ML_DV)[:, :seq].reshape(b * seq, ML_H * ML_DV)
    y = _out_proj(x, a, h, P["odd_w_out"], j)
    caug = caug.reshape(b, ML_H, ML_DK, 2 * LANES)
    return y, k_new, v_new, caug[..., :ML_DV], caug[..., ML_DV], m_new.reshape(b, ML_H), conv_new


def _trunk(x, pos, P, ctx):
    b, seq, _ = x.shape
    m = b * seq
    x = x.reshape(m, D_MODEL).astype(F32)
    depth = P["norm_mix"].shape[0]
    tabs = _rope_tables(pos, max(_row_tile(m), seq))
    new = {k: [] for k in ("mla", "hgrn", "moba_k", "moba_v", "mlstm_c", "mlstm_n", "mlstm_m", "mlstm_conv")}
    for l in range(depth):
        x = _ffn(x, P["norm_ffn1"], P["ffn1_w_in"], P["ffn1_w_out"], l)
        if l % 2 == 0:
            x, rows, s = _even_layer(x, P, l // 2, l, b, seq, tabs, ctx)
            new["mla"].append(rows)
            new["hgrn"].append(s)
        else:
            x, k, v, c, n, mm, cv = _odd_layer(x, P, l // 2, l, b, seq, ctx)
            for name, val in zip(("moba_k", "moba_v", "mlstm_c", "mlstm_n", "mlstm_m", "mlstm_conv"),
                                 (k, v, c, n, mm, cv)):
                new[name].append(val)
        x = _ffn(x, P["norm_ffn2"], P["ffn2_w_in"], P["ffn2_w_out"], l)
    return x.reshape(b, seq, D_MODEL), {k: jnp.stack(v).astype(F32) for k, v in new.items()}


def kernel(x_prompt, x_sample, cache_mla, state_hgrn, cache_moba_k, cache_moba_v, state_mlstm_c, state_mlstm_n, state_mlstm_m, state_mlstm_conv, page_table, norm_ffn1, norm_mix, norm_ffn2, ffn1_w_in, ffn1_w_out, ffn2_w_in, ffn2_w_out, even_w_in, even_w_out, mla_q_norm, mla_w_uq, mla_kv_norm, mla_w_uk, mla_w_uv, mla_qn_norm, mla_qr_norm, mla_kn_norm, mla_kr_norm, hgrn_lb_raw, hgrn_out_norm, odd_w_in, odd_w_out, moba_q_norm, moba_k_norm, mlstm_conv_w, mlstm_conv_b, mlstm_b_i, mlstm_b_f, mlstm_out_norm):
    P = _prepare(norm_ffn1, norm_mix, norm_ffn2, ffn1_w_in, ffn1_w_out, ffn2_w_in, ffn2_w_out, even_w_in,
                 even_w_out, mla_q_norm, mla_w_uq, mla_kv_norm, mla_w_uk, mla_w_uv, mla_qn_norm, mla_qr_norm,
                 mla_kn_norm, mla_kr_norm, hgrn_lb_raw, hgrn_out_norm, odd_w_in, odd_w_out, moba_q_norm,
                 moba_k_norm, mlstm_conv_w, mlstm_conv_b, mlstm_b_i, mlstm_b_f, mlstm_out_norm)
    n_pool = cache_moba_k.shape[1]
    page_t = lambda c: jnp.transpose(c, (0, 1, 3, 4, 2)).reshape(-1, n_pool, MB_KVH * MB_DH, PAGE)
    ctx = dict(page_table=page_table.astype(I32), cache_mla_t=jnp.swapaxes(cache_mla, 2, 3), state_hgrn=state_hgrn,
               cache_moba_kt=page_t(cache_moba_k), cache_moba_vt=page_t(cache_moba_v), state_mlstm_c=state_mlstm_c,
               state_mlstm_n=state_mlstm_n, state_mlstm_m=state_mlstm_m, state_mlstm_conv=state_mlstm_conv)
    y_p, sp = _trunk(x_prompt, jnp.arange(x_prompt.shape[1]), P, None)
    past_len = page_table.shape[1] * PAGE
    y_s, ss = _trunk(x_sample, past_len + jnp.arange(x_sample.shape[1]), P, ctx)
    names = ("mla", "hgrn", "moba_k", "moba_v", "mlstm_c", "mlstm_n", "mlstm_m", "mlstm_conv")
    out = [y_p, y_s]
    for name in names:
        out += [sp[name], ss[name]]
    return tuple(out)
```
